```python
import math, functools
import jax, jax.numpy as jnp
from jax import lax
import numpy as np

D_MODEL = 1024
BATCH = 8
SEQ = 8192
DEPTH = 1

GRID_W = 64
CTX_LEN = 256
D_FOURIER = 512
F_GROUPS = 4
F_GROUP_DIM = D_FOURIER // F_GROUPS
D_RNN = 1024
RNN_HEADS = 8
RNN_BLOCK = D_RNN // RNN_HEADS
CONV_W = 4
CONV_LEFT = 2
LRU_C = 8.0
COL_R0 = D_FOURIER
COL_G0 = COL_R0 + D_RNN
COL_M0 = COL_G0 + D_RNN
D_IN = COL_M0 + 2 * D_MODEL
N_EXPERTS = 32
TOP_K = 4
D_EXPERT = 1024
SWIGLU_ALPHA = 1.702
SWIGLU_LIMIT = 7.0
MOE_BLOCK = 256
EPS = 1e-6

kernel_name = "hybrid_fnet_rglru_moe_dit_block"


def rms_norm(x, g):
    xf = x.astype(jnp.float32)
    y = xf * lax.rsqrt(jnp.mean(xf * xf, axis=-1, keepdims=True) + EPS)
    return (y * g.astype(jnp.float32)).astype(x.dtype)


def ada_modulation(cond, w_ada, b_ada):
    m = jax.nn.silu(cond) @ w_ada + b_ada
    return jnp.split(m, 6, axis=-1)


def centred_conv(u, w, b):
    L = u.shape[-2]
    pad = [(0, 0)] * (u.ndim - 2) + [(CONV_LEFT, CONV_W - 1 - CONV_LEFT), (0, 0)]
    up = jnp.pad(u, pad)
    out = b + up[..., 0:L, :] * w[0]
    for k in range(1, CONV_W):
        out = out + up[..., k:k + L, :] * w[k]
    return out


def fourier_mix(u):
    b_, L, _ = u.shape
    ug = u.astype(jnp.float32).reshape(b_, L, F_GROUPS, F_GROUP_DIM)
    y = jnp.fft.fftn(ug, axes=(1, 3), norm="ortho").real
    return y.reshape(b_, L, D_FOURIER).astype(u.dtype)


def rglru_coeffs(xc, wa, ba, wx, bx, lam):
    b_, L, _ = xc.shape
    xh = xc.reshape(b_, L, RNN_HEADS, RNN_BLOCK)
    r = jax.nn.sigmoid(jnp.einsum('blhi,hij->blhj', xh, wa.astype(jnp.float32)).reshape(b_, L, D_RNN) + ba.astype(jnp.float32))
    i = jax.nn.sigmoid(jnp.einsum('blhi,hij->blhj', xh, wx.astype(jnp.float32)).reshape(b_, L, D_RNN) + bx.astype(jnp.float32))
    log_a = -LRU_C * r * jax.nn.softplus(-lam.astype(jnp.float32))
    a = jnp.exp(log_a)
    bt = jnp.sqrt(-jnp.expm1(2.0 * log_a)) * (i * xc)
    return a, bt


def _combine(e1, e2):
    a1, b1 = e1
    a2, b2 = e2
    return a1 * a2, a2 * b1 + b2


def linear_scan(a, bt, h0, reverse):
    idx = -1 if reverse else 0
    bt = bt.at[:, idx].add(a[:, idx] * h0)
    _, h = lax.associative_scan(_combine, (a, bt), axis=1, reverse=reverse)
    return h


def bidir_rglru(xc, wa, ba, wx, bx, lam, h0s):
    y = None
    finals = []
    for d, rev in enumerate((False, True)):
        a, bt = rglru_coeffs(xc, wa[d], ba[d], wx[d], bx[d], lam[d])
        h = linear_scan(a, bt, h0s[d], rev)
        y = h if y is None else y + h
        finals.append(h[:, 0] if rev else h[:, -1])
    return y, finals


def merge_branches(z, y_rnn, w_f, w_r, w_o):
    f_out = fourier_mix(z[..., :COL_R0]) @ w_f
    r_out = (y_rnn.astype(z.dtype) * jax.nn.gelu(z[..., COL_G0:COL_M0])) @ w_r
    g_f = jax.nn.sigmoid(z[..., COL_M0:COL_M0 + D_MODEL])
    g_r = jax.nn.sigmoid(z[..., COL_M0 + D_MODEL:])
    return (g_f * f_out + g_r * r_out) @ w_o


def moe_ffn(h, w_router, b_router, w_gu, b_gu, w_down, b_down):
    shp = h.shape
    d = shp[-1]
    t = h.reshape(-1, d)
    T = t.shape[0]
    logits = (t @ w_router + b_router).astype(jnp.float32)
    top_v, top_i = lax.top_k(logits, TOP_K)
    wts = jax.nn.softmax(top_v, axis=-1)
    A = T * TOP_K
    flat_e = top_i.reshape(-1).astype(jnp.int32)
    order = jnp.argsort(flat_e, stable=True).astype(jnp.int32)
    sorted_e = flat_e[order]
    counts = jnp.bincount(flat_e, length=N_EXPERTS).astype(jnp.int32)
    padded = (counts + MOE_BLOCK - 1) // MOE_BLOCK * MOE_BLOCK
    start_sorted = jnp.cumsum(counts) - counts
    ends_pad = jnp.cumsum(padded)
    start_pad = ends_pad - padded
    rank = jnp.arange(A, dtype=jnp.int32) - start_sorted[sorted_e]
    dest = start_pad[sorted_e] + rank
    n_blocks = (A + N_EXPERTS * (MOE_BLOCK - 1) + MOE_BLOCK - 1) // MOE_BLOCK
    P = n_blocks * MOE_BLOCK
    slot_tok = jnp.full((P,), T, jnp.int32).at[dest].set(order // TOP_K)
    slot_w = jnp.zeros((P,), jnp.float32).at[dest].set(wts.reshape(-1)[order])
    block_start = jnp.arange(n_blocks, dtype=jnp.int32) * MOE_BLOCK
    block_expert = jnp.minimum(jnp.searchsorted(ends_pad, block_start, side='right'), N_EXPERTS - 1).astype(jnp.int32)
    t_pad = jnp.concatenate([t, jnp.zeros((1, d), t.dtype)], axis=0)

    def expert_block(args):
        idx, wgt, e = args
        xb = t_pad[idx]
        gu = xb @ w_gu[e] + b_gu[e]
        gate = jnp.minimum(gu[:, :D_EXPERT], SWIGLU_LIMIT)
        up = jnp.clip(gu[:, D_EXPERT:], -SWIGLU_LIMIT, SWIGLU_LIMIT)
        act = (up + 1.0) * gate * jax.nn.sigmoid(SWIGLU_ALPHA * gate)
        yb = act @ w_down[e] + b_down[e]
        return yb * wgt[:, None].astype(yb.dtype)

    ys = lax.map(expert_block, (slot_tok.reshape(n_blocks, MOE_BLOCK),
                                slot_w.reshape(n_blocks, MOE_BLOCK), block_expert))
    y = jnp.zeros((T + 1, d), ys.dtype).at[slot_tok].add(ys.reshape(P, d))[:T]
    return y.reshape(shp).astype(h.dtype)


def setup_inputs(seed: int = 0) -> dict:
    key = jax.random.key(seed)
    ks = jax.random.split(key, 32)
    f32 = jnp.float32
    nrm = lambda k, s, sc: jax.random.normal(k, s, f32) * sc
    u = jax.random.uniform(ks[16], (DEPTH, 2, D_RNN), f32, 0.9, 0.999)
    a0 = u ** (1.0 / LRU_C)
    lru_lambda = jnp.log(a0) - jnp.log1p(-a0)
    return {
        "x": nrm(ks[0], (BATCH, SEQ, D_MODEL), 1.0),
        "c": nrm(ks[1], (BATCH, D_MODEL), 1.0),
        "ctx": nrm(ks[2], (BATCH, CTX_LEN, D_MODEL), 1.0),
        "c_ctx": nrm(ks[3], (D_MODEL,), 1.0),
        "w_ada": nrm(ks[4], (DEPTH, D_MODEL, 6 * D_MODEL), 0.5 * D_MODEL ** -0.5),
        "b_ada": nrm(ks[5], (DEPTH, 6 * D_MODEL), 0.01),
        "norm1": 1.0 + nrm(ks[6], (DEPTH, D_MODEL), 0.02),
        "w_in": nrm(ks[7], (DEPTH, D_MODEL, D_IN), D_MODEL ** -0.5),
        "conv_w": nrm(ks[8], (DEPTH, CONV_W, D_RNN), CONV_W ** -0.5),
        "conv_b": nrm(ks[9], (DEPTH, D_RNN), 0.01),
        "gate_a_w": nrm(ks[10], (DEPTH, 2, RNN_HEADS, RNN_BLOCK, RNN_BLOCK), RNN_BLOCK ** -0.5),
        "gate_a_b": nrm(ks[11], (DEPTH, 2, D_RNN), 0.01),
        "gate_x_w": nrm(ks[12], (DEPTH, 2, RNN_HEADS, RNN_BLOCK, RNN_BLOCK), RNN_BLOCK ** -0.5),
        "gate_x_b": nrm(ks[13], (DEPTH, 2, D_RNN), 0.01),
        "lru_lambda": lru_lambda,
        "w_fourier": nrm(ks[14], (DEPTH, D_FOURIER, D_MODEL), D_FOURIER ** -0.5),
        "w_rnn": nrm(ks[15], (DEPTH, D_RNN, D_MODEL), D_RNN ** -0.5),
        "w_out": nrm(ks[17], (DEPTH, D_MODEL, D_MODEL), D_MODEL ** -0.5),
        "norm2": 1.0 + nrm(ks[18], (DEPTH, D_MODEL), 0.02),
        "w_router": nrm(ks[19], (DEPTH, D_MODEL, N_EXPERTS), D_MODEL ** -0.5),
        "b_router": nrm(ks[20], (DEPTH, N_EXPERTS), 0.01),
        "w_gu": nrm(ks[21], (DEPTH, N_EXPERTS, D_MODEL, 2 * D_EXPERT), D_MODEL ** -0.5),
        "b_gu": nrm(ks[22], (DEPTH, N_EXPERTS, 2 * D_EXPERT), 0.01),
        "w_down": nrm(ks[23], (DEPTH, N_EXPERTS, D_EXPERT, D_MODEL), D_EXPERT ** -0.5),
        "b_down": nrm(ks[24], (DEPTH, N_EXPERTS, D_MODEL), 0.01),
        "norm_f": 1.0 + nrm(ks[25], (D_MODEL,), 0.02),
    }


def reference(x, c, ctx, c_ctx, w_ada, b_ada, norm1, w_in, conv_w, conv_b, gate_a_w, gate_a_b,
              gate_x_w, gate_x_b, lru_lambda, w_fourier, w_rnn, w_out, norm2, w_router, b_router,
              w_gu, b_gu, w_down, b_down, norm_f):
    bsz, seq, _ = x.shape
    rows = seq // GRID_W
    for l in range(DEPTH):
        last = l == DEPTH - 1
        sh1, sc1, g1, sh2, sc2, g2 = [m[:, None, :] for m in ada_modulation(c, w_ada[l], b_ada[l])]
        csh1, csc1, cg1, csh2, csc2, cg2 = ada_modulation(c_ctx, w_ada[l], b_ada[l])
        lru_p = (gate_a_w[l], gate_a_b[l], gate_x_w[l], gate_x_b[l], lru_lambda[l])

        hc = rms_norm(ctx, norm1[l]) * (1.0 + csc1) + csh1
        uc = centred_conv(hc @ w_in[l][:, COL_R0:COL_G0], conv_w[l], conv_b[l]).astype(jnp.float32)
        h_zero = jnp.zeros((bsz, D_RNN), jnp.float32)
        yc, h_ctx_final = bidir_rglru(uc, *lru_p, [h_zero, h_zero])

        hx = rms_norm(x, norm1[l]) * (1.0 + sc1) + sh1
        zx = hx @ w_in[l]
        ux = centred_conv(zx[..., COL_R0:COL_G0].reshape(bsz, rows, GRID_W, D_RNN),
                          conv_w[l], conv_b[l]).reshape(bsz, seq, D_RNN).astype(jnp.float32)
        yx, _ = bidir_rglru(ux, *lru_p, h_ctx_final)
        x = x + g1 * merge_branches(zx, yx, w_fourier[l], w_rnn[l], w_out[l])
        if not last:
            zc = hc @ w_in[l]
            ctx = ctx + cg1 * merge_branches(zc, yc, w_fourier[l], w_rnn[l], w_out[l])

        moe_p = (w_router[l], b_router[l], w_gu[l], b_gu[l], w_down[l], b_down[l])
        x = x + g2 * moe_ffn(rms_norm(x, norm2[l]) * (1.0 + sc2) + sh2, *moe_p)
        if not last:
            ctx = ctx + cg2 * moe_ffn(rms_norm(ctx, norm2[l]) * (1.0 + csc2) + csh2, *moe_p)
    return rms_norm(x, norm_f)
```

```python
import functools
import math

import numpy as np
import jax
import jax.numpy as jnp
from jax import lax
from jax.experimental import pallas as pl
from jax.experimental.pallas import tpu as pltpu

F32 = jnp.float32
BF16 = jnp.bfloat16

GRID_W = 64
CONV_W = 4
LRU_C = 8.0
TOP_K = 4
F_GROUP_DIM = 128
RNN_BLOCK = 128
SWIGLU_ALPHA = 1.702
SWIGLU_LIMIT = 7.0
EPS = 1e-6

V7X_LANES = 128
V7X_SUBLANES = 8
V7X_BF16_ROWS = 16
V7X_VMEM_LIMIT = 56 * 1024 * 1024

ROW_TILE = GRID_W * V7X_SUBLANES
FFT_PERM_ROWS = V7X_BF16_ROWS * GRID_W
FFT_COLS = 256
SCAN_ROWS = 2048
MOE_ROWS = 512
GATHER_ROWS = 512


def _dot(a, b):
    return jnp.dot(a, b, preferred_element_type=F32)


def _split_bf16(a):
    hi = a.astype(BF16)
    return hi, (a - hi.astype(F32)).astype(BF16)


def _dot3(a, b):
    ah, al = _split_bf16(a)
    bh, bl = _split_bf16(b)
    return _dot(ah, bh) + _dot(ah, bl) + _dot(al, bh)


def _sigmoid(x):
    return 0.5 * (jnp.tanh(0.5 * x) + 1.0)


def _gelu_tanh(x):
    c = math.sqrt(2.0 / math.pi)
    return 0.5 * x * (1.0 + jnp.tanh(c * (x + 0.044715 * (x * x * x))))


def _rms(x, g):
    return x * lax.rsqrt(jnp.mean(x * x, axis=-1, keepdims=True) + EPS) * g


def _modulate(xn, sc, sh):
    if sc.shape[0] == 1:
        return xn * (1.0 + sc) + sh
    r, d = xn.shape
    nb = sc.shape[0]
    x3 = xn.reshape(r // nb, nb, d)
    return (x3 * (1.0 + sc)[None] + sh[None]).reshape(r, d)


def _per_batch(v, g):
    r, d = v.shape
    nb = g.shape[0]
    return (v.reshape(r // nb, nb, d) * g[None]).reshape(r, d)


def _conv_time_major(z, cw, cb, nb):
    r, c = z.shape

    def shifted(s):
        k = abs(s) * nb
        zero = jnp.zeros((k, c), z.dtype)
        if s > 0:
            return jnp.concatenate([zero, z[:r - k]], axis=0)
        return jnp.concatenate([z[k:], zero], axis=0)

    return (cb + shifted(2) * cw[0:1] + shifted(1) * cw[1:2] + z * cw[2:3]
            + shifted(-1) * cw[3:4])


def _ada_kernel(c_ref, w_ref, b_ref, o_ref):
    cnd = c_ref[...]
    o_ref[...] = _dot3(cnd * _sigmoid(cnd), w_ref[...]) + b_ref[...]


def _ada_call(cond, w, b):
    r, d = cond.shape
    n = w.shape[1]
    return pl.pallas_call(
        _ada_kernel,
        grid=(n // d,),
        in_specs=[pl.BlockSpec((r, d), lambda j: (0, 0)),
                  pl.BlockSpec((d, d), lambda j: (0, j)),
                  pl.BlockSpec((1, d), lambda j: (0, j))],
        out_specs=pl.BlockSpec((r, d), lambda j: (0, j)),
        out_shape=jax.ShapeDtypeStruct((r, n), F32),
        name="ada_modulation",
    )(cond, w, b)


def _ctx_in_kernel(x_ref, sc_ref, sh_ref, g_ref, w_ref, cw_ref, cb_ref, u_ref, *, nb):
    h = _modulate(_rms(x_ref[...], g_ref[...]), sc_ref[...], sh_ref[...])
    z = _dot(h.astype(BF16), w_ref[...])
    u_ref[...] = _conv_time_major(z, cw_ref[...], cb_ref[...], nb).astype(u_ref.dtype)


def _ctx_in_call(ctx_t, sc, sh, g, w_r, cw, cb, nb):
    r, d = ctx_t.shape
    c = w_r.shape[1]
    full = lambda shape: pl.BlockSpec(shape, lambda i: (0,) * len(shape))
    return pl.pallas_call(
        functools.partial(_ctx_in_kernel, nb=nb),
        grid=(1,),
        in_specs=[full((r, d)), full((1, d)), full((1, d)), full((1, d)), full((d, c)),
                  full((CONV_W, c)), full((1, c))],
        out_specs=full((r, c)),
        out_shape=jax.ShapeDtypeStruct((r, c), BF16),
        compiler_params=pltpu.CompilerParams(vmem_limit_bytes=V7X_VMEM_LIMIT),
        name="ctx_in_proj",
    )(ctx_t, sc, sh, g, w_r, cw, cb)


def _in_proj_kernel(x_ref, sc_ref, sh_ref, g_ref, w_ref, cw_ref, cb_ref, pd_ref, cs_ref,
                    a_ref, b_ref, u_ref, gg_ref, gf_ref, gr_ref, *, nb, d_f, d_rnn):
    d = x_ref.shape[1]
    h = _modulate(_rms(x_ref[...], g_ref[...]), sc_ref[...], sh_ref[...]).astype(BF16)
    c_r, c_g, c_m = d_f, d_f + d_rnn, d_f + 2 * d_rnn

    zf = _dot(h, w_ref[:, 0:c_r]).astype(BF16)
    zf = _dot(pd_ref[...], zf).astype(BF16)
    for grp in range(d_f // F_GROUP_DIM):
        ab = _dot(zf[:, grp * F_GROUP_DIM:(grp + 1) * F_GROUP_DIM], cs_ref[...])
        for bi in range(nb):
            rows = slice(bi * GRID_W, (bi + 1) * GRID_W)
            col = bi * d_f + grp * F_GROUP_DIM
            a_ref[:, col:col + F_GROUP_DIM] = ab[rows, :F_GROUP_DIM].astype(BF16)
            b_ref[:, col:col + F_GROUP_DIM] = ab[rows, F_GROUP_DIM:].astype(BF16)

    zr = _dot(h, w_ref[:, c_r:c_g])
    u_ref[...] = _conv_time_major(zr, cw_ref[...], cb_ref[...], nb).astype(BF16)
    gg_ref[...] = _gelu_tanh(_dot(h, w_ref[:, c_g:c_m])).astype(BF16)
    gf_ref[...] = _sigmoid(_dot(h, w_ref[:, c_m:c_m + d])).astype(BF16)
    gr_ref[...] = _sigmoid(_dot(h, w_ref[:, c_m + d:c_m + 2 * d])).astype(BF16)


def _in_proj_call(xt, sc, sh, g, w_in, cw, cb, pd, cs, nb, d_f, d_rnn):
    n, d = xt.shape
    d_in = w_in.shape[1]
    nt = n // ROW_TILE
    seq = n // nb
    const = lambda shape: pl.BlockSpec(shape, lambda i: (0,) * len(shape))
    row = lambda c: pl.BlockSpec((ROW_TILE, c), lambda i: (i, 0))
    wide = pl.BlockSpec((GRID_W, nb * d_f), lambda i: (i, 0))
    return pl.pallas_call(
        functools.partial(_in_proj_kernel, nb=nb, d_f=d_f, d_rnn=d_rnn),
        grid=(nt,),
        in_specs=[row(d), const((nb, d)), const((nb, d)), const((1, d)), const((d, d_in)),
                  const((CONV_W, d_rnn)), const((1, d_rnn)), const((ROW_TILE, ROW_TILE)),
                  const((F_GROUP_DIM, 2 * F_GROUP_DIM))],
        out_specs=[wide, wide, row(d_rnn), row(d_rnn), row(d), row(d)],
        out_shape=[jax.ShapeDtypeStruct((seq, nb * d_f), BF16),
                   jax.ShapeDtypeStruct((seq, nb * d_f), BF16),
                   jax.ShapeDtypeStruct((n, d_rnn), BF16),
                   jax.ShapeDtypeStruct((n, d_rnn), BF16),
                   jax.ShapeDtypeStruct((n, d), BF16),
                   jax.ShapeDtypeStruct((n, d), BF16)],
        compiler_params=pltpu.CompilerParams(dimension_semantics=("parallel",),
                                             vmem_limit_bytes=V7X_VMEM_LIMIT),
        name="latent_in_proj",
    )(xt, sc, sh, g, w_in, cw, cb, pd, cs)


def _fft_kernel(a_ref, b_ref, p_ref, d1_ref, m3c_ref, m3s_ref, o_ref, sa, sb, *, n1, scale):
    tn = a_ref.shape[1]
    blk = V7X_BF16_ROWS
    for j in range(n1 // blk):
        rows = slice(j * FFT_PERM_ROWS, (j + 1) * FFT_PERM_ROWS)
        sa[:, j * blk:(j + 1) * blk, :] = (
            _dot(p_ref[...], a_ref[rows, :]).astype(BF16).reshape(GRID_W, blk, tn))
        sb[:, j * blk:(j + 1) * blk, :] = (
            _dot(p_ref[...], b_ref[rows, :]).astype(BF16).reshape(GRID_W, blk, tn))

    def stage1(l2, carry):
        dm = d1_ref[l2]
        ca = _dot(dm, sa[l2])
        cb = _dot(dm, sb[l2])
        sa[l2] = (ca[:n1] - cb[n1:]).astype(BF16)
        sb[l2] = (-(cb[:n1] + ca[n1:])).astype(BF16)
        return carry

    lax.fori_loop(0, GRID_W, stage1, 0)

    for i in range(n1 // blk):
        xr = sa[:, i * blk:(i + 1) * blk, :].reshape(FFT_PERM_ROWS, tn)
        xi = sb[:, i * blk:(i + 1) * blk, :].reshape(FFT_PERM_ROWS, tn)
        y = (_dot(m3c_ref[...], xr) + _dot(m3s_ref[...], xi)) * scale
        o_ref[:, i * blk:(i + 1) * blk, :] = y.astype(BF16).reshape(GRID_W, blk, tn)


def _fft_constants(seq):
    n1 = seq // GRID_W
    blk = V7X_BF16_ROWS
    perm = np.zeros((FFT_PERM_ROWS, FFT_PERM_ROWS), np.float32)
    l1s, l2 = np.meshgrid(np.arange(blk), np.arange(GRID_W), indexing="ij")
    perm[(l2 * blk + l1s).ravel(), (l1s * GRID_W + l2).ravel()] = 1.0
    k1 = np.arange(n1)[None, :, None]
    l1 = np.arange(n1)[None, None, :]
    l2v = np.arange(GRID_W)[:, None, None]
    ang = 2.0 * np.pi * ((k1 * (GRID_W * l1 + l2v)) % seq) / seq
    d1 = np.concatenate([np.cos(ang), np.sin(ang)], axis=1)
    k2 = np.arange(GRID_W)[:, None]
    l2m = np.arange(GRID_W)[None, :]
    ang3 = 2.0 * np.pi * ((k2 * l2m) % GRID_W) / GRID_W
    eye = np.eye(blk)
    m3c = np.kron(np.cos(ang3), eye)
    m3s = np.kron(np.sin(ang3), eye)
    as_bf16 = lambda v: jnp.asarray(v, F32).astype(BF16)
    return as_bf16(perm), as_bf16(d1), as_bf16(m3c), as_bf16(m3s)


def _fft_call(a_w, b_w):
    seq, nw = a_w.shape
    n1 = seq // GRID_W
    tn = min(FFT_COLS, nw)
    perm, d1, m3c, m3s = _fft_constants(seq)
    scale = 1.0 / math.sqrt(seq * F_GROUP_DIM)
    const = lambda shape: pl.BlockSpec(shape, lambda j: (0,) * len(shape))
    strip = pl.BlockSpec((seq, tn), lambda j: (0, j))
    out = pl.pallas_call(
        functools.partial(_fft_kernel, n1=n1, scale=scale),
        grid=(nw // tn,),
        in_specs=[strip, strip, const(perm.shape), const(d1.shape), const(m3c.shape),
                  const(m3s.shape)],
        out_specs=pl.BlockSpec((GRID_W, n1, tn), lambda j: (0, 0, j)),
        out_shape=jax.ShapeDtypeStruct((GRID_W, n1, nw), BF16),
        scratch_shapes=[pltpu.VMEM((GRID_W, n1, tn), BF16), pltpu.VMEM((GRID_W, n1, tn), BF16)],
        compiler_params=pltpu.CompilerParams(dimension_semantics=("parallel",),
                                             vmem_limit_bytes=V7X_VMEM_LIMIT),
        name="position_dft",
    )(a_w, b_w, perm, d1, m3c, m3s)
    return out.reshape(seq, nw)


def _scan_kernel(u_ref, wg_ref, ba_ref, bx_ref, sp_ref, h0_ref, y_ref, hf_ref, a_s, b_s, h_s,
                 *, reverse, steps, heads, nb):
    @pl.when(pl.program_id(0) == 0)
    def _():
        h_s[...] = h0_ref[...]

    u = u_ref[...]
    for hd in range(heads):
        sl = slice(hd * RNN_BLOCK, (hd + 1) * RNN_BLOCK)
        ub = u[:, sl]
        g = _dot(ub, wg_ref[hd])
        r = _sigmoid(g[:, :RNN_BLOCK] + ba_ref[:, sl])
        ig = _sigmoid(g[:, RNN_BLOCK:] + bx_ref[:, sl])
        log_a = (-LRU_C) * r * sp_ref[:, sl]
        a = jnp.exp(log_a)
        mult = jnp.sqrt(-jnp.tanh(log_a) * (a * a + 1.0))
        a_s[:, sl] = a
        b_s[:, sl] = mult * ig * ub.astype(F32)

    def step(j, h):
        t = (steps - 1 - j) if reverse else j
        r0 = pl.multiple_of(t * nb, nb)
        h = a_s[pl.ds(r0, nb), :] * h + b_s[pl.ds(r0, nb), :]
        b_s[pl.ds(r0, nb), :] = h
        return h

    h = lax.fori_loop(0, steps, step, h_s[...], unroll=8)
    h_s[...] = h
    hf_ref[...] = h
    y_ref[...] = b_s[...].astype(y_ref.dtype)


def _scan_call(u, wg, ba, bx, sp, h0, reverse, name):
    n, c = u.shape
    nb = h0.shape[0]
    heads = wg.shape[0]
    tr = min(SCAN_ROWS, n)
    nt = n // tr
    const = lambda shape: pl.BlockSpec(shape, lambda i: (0,) * len(shape))
    tile = pl.BlockSpec((tr, c), (lambda i: (nt - 1 - i, 0)) if reverse else (lambda i: (i, 0)))
    return pl.pallas_call(
        functools.partial(_scan_kernel, reverse=reverse, steps=tr // nb, heads=heads, nb=nb),
        grid=(nt,),
        in_specs=[tile, const(wg.shape), const((1, c)), const((1, c)), const((1, c)),
                  const((nb, c))],
        out_specs=[tile, const((nb, c))],
        out_shape=[jax.ShapeDtypeStruct((n, c), BF16), jax.ShapeDtypeStruct((nb, c), F32)],
        scratch_shapes=[pltpu.VMEM((tr, c), F32), pltpu.VMEM((tr, c), F32),
                        pltpu.VMEM((nb, c), F32)],
        compiler_params=pltpu.CompilerParams(dimension_semantics=("arbitrary",),
                                             vmem_limit_bytes=V7X_VMEM_LIMIT),
        name=name,
    )(u, wg, ba, bx, sp, h0)


def _merge_kernel(x_ref, yf_ref, yb_ref, gg_ref, gf_ref, gr_ref, yw_ref, pi_ref, wf_ref,
                  wr_ref, wo_ref, g1_ref, n2_ref, sc2_ref, sh2_ref, wrt_ref, brt_ref, tri_ref,
                  x1_ref, hm_ref, ti_ref, tw_ref, rk_ref, cnt_ref, carry, *, nb, d_f):
    @pl.when(pl.program_id(0) == 0)
    def _():
        carry[...] = jnp.zeros_like(carry)

    yw = jnp.concatenate([yw_ref[:, bi * d_f:(bi + 1) * d_f] for bi in range(nb)], axis=0)
    yfm = _dot(pi_ref[...], yw).astype(BF16)
    f_out = _dot(yfm, wf_ref[...])
    y = yf_ref[...].astype(F32) + yb_ref[...].astype(F32)
    r_out = _dot((y * gg_ref[...].astype(F32)).astype(BF16), wr_ref[...])
    mix = gf_ref[...].astype(F32) * f_out + gr_ref[...].astype(F32) * r_out
    x1 = x_ref[...] + _per_batch(_dot(mix.astype(BF16), wo_ref[...]), g1_ref[...])
    x1_ref[...] = x1

    h2 = _modulate(_rms(x1, n2_ref[...]), sc2_ref[...], sh2_ref[...])
    hm_ref[...] = h2

    lg = _dot3(h2, wrt_ref[...]) + brt_ref[...]
    tm, ne = lg.shape
    lane = lax.broadcasted_iota(jnp.int32, (tm, ne), 1)
    lane_k = lax.broadcasted_iota(jnp.int32, (tm, TOP_K), 1)
    vals, hots = [], []
    top_i = jnp.zeros((tm, TOP_K), jnp.int32)
    for k in range(TOP_K):
        m = jnp.max(lg, axis=-1, keepdims=True)
        idx = jnp.min(jnp.where(lg == m, lane, ne), axis=-1, keepdims=True)
        hit = lane == idx
        vals.append(m)
        hots.append(hit.astype(F32))
        top_i = jnp.where(lane_k == k, idx, top_i)
        lg = jnp.where(hit, -jnp.inf, lg)
    exps = [jnp.exp(v - vals[0]) for v in vals]
    inv = 1.0 / (exps[0] + exps[1] + exps[2] + exps[3])
    top_w = jnp.zeros((tm, TOP_K), F32)
    for k in range(TOP_K):
        top_w = jnp.where(lane_k == k, exps[k] * inv, top_w)
    ti_ref[...] = top_i
    tw_ref[...] = top_w

    per_tok = hots[0] + hots[1] + hots[2] + hots[3]
    before = _dot(tri_ref[...], per_tok.astype(BF16)) + carry[...]
    rank = jnp.zeros((tm, TOP_K), F32)
    for k in range(TOP_K):
        rank = jnp.where(lane_k == k, jnp.sum(hots[k] * before, axis=-1, keepdims=True), rank)
    rk_ref[...] = rank.astype(jnp.int32)
    carry[...] = carry[...] + jnp.sum(per_tok, axis=0, keepdims=True)
    cnt_ref[...] = carry[...]


def _merge_call(xt, y_f, y_b, gg, gf, gr, yw, pint, w_f, w_r, w_o, g1, n2, sc2, sh2, w_rt, b_rt,
                tri, nb, d_f):
    n, d = xt.shape
    d_rnn = y_f.shape[1]
    ne = w_rt.shape[1]
    nt = n // ROW_TILE
    const = lambda shape: pl.BlockSpec(shape, lambda i: (0,) * len(shape))
    row = lambda c: pl.BlockSpec((ROW_TILE, c), lambda i: (i, 0))
    return pl.pallas_call(
        functools.partial(_merge_kernel, nb=nb, d_f=d_f),
        grid=(nt,),
        in_specs=[row(d), row(d_rnn), row(d_rnn), row(d_rnn), row(d), row(d),
                  pl.BlockSpec((GRID_W, nb * d_f), lambda i: (i, 0)),
                  const((ROW_TILE, ROW_TILE)), const((d_f, d)), const((d_rnn, d)), const((d, d)),
                  const((nb, d)), const((1, d)), const((nb, d)), const((nb, d)),
                  const((d, ne)), const((1, ne)), const((ROW_TILE, ROW_TILE))],
        out_specs=[row(d), row(d), row(TOP_K), row(TOP_K), row(TOP_K), const((1, ne))],
        out_shape=[jax.ShapeDtypeStruct((n, d), F32), jax.ShapeDtypeStruct((n, d), F32),
                   jax.ShapeDtypeStruct((n, TOP_K), jnp.int32),
                   jax.ShapeDtypeStruct((n, TOP_K), F32),
                   jax.ShapeDtypeStruct((n, TOP_K), jnp.int32),
                   jax.ShapeDtypeStruct((1, ne), F32)],
        scratch_shapes=[pltpu.VMEM((1, ne), F32)],
        compiler_params=pltpu.CompilerParams(dimension_semantics=("arbitrary",),
                                             vmem_limit_bytes=V7X_VMEM_LIMIT),
        name="merge_and_route",
    )(xt, y_f, y_b, gg, gf, gr, yw, pint, w_f, w_r, w_o, g1, n2, sc2, sh2, w_rt, b_rt, tri)


def _gather_kernel(idx_ref, src_ref, o_ref, sem, *, rows):
    def row_copy(r):
        return pltpu.make_async_copy(src_ref.at[pl.ds(idx_ref[0, 0, r], 1)],
                                     o_ref.at[pl.ds(r, 1)], sem)

    def issue(r, carry):
        row_copy(r).start()
        return carry

    lax.fori_loop(0, rows, issue, 0, unroll=8)
    pltpu.make_async_copy(src_ref.at[pl.ds(0, rows)], o_ref, sem).wait()


def _gather_call(src, idx, name):
    n_out = idx.shape[0]
    d = src.shape[1]
    g = min(GATHER_ROWS, n_out)
    idx3 = idx.reshape(n_out // g, 1, g)
    return pl.pallas_call(
        functools.partial(_gather_kernel, rows=g),
        grid=(n_out // g,),
        in_specs=[pl.BlockSpec((1, 1, g), lambda i: (i, 0, 0), memory_space=pltpu.SMEM),
                  pl.BlockSpec(memory_space=pl.ANY)],
        out_specs=pl.BlockSpec((g, d), lambda i: (i, 0)),
        out_shape=jax.ShapeDtypeStruct((n_out, d), src.dtype),
        scratch_shapes=[pltpu.SemaphoreType.DMA],
        compiler_params=pltpu.CompilerParams(dimension_semantics=("arbitrary",)),
        name=name,
    )(idx3, src)


def _expert_kernel(be_ref, nu_ref, x_ref, wgu_ref, bgu_ref, wd_ref, bd_ref, y_ref, *, d_e):
    i = pl.program_id(0)

    @pl.when(i < nu_ref[0])
    def _():
        gu = _dot(x_ref[...].astype(BF16), wgu_ref[...]) + bgu_ref[...]
        gate = jnp.minimum(gu[:, :d_e], SWIGLU_LIMIT)
        up = jnp.clip(gu[:, d_e:], -SWIGLU_LIMIT, SWIGLU_LIMIT)
        act = (up + 1.0) * gate * _sigmoid(SWIGLU_ALPHA * gate)
        y_ref[...] = _dot(act.astype(BF16), wd_ref[...]) + bd_ref[...]

    @pl.when(i >= nu_ref[0])
    def _():
        y_ref[...] = jnp.zeros_like(y_ref)


def _expert_call(xs, block_expert, n_used, w_gu, b_gu, w_down, b_down):
    p, d = xs.shape
    ne, _, d_e2 = w_gu.shape
    d_e = d_e2 // 2
    nblk = p // MOE_ROWS
    grid_spec = pltpu.PrefetchScalarGridSpec(
        num_scalar_prefetch=2,
        grid=(nblk,),
        in_specs=[pl.BlockSpec((MOE_ROWS, d), lambda i, be, nu: (i, 0)),
                  pl.BlockSpec((None, d, d_e2), lambda i, be, nu: (be[i], 0, 0)),
                  pl.BlockSpec((None, 1, d_e2), lambda i, be, nu: (be[i], 0, 0)),
                  pl.BlockSpec((None, d_e, d), lambda i, be, nu: (be[i], 0, 0)),
                  pl.BlockSpec((None, 1, d), lambda i, be, nu: (be[i], 0, 0))],
        out_specs=pl.BlockSpec((MOE_ROWS, d), lambda i, be, nu: (i, 0)),
    )
    return pl.pallas_call(
        functools.partial(_expert_kernel, d_e=d_e),
        grid_spec=grid_spec,
        out_shape=jax.ShapeDtypeStruct((p, d), F32),
        compiler_params=pltpu.CompilerParams(dimension_semantics=("arbitrary",),
                                             vmem_limit_bytes=V7X_VMEM_LIMIT),
        name="expert_ffn",
    )(block_expert, n_used, xs, w_gu, b_gu.reshape(ne, 1, d_e2), w_down, b_down.reshape(ne, 1, d))


def _combine_kernel(x1_ref, yg_ref, tw_ref, g2_ref, nf_ref, o_ref):
    w = tw_ref[...]
    moe = w[:, 0:1] * yg_ref[0]
    for k in range(1, TOP_K):
        moe = moe + w[:, k:k + 1] * yg_ref[k]
    o_ref[...] = _rms(x1_ref[...] + _per_batch(moe, g2_ref[...]), nf_ref[...])


def _combine_call(x1, yg, top_w, g2, nf):
    n, d = x1.shape
    nb = g2.shape[0]
    const = lambda shape: pl.BlockSpec(shape, lambda i: (0,) * len(shape))
    row = lambda c: pl.BlockSpec((ROW_TILE, c), lambda i: (i, 0))
    return pl.pallas_call(
        _combine_kernel,
        grid=(n // ROW_TILE,),
        in_specs=[row(d), pl.BlockSpec((TOP_K, ROW_TILE, d), lambda i: (0, i, 0)), row(TOP_K),
                  const((nb, d)), const((1, d))],
        out_specs=row(d),
        out_shape=jax.ShapeDtypeStruct((n, d), F32),
        compiler_params=pltpu.CompilerParams(dimension_semantics=("parallel",),
                                             vmem_limit_bytes=V7X_VMEM_LIMIT),
        name="combine_final_norm",
    )(x1, yg, top_w, g2, nf)


def _tile_constants(nb):
    t = np.arange(GRID_W)[:, None]
    b = np.arange(nb)[None, :]
    deint = np.zeros((ROW_TILE, ROW_TILE), np.float32)
    deint[(b * GRID_W + t).ravel(), (t * nb + b).ravel()] = 1.0
    c = np.arange(F_GROUP_DIM)
    ang = 2.0 * np.pi * ((c[:, None] * c[None, :]) % F_GROUP_DIM) / F_GROUP_DIM
    cs = np.concatenate([np.cos(ang), np.sin(ang)], axis=1)
    tri = np.tril(np.ones((ROW_TILE, ROW_TILE), np.float32), -1)
    as_bf16 = lambda v: jnp.asarray(v, F32).astype(BF16)
    return as_bf16(deint), as_bf16(deint.T), as_bf16(cs), as_bf16(tri)


def kernel(x, c, ctx, c_ctx, w_ada, b_ada, norm1, w_in, conv_w, conv_b, gate_a_w, gate_a_b,
           gate_x_w, gate_x_b, lru_lambda, w_fourier, w_rnn, w_out, norm2, w_router, b_router,
           w_gu, b_gu, w_down, b_down, norm_f):
    nb, seq, d = x.shape
    depth = w_ada.shape[0]
    assert depth == 1 and nb == V7X_SUBLANES and seq % FFT_PERM_ROWS == 0
    n = nb * seq
    d_rnn = conv_w.shape[-1]
    d_f = w_fourier.shape[1]
    ne = w_router.shape[-1]
    lyr = 0
    deint, inter, cs, tri = _tile_constants(nb)

    xt = jnp.transpose(x, (1, 0, 2)).reshape(n, d)
    ctx_t = jnp.transpose(ctx, (1, 0, 2)).reshape(-1, d)

    cond = jnp.zeros((2 * V7X_SUBLANES, d), F32).at[:nb].set(c).at[nb].set(c_ctx)
    mods = _ada_call(cond, w_ada[lyr], b_ada[lyr].reshape(1, -1))
    sh1, sc1, g1, sh2, sc2, g2 = [mods[:nb, k * d:(k + 1) * d] for k in range(6)]
    csh1, csc1 = mods[nb:nb + 1, 0:d], mods[nb:nb + 1, d:2 * d]

    w_in_b = w_in[lyr].astype(BF16)
    cw, cb = conv_w[lyr], conv_b[lyr].reshape(1, d_rnn)
    n1r = norm1[lyr].reshape(1, d)
    wg = jnp.concatenate([gate_a_w[lyr], gate_x_w[lyr]], axis=-1).astype(BF16)
    sp = jax.nn.softplus(-lru_lambda[lyr].astype(F32))
    scan = lambda u, dr, h0, name: _scan_call(
        u, wg[dr], gate_a_b[lyr, dr].reshape(1, -1), gate_x_b[lyr, dr].reshape(1, -1),
        sp[dr].reshape(1, -1), h0, bool(dr), name)

    u_ctx = _ctx_in_call(ctx_t, csc1, csh1, n1r, w_in_b[:, d_f:d_f + d_rnn], cw, cb, nb)
    h_zero = jnp.zeros((nb, d_rnn), F32)
    _, h_ctx_f = scan(u_ctx, 0, h_zero, "ctx_scan_fwd")
    _, h_ctx_b = scan(u_ctx, 1, h_zero, "ctx_scan_bwd")

    a_w, b_w, u, gg, gf, gr = _in_proj_call(xt, sc1, sh1, n1r, w_in_b, cw, cb, deint, cs, nb,
                                            d_f, d_rnn)
    y_wide = _fft_call(a_w, b_w)
    y_f, _ = scan(u, 0, h_ctx_f, "scan_fwd")
    y_b, _ = scan(u, 1, h_ctx_b, "scan_bwd")
    x1, hmod, top_i, top_w, rank, counts = _merge_call(
        xt, y_f, y_b, gg, gf, gr, y_wide, inter, w_fourier[lyr].astype(BF16),
        w_rnn[lyr].astype(BF16), w_out[lyr].astype(BF16), g1, norm2[lyr].reshape(1, d), sc2, sh2,
        w_router[lyr], b_router[lyr].reshape(1, ne), tri, nb, d_f)

    cnt = counts.reshape(ne).astype(jnp.int32)
    padded = (cnt + MOE_ROWS - 1) // MOE_ROWS * MOE_ROWS
    ends = jnp.cumsum(padded)
    dest = (ends - padded)[top_i] + rank
    nblk = (n * TOP_K + ne * (MOE_ROWS - 1) + MOE_ROWS - 1) // MOE_ROWS
    tok = jnp.broadcast_to(jnp.arange(n, dtype=jnp.int32)[:, None], (n, TOP_K))
    slot_tok = jnp.zeros((nblk * MOE_ROWS,), jnp.int32).at[dest.reshape(-1)].set(
        tok.reshape(-1), unique_indices=True)
    block_start = jnp.arange(nblk, dtype=jnp.int32) * MOE_ROWS
    block_expert = jnp.minimum(jnp.searchsorted(ends, block_start, side="right"),
                               ne - 1).astype(jnp.int32)
    n_used = (ends[-1:] // MOE_ROWS).astype(jnp.int32)

    xs = _gather_call(hmod, slot_tok, "dispatch_gather")
    ys = _expert_call(xs, block_expert, n_used, w_gu[lyr].astype(BF16), b_gu[lyr],
                      w_down[lyr].astype(BF16), b_down[lyr])
    yg = _gather_call(ys, dest.T.reshape(-1), "combine_gather").reshape(TOP_K, n, d)
    out_t = _combine_call(x1, yg, top_w, g2, norm_f.reshape(1, d))
    return jnp.transpose(out_t.reshape(seq, nb, d), (1, 0, 2))
```

```python
import functools
import math

import numpy as np
import jax
import jax.numpy as jnp
from jax import lax
from jax.experimental import pallas as pl
from jax.experimental.pallas import tpu as pltpu
from jax.experimental.pallas import tpu_sc as plsc

F32 = jnp.float32
BF16 = jnp.bfloat16

GRID_W = 64
CONV_W = 4
LRU_C = 8.0
TOP_K = 4
F_GROUP_DIM = 128
RNN_BLOCK = 128
SWIGLU_ALPHA = 1.702
SWIGLU_LIMIT = 7.0
EPS = 1e-6

V7X_LANES = 128
V7X_SUBLANES = 8
V7X_BF16_ROWS = 16
V7X_VMEM_LIMIT = 56 * 1024 * 1024

ROW_TILE = GRID_W * V7X_SUBLANES
FFT_PERM_ROWS = V7X_BF16_ROWS * GRID_W
FFT_COLS = 256
SCAN_ROWS = 2048
MOE_ROWS = 512
SC_WINDOW = 128
HIGH_HALF = -65536
LOW_HALF = 65535


def _dot(a, b):
    return jnp.dot(a, b, preferred_element_type=F32)


def _split_bf16(a):
    hi = a.astype(BF16)
    return hi, (a - hi.astype(F32)).astype(BF16)


def _dot3(a, b):
    ah, al = _split_bf16(a)
    bh, bl = _split_bf16(b)
    return _dot(ah, bh) + _dot(ah, bl) + _dot(al, bh)


def _sigmoid(x):
    return 0.5 * (jnp.tanh(0.5 * x) + 1.0)


def _gelu_tanh(x):
    c = math.sqrt(2.0 / math.pi)
    return 0.5 * x * (1.0 + jnp.tanh(c * (x + 0.044715 * (x * x * x))))


def _pack_bf16_pairs(v):
    c = v.shape[1] // 2
    bits = pltpu.bitcast(v.astype(BF16).astype(F32), jnp.int32)
    return (bits[:, :c] & HIGH_HALF) | ((bits[:, c:] >> 16) & LOW_HALF)


def _unpack_bf16_pairs(p):
    hi = pltpu.bitcast(p & HIGH_HALF, F32)
    lo = pltpu.bitcast(p << 16, F32)
    return jnp.concatenate([hi, lo], axis=1)


def _rms(x, g):
    return x * lax.rsqrt(jnp.mean(x * x, axis=-1, keepdims=True) + EPS) * g


def _modulate(xn, sc, sh):
    if sc.shape[0] == 1:
        return xn * (1.0 + sc) + sh
    r, d = xn.shape
    nb = sc.shape[0]
    x3 = xn.reshape(r // nb, nb, d)
    return (x3 * (1.0 + sc)[None] + sh[None]).reshape(r, d)


def _per_batch(v, g):
    r, d = v.shape
    nb = g.shape[0]
    return (v.reshape(r // nb, nb, d) * g[None]).reshape(r, d)


def _conv_time_major(z, cw, cb, nb):
    r, c = z.shape

    def shifted(s):
        k = abs(s) * nb
        zero = jnp.zeros((k, c), z.dtype)
        if s > 0:
            return jnp.concatenate([zero, z[:r - k]], axis=0)
        return jnp.concatenate([z[k:], zero], axis=0)

    return (cb + shifted(2) * cw[0:1] + shifted(1) * cw[1:2] + z * cw[2:3]
            + shifted(-1) * cw[3:4])


def _ada_kernel(c_ref, w_ref, b_ref, o_ref):
    cnd = c_ref[...]
    o_ref[...] = _dot3(cnd * _sigmoid(cnd), w_ref[...]) + b_ref[...]


def _ada_call(cond, w, b):
    r, d = cond.shape
    n = w.shape[1]
    return pl.pallas_call(
        _ada_kernel,
        grid=(n // d,),
        in_specs=[pl.BlockSpec((r, d), lambda j: (0, 0)),
                  pl.BlockSpec((d, d), lambda j: (0, j)),
                  pl.BlockSpec((1, d), lambda j: (0, j))],
        out_specs=pl.BlockSpec((r, d), lambda j: (0, j)),
        out_shape=jax.ShapeDtypeStruct((r, n), F32),
        name="ada_modulation",
    )(cond, w, b)


def _ctx_in_kernel(x_ref, sc_ref, sh_ref, g_ref, w_ref, cw_ref, cb_ref, u_ref, *, nb):
    h = _modulate(_rms(x_ref[...], g_ref[...]), sc_ref[...], sh_ref[...])
    z = _dot(h.astype(BF16), w_ref[...])
    u_ref[...] = _conv_time_major(z, cw_ref[...], cb_ref[...], nb).astype(u_ref.dtype)


def _ctx_in_call(ctx_t, sc, sh, g, w_r, cw, cb, nb):
    r, d = ctx_t.shape
    c = w_r.shape[1]
    full = lambda shape: pl.BlockSpec(shape, lambda i: (0,) * len(shape))
    return pl.pallas_call(
        functools.partial(_ctx_in_kernel, nb=nb),
        grid=(1,),
        in_specs=[full((r, d)), full((1, d)), full((1, d)), full((1, d)), full((d, c)),
                  full((CONV_W, c)), full((1, c))],
        out_specs=full((r, c)),
        out_shape=jax.ShapeDtypeStruct((r, c), BF16),
        compiler_params=pltpu.CompilerParams(vmem_limit_bytes=V7X_VMEM_LIMIT),
        name="ctx_in_proj",
    )(ctx_t, sc, sh, g, w_r, cw, cb)


def _in_proj_kernel(x_ref, sc_ref, sh_ref, g_ref, w_ref, cw_ref, cb_ref, pd_ref, cs_ref,
                    a_ref, b_ref, u_ref, gg_ref, gf_ref, gr_ref, *, nb, d_f, d_rnn):
    d = x_ref.shape[1]
    h = _modulate(_rms(x_ref[...], g_ref[...]), sc_ref[...], sh_ref[...]).astype(BF16)
    c_r, c_g, c_m = d_f, d_f + d_rnn, d_f + 2 * d_rnn

    zf = _dot(h, w_ref[:, 0:c_r]).astype(BF16)
    zf = _dot(pd_ref[...], zf).astype(BF16)
    for grp in range(d_f // F_GROUP_DIM):
        ab = _dot(zf[:, grp * F_GROUP_DIM:(grp + 1) * F_GROUP_DIM], cs_ref[...])
        for bi in range(nb):
            rows = slice(bi * GRID_W, (bi + 1) * GRID_W)
            col = bi * d_f + grp * F_GROUP_DIM
            a_ref[:, col:col + F_GROUP_DIM] = ab[rows, :F_GROUP_DIM].astype(BF16)
            b_ref[:, col:col + F_GROUP_DIM] = ab[rows, F_GROUP_DIM:].astype(BF16)

    zr = _dot(h, w_ref[:, c_r:c_g])
    u_ref[...] = _conv_time_major(zr, cw_ref[...], cb_ref[...], nb).astype(BF16)
    gg_ref[...] = _gelu_tanh(_dot(h, w_ref[:, c_g:c_m])).astype(BF16)
    gf_ref[...] = _sigmoid(_dot(h, w_ref[:, c_m:c_m + d])).astype(BF16)
    gr_ref[...] = _sigmoid(_dot(h, w_ref[:, c_m + d:c_m + 2 * d])).astype(BF16)


def _in_proj_call(xt, sc, sh, g, w_in, cw, cb, pd, cs, nb, d_f, d_rnn):
    n, d = xt.shape
    d_in = w_in.shape[1]
    nt = n // ROW_TILE
    seq = n // nb
    const = lambda shape: pl.BlockSpec(shape, lambda i: (0,) * len(shape))
    row = lambda c: pl.BlockSpec((ROW_TILE, c), lambda i: (i, 0))
    wide = pl.BlockSpec((GRID_W, nb * d_f), lambda i: (i, 0))
    return pl.pallas_call(
        functools.partial(_in_proj_kernel, nb=nb, d_f=d_f, d_rnn=d_rnn),
        grid=(nt,),
        in_specs=[row(d), const((nb, d)), const((nb, d)), const((1, d)), const((d, d_in)),
                  const((CONV_W, d_rnn)), const((1, d_rnn)), const((ROW_TILE, ROW_TILE)),
                  const((F_GROUP_DIM, 2 * F_GROUP_DIM))],
        out_specs=[wide, wide, row(d_rnn), row(d_rnn), row(d), row(d)],
        out_shape=[jax.ShapeDtypeStruct((seq, nb * d_f), BF16),
                   jax.ShapeDtypeStruct((seq, nb * d_f), BF16),
                   jax.ShapeDtypeStruct((n, d_rnn), BF16),
                   jax.ShapeDtypeStruct((n, d_rnn), BF16),
                   jax.ShapeDtypeStruct((n, d), BF16),
                   jax.ShapeDtypeStruct((n, d), BF16)],
        compiler_params=pltpu.CompilerParams(dimension_semantics=("parallel",),
                                             vmem_limit_bytes=V7X_VMEM_LIMIT),
        name="latent_in_proj",
    )(xt, sc, sh, g, w_in, cw, cb, pd, cs)


def _fft_kernel(a_ref, b_ref, p_ref, d1_ref, m3c_ref, m3s_ref, o_ref, sa, sb, *, n1, scale):
    tn = a_ref.shape[1]
    blk = V7X_BF16_ROWS
    for j in range(n1 // blk):
        rows = slice(j * FFT_PERM_ROWS, (j + 1) * FFT_PERM_ROWS)
        sa[:, j * blk:(j + 1) * blk, :] = (
            _dot(p_ref[...], a_ref[rows, :]).astype(BF16).reshape(GRID_W, blk, tn))
        sb[:, j * blk:(j + 1) * blk, :] = (
            _dot(p_ref[...], b_ref[rows, :]).astype(BF16).reshape(GRID_W, blk, tn))

    def stage1(l2, carry):
        dm = d1_ref[l2]
        ca = _dot(dm, sa[l2])
        cb = _dot(dm, sb[l2])
        sa[l2] = (ca[:n1] - cb[n1:]).astype(BF16)
        sb[l2] = (-(cb[:n1] + ca[n1:])).astype(BF16)
        return carry

    lax.fori_loop(0, GRID_W, stage1, 0)

    for i in range(n1 // blk):
        xr = sa[:, i * blk:(i + 1) * blk, :].reshape(FFT_PERM_ROWS, tn)
        xi = sb[:, i * blk:(i + 1) * blk, :].reshape(FFT_PERM_ROWS, tn)
        y = (_dot(m3c_ref[...], xr) + _dot(m3s_ref[...], xi)) * scale
        o_ref[:, i * blk:(i + 1) * blk, :] = y.astype(BF16).reshape(GRID_W, blk, tn)


def _fft_constants(seq):
    n1 = seq // GRID_W
    blk = V7X_BF16_ROWS
    perm = np.zeros((FFT_PERM_ROWS, FFT_PERM_ROWS), np.float32)
    l1s, l2 = np.meshgrid(np.arange(blk), np.arange(GRID_W), indexing="ij")
    perm[(l2 * blk + l1s).ravel(), (l1s * GRID_W + l2).ravel()] = 1.0
    k1 = np.arange(n1)[None, :, None]
    l1 = np.arange(n1)[None, None, :]
    l2v = np.arange(GRID_W)[:, None, None]
    ang = 2.0 * np.pi * ((k1 * (GRID_W * l1 + l2v)) % seq) / seq
    d1 = np.concatenate([np.cos(ang), np.sin(ang)], axis=1)
    k2 = np.arange(GRID_W)[:, None]
    l2m = np.arange(GRID_W)[None, :]
    ang3 = 2.0 * np.pi * ((k2 * l2m) % GRID_W) / GRID_W
    eye = np.eye(blk)
    m3c = np.kron(np.cos(ang3), eye)
    m3s = np.kron(np.sin(ang3), eye)
    as_bf16 = lambda v: jnp.asarray(v, F32).astype(BF16)
    return as_bf16(perm), as_bf16(d1), as_bf16(m3c), as_bf16(m3s)


def _fft_call(a_w, b_w):
    seq, nw = a_w.shape
    n1 = seq // GRID_W
    tn = min(FFT_COLS, nw)
    perm, d1, m3c, m3s = _fft_constants(seq)
    scale = 1.0 / math.sqrt(seq * F_GROUP_DIM)
    const = lambda shape: pl.BlockSpec(shape, lambda j: (0,) * len(shape))
    strip = pl.BlockSpec((seq, tn), lambda j: (0, j))
    out = pl.pallas_call(
        functools.partial(_fft_kernel, n1=n1, scale=scale),
        grid=(nw // tn,),
        in_specs=[strip, strip, const(perm.shape), const(d1.shape), const(m3c.shape),
                  const(m3s.shape)],
        out_specs=pl.BlockSpec((GRID_W, n1, tn), lambda j: (0, 0, j)),
        out_shape=jax.ShapeDtypeStruct((GRID_W, n1, nw), BF16),
        scratch_shapes=[pltpu.VMEM((GRID_W, n1, tn), BF16), pltpu.VMEM((GRID_W, n1, tn), BF16)],
        compiler_params=pltpu.CompilerParams(dimension_semantics=("parallel",),
                                             vmem_limit_bytes=V7X_VMEM_LIMIT),
        name="position_dft",
    )(a_w, b_w, perm, d1, m3c, m3s)
    return out.reshape(seq, nw)


def _scan_kernel(u_ref, wg_ref, ba_ref, bx_ref, sp_ref, h0_ref, y_ref, hf_ref, a_s, b_s, h_s,
                 *, reverse, steps, heads, nb):
    @pl.when(pl.program_id(0) == 0)
    def _():
        h_s[...] = h0_ref[...]

    u = u_ref[...]
    for hd in range(heads):
        sl = slice(hd * RNN_BLOCK, (hd + 1) * RNN_BLOCK)
        ub = u[:, sl]
        g = _dot(ub, wg_ref[hd])
        r = _sigmoid(g[:, :RNN_BLOCK] + ba_ref[:, sl])
        ig = _sigmoid(g[:, RNN_BLOCK:] + bx_ref[:, sl])
        log_a = (-LRU_C) * r * sp_ref[:, sl]
        a = jnp.exp(log_a)
        mult = jnp.sqrt(-jnp.tanh(log_a) * (a * a + 1.0))
        a_s[:, sl] = a
        b_s[:, sl] = mult * ig * ub.astype(F32)

    def step(j, h):
        t = (steps - 1 - j) if reverse else j
        r0 = pl.multiple_of(t * nb, nb)
        h = a_s[pl.ds(r0, nb), :] * h + b_s[pl.ds(r0, nb), :]
        b_s[pl.ds(r0, nb), :] = h
        return h

    h = lax.fori_loop(0, steps, step, h_s[...], unroll=8)
    h_s[...] = h
    hf_ref[...] = h
    y_ref[...] = b_s[...].astype(y_ref.dtype)


def _scan_call(u, wg, ba, bx, sp, h0, reverse, name):
    n, c = u.shape
    nb = h0.shape[0]
    heads = wg.shape[0]
    tr = min(SCAN_ROWS, n)
    nt = n // tr
    const = lambda shape: pl.BlockSpec(shape, lambda i: (0,) * len(shape))
    tile = pl.BlockSpec((tr, c), (lambda i: (nt - 1 - i, 0)) if reverse else (lambda i: (i, 0)))
    return pl.pallas_call(
        functools.partial(_scan_kernel, reverse=reverse, steps=tr // nb, heads=heads, nb=nb),
        grid=(nt,),
        in_specs=[tile, const(wg.shape), const((1, c)), const((1, c)), const((1, c)),
                  const((nb, c))],
        out_specs=[tile, const((nb, c))],
        out_shape=[jax.ShapeDtypeStruct((n, c), BF16), jax.ShapeDtypeStruct((nb, c), F32)],
        scratch_shapes=[pltpu.VMEM((tr, c), F32), pltpu.VMEM((tr, c), F32),
                        pltpu.VMEM((nb, c), F32)],
        compiler_params=pltpu.CompilerParams(dimension_semantics=("arbitrary",),
                                             vmem_limit_bytes=V7X_VMEM_LIMIT),
        name=name,
    )(u, wg, ba, bx, sp, h0)


def _merge_kernel(x_ref, yf_ref, yb_ref, gg_ref, gf_ref, gr_ref, yw_ref, pi_ref, wf_ref,
                  wr_ref, wo_ref, g1_ref, n2_ref, sc2_ref, sh2_ref, wrt_ref, brt_ref, tri_ref,
                  x1_ref, hm_ref, ti_ref, tw_ref, rk_ref, cnt_ref, carry, *, nb, d_f):
    @pl.when(pl.program_id(0) == 0)
    def _():
        carry[...] = jnp.zeros_like(carry)

    yw = jnp.concatenate([yw_ref[:, bi * d_f:(bi + 1) * d_f] for bi in range(nb)], axis=0)
    yfm = _dot(pi_ref[...], yw).astype(BF16)
    f_out = _dot(yfm, wf_ref[...])
    y = yf_ref[...].astype(F32) + yb_ref[...].astype(F32)
    r_out = _dot((y * gg_ref[...].astype(F32)).astype(BF16), wr_ref[...])
    mix = gf_ref[...].astype(F32) * f_out + gr_ref[...].astype(F32) * r_out
    x1 = x_ref[...] + _per_batch(_dot(mix.astype(BF16), wo_ref[...]), g1_ref[...])
    x1_ref[...] = x1

    h2 = _modulate(_rms(x1, n2_ref[...]), sc2_ref[...], sh2_ref[...])
    hm_ref[...] = _pack_bf16_pairs(h2)

    lg = _dot3(h2, wrt_ref[...]) + brt_ref[...]
    tm, ne = lg.shape
    lane = lax.broadcasted_iota(jnp.int32, (tm, ne), 1)
    lane_k = lax.broadcasted_iota(jnp.int32, (tm, TOP_K), 1)
    vals, hots = [], []
    top_i = jnp.zeros((tm, TOP_K), jnp.int32)
    for k in range(TOP_K):
        m = jnp.max(lg, axis=-1, keepdims=True)
        idx = jnp.min(jnp.where(lg == m, lane, ne), axis=-1, keepdims=True)
        hit = lane == idx
        vals.append(m)
        hots.append(hit.astype(F32))
        top_i = jnp.where(lane_k == k, idx, top_i)
        lg = jnp.where(hit, -jnp.inf, lg)
    exps = [jnp.exp(v - vals[0]) for v in vals]
    inv = 1.0 / (exps[0] + exps[1] + exps[2] + exps[3])
    top_w = jnp.zeros((tm, TOP_K), F32)
    for k in range(TOP_K):
        top_w = jnp.where(lane_k == k, exps[k] * inv, top_w)
    ti_ref[...] = top_i
    tw_ref[...] = top_w

    per_tok = hots[0] + hots[1] + hots[2] + hots[3]
    before = _dot(tri_ref[...], per_tok.astype(BF16)) + carry[...]
    rank = jnp.zeros((tm, TOP_K), F32)
    for k in range(TOP_K):
        rank = jnp.where(lane_k == k, jnp.sum(hots[k] * before, axis=-1, keepdims=True), rank)
    rk_ref[...] = rank.astype(jnp.int32)
    carry[...] = carry[...] + jnp.sum(per_tok, axis=0, keepdims=True)
    cnt_ref[...] = carry[...]


def _merge_call(xt, y_f, y_b, gg, gf, gr, yw, pint, w_f, w_r, w_o, g1, n2, sc2, sh2, w_rt, b_rt,
                tri, nb, d_f):
    n, d = xt.shape
    d_rnn = y_f.shape[1]
    ne = w_rt.shape[1]
    nt = n // ROW_TILE
    const = lambda shape: pl.BlockSpec(shape, lambda i: (0,) * len(shape))
    row = lambda c: pl.BlockSpec((ROW_TILE, c), lambda i: (i, 0))
    return pl.pallas_call(
        functools.partial(_merge_kernel, nb=nb, d_f=d_f),
        grid=(nt,),
        in_specs=[row(d), row(d_rnn), row(d_rnn), row(d_rnn), row(d), row(d),
                  pl.BlockSpec((GRID_W, nb * d_f), lambda i: (i, 0)),
                  const((ROW_TILE, ROW_TILE)), const((d_f, d)), const((d_rnn, d)), const((d, d)),
                  const((nb, d)), const((1, d)), const((nb, d)), const((nb, d)),
                  const((d, ne)), const((1, ne)), const((ROW_TILE, ROW_TILE))],
        out_specs=[row(d), row(d // 2), row(TOP_K), row(TOP_K), row(TOP_K), const((1, ne))],
        out_shape=[jax.ShapeDtypeStruct((n, d), F32), jax.ShapeDtypeStruct((n, d // 2), jnp.int32),
                   jax.ShapeDtypeStruct((n, TOP_K), jnp.int32),
                   jax.ShapeDtypeStruct((n, TOP_K), F32),
                   jax.ShapeDtypeStruct((n, TOP_K), jnp.int32),
                   jax.ShapeDtypeStruct((1, ne), F32)],
        scratch_shapes=[pltpu.VMEM((1, ne), F32)],
        compiler_params=pltpu.CompilerParams(dimension_semantics=("arbitrary",),
                                             vmem_limit_bytes=V7X_VMEM_LIMIT),
        name="merge_and_route",
    )(xt, y_f, y_b, gg, gf, gr, yw, pint, w_f, w_r, w_o, g1, n2, sc2, sh2, w_rt, b_rt, tri)


def _sc_mesh():
    return plsc.VectorSubcoreMesh(core_axis_name="core", subcore_axis_name="subcore")


def _sc_scatter_rows(x, idx_rows, n_out):
    n, c = x.shape
    kk = len(idx_rows)
    mesh = _sc_mesh()
    workers = mesh.num_cores * mesh.num_subcores
    per_worker = n // workers
    assert per_worker % SC_WINDOW == 0

    @functools.partial(
        pl.kernel, out_type=jax.ShapeDtypeStruct((n_out, c), x.dtype), mesh=mesh,
        scratch_types=[pltpu.VMEM((SC_WINDOW, c), x.dtype)]
        + [pltpu.VMEM((SC_WINDOW,), jnp.int32)] * kk + [pltpu.SemaphoreType.DMA],
        name="sc_dispatch_scatter")
    def scatter(x_hbm, *refs):
        idx_hbm, o_hbm, rows_v = refs[:kk], refs[kk], refs[kk + 1]
        idx_v, sem = refs[kk + 2:2 * kk + 2], refs[2 * kk + 2]
        wid = lax.axis_index("subcore") * mesh.num_cores + lax.axis_index("core")

        @pl.loop(0, per_worker // SC_WINDOW)
        def _(j):
            base = pl.multiple_of(wid * per_worker + j * SC_WINDOW, SC_WINDOW)
            pltpu.sync_copy(x_hbm.at[pl.ds(base, SC_WINDOW)], rows_v)
            for ih, iv in zip(idx_hbm, idx_v):
                pltpu.sync_copy(ih.at[pl.ds(base, SC_WINDOW)], iv)
            copies = [pltpu.async_copy(rows_v, o_hbm.at[iv], sem) for iv in idx_v]
            for cp in copies:
                cp.wait()

    return scatter(x, *idx_rows)


def _sc_gather_rows(src, idx):
    n_out = idx.shape[0]
    c = src.shape[1]
    mesh = _sc_mesh()
    workers = mesh.num_cores * mesh.num_subcores
    per_worker = n_out // workers
    assert per_worker % SC_WINDOW == 0

    @functools.partial(
        pl.kernel, out_type=jax.ShapeDtypeStruct((n_out, c), src.dtype), mesh=mesh,
        scratch_types=[pltpu.VMEM((SC_WINDOW, c), src.dtype), pltpu.VMEM((SC_WINDOW,), jnp.int32),
                       pltpu.SemaphoreType.DMA],
        name="sc_combine_gather")
    def gather(src_hbm, idx_hbm, o_hbm, rows_v, idx_v, sem):
        wid = lax.axis_index("subcore") * mesh.num_cores + lax.axis_index("core")

        @pl.loop(0, per_worker // SC_WINDOW)
        def _(j):
            base = pl.multiple_of(wid * per_worker + j * SC_WINDOW, SC_WINDOW)
            pltpu.sync_copy(idx_hbm.at[pl.ds(base, SC_WINDOW)], idx_v)
            pltpu.async_copy(src_hbm.at[idx_v], rows_v, sem).wait()
            pltpu.sync_copy(rows_v, o_hbm.at[pl.ds(base, SC_WINDOW)])

    return gather(src, idx)


def _expert_kernel(be_ref, nv_ref, x_ref, wgu_ref, bgu_ref, wd_ref, bd_ref, y_ref, *, d_e):
    valid = nv_ref[pl.program_id(0)]

    @pl.when(valid > 0)
    def _():
        rows = lax.broadcasted_iota(jnp.int32, x_ref.shape, 0)
        x = _unpack_bf16_pairs(jnp.where(rows < valid, x_ref[...], 0)).astype(BF16)
        gu = _dot(x, wgu_ref[...]) + bgu_ref[...]
        gate = jnp.minimum(gu[:, :d_e], SWIGLU_LIMIT)
        up = jnp.clip(gu[:, d_e:], -SWIGLU_LIMIT, SWIGLU_LIMIT)
        act = (up + 1.0) * gate * _sigmoid(SWIGLU_ALPHA * gate)
        y_ref[...] = _pack_bf16_pairs(_dot(act.astype(BF16), wd_ref[...]) + bd_ref[...])

    @pl.when(valid <= 0)
    def _():
        y_ref[...] = jnp.zeros_like(y_ref)


def _expert_call(xs, block_expert, block_valid, w_gu, b_gu, w_down, b_down):
    p, dh = xs.shape
    ne, d, d_e2 = w_gu.shape
    d_e = d_e2 // 2
    nblk = p // MOE_ROWS
    grid_spec = pltpu.PrefetchScalarGridSpec(
        num_scalar_prefetch=2,
        grid=(nblk,),
        in_specs=[pl.BlockSpec((MOE_ROWS, dh), lambda i, be, nv: (i, 0)),
                  pl.BlockSpec((None, d, d_e2), lambda i, be, nv: (be[i], 0, 0)),
                  pl.BlockSpec((None, 1, d_e2), lambda i, be, nv: (be[i], 0, 0)),
                  pl.BlockSpec((None, d_e, d), lambda i, be, nv: (be[i], 0, 0)),
                  pl.BlockSpec((None, 1, d), lambda i, be, nv: (be[i], 0, 0))],
        out_specs=pl.BlockSpec((MOE_ROWS, dh), lambda i, be, nv: (i, 0)),
    )
    return pl.pallas_call(
        functools.partial(_expert_kernel, d_e=d_e),
        grid_spec=grid_spec,
        out_shape=jax.ShapeDtypeStruct((p, dh), jnp.int32),
        compiler_params=pltpu.CompilerParams(dimension_semantics=("arbitrary",),
                                             vmem_limit_bytes=V7X_VMEM_LIMIT),
        name="expert_ffn",
    )(block_expert, block_valid, xs, w_gu, b_gu.reshape(ne, 1, d_e2), w_down,
      b_down.reshape(ne, 1, d))


def _combine_kernel(x1_ref, yg_ref, tw_ref, g2_ref, nf_ref, o_ref):
    w = tw_ref[...]
    moe = w[:, 0:1] * _unpack_bf16_pairs(yg_ref[0])
    for k in range(1, TOP_K):
        moe = moe + w[:, k:k + 1] * _unpack_bf16_pairs(yg_ref[k])
    o_ref[...] = _rms(x1_ref[...] + _per_batch(moe, g2_ref[...]), nf_ref[...])


def _combine_call(x1, yg, top_w, g2, nf):
    n, d = x1.shape
    nb = g2.shape[0]
    const = lambda shape: pl.BlockSpec(shape, lambda i: (0,) * len(shape))
    row = lambda c: pl.BlockSpec((ROW_TILE, c), lambda i: (i, 0))
    return pl.pallas_call(
        _combine_kernel,
        grid=(n // ROW_TILE,),
        in_specs=[row(d), pl.BlockSpec((TOP_K, ROW_TILE, d // 2), lambda i: (0, i, 0)),
                  row(TOP_K), const((nb, d)), const((1, d))],
        out_specs=row(d),
        out_shape=jax.ShapeDtypeStruct((n, d), F32),
        compiler_params=pltpu.CompilerParams(dimension_semantics=("parallel",),
                                             vmem_limit_bytes=V7X_VMEM_LIMIT),
        name="combine_final_norm",
    )(x1, yg, top_w, g2, nf)


def _tile_constants(nb):
    t = np.arange(GRID_W)[:, None]
    b = np.arange(nb)[None, :]
    deint = np.zeros((ROW_TILE, ROW_TILE), np.float32)
    deint[(b * GRID_W + t).ravel(), (t * nb + b).ravel()] = 1.0
    c = np.arange(F_GROUP_DIM)
    ang = 2.0 * np.pi * ((c[:, None] * c[None, :]) % F_GROUP_DIM) / F_GROUP_DIM
    cs = np.concatenate([np.cos(ang), np.sin(ang)], axis=1)
    tri = np.tril(np.ones((ROW_TILE, ROW_TILE), np.float32), -1)
    as_bf16 = lambda v: jnp.asarray(v, F32).astype(BF16)
    return as_bf16(deint), as_bf16(deint.T), as_bf16(cs), as_bf16(tri)


def kernel(x, c, ctx, c_ctx, w_ada, b_ada, norm1, w_in, conv_w, conv_b, gate_a_w, gate_a_b,
           gate_x_w, gate_x_b, lru_lambda, w_fourier, w_rnn, w_out, norm2, w_router, b_router,
           w_gu, b_gu, w_down, b_down, norm_f):
    nb, seq, d = x.shape
    depth = w_ada.shape[0]
    assert depth == 1 and nb == V7X_SUBLANES and seq % FFT_PERM_ROWS == 0
    n = nb * seq
    d_rnn = conv_w.shape[-1]
    d_f = w_fourier.shape[1]
    ne = w_router.shape[-1]
    lyr = 0
    deint, inter, cs, tri = _tile_constants(nb)

    xt = jnp.transpose(x, (1, 0, 2)).reshape(n, d)
    ctx_t = jnp.transpose(ctx, (1, 0, 2)).reshape(-1, d)

    cond = jnp.zeros((2 * V7X_SUBLANES, d), F32).at[:nb].set(c).at[nb].set(c_ctx)
    mods = _ada_call(cond, w_ada[lyr], b_ada[lyr].reshape(1, -1))
    sh1, sc1, g1, sh2, sc2, g2 = [mods[:nb, k * d:(k + 1) * d] for k in range(6)]
    csh1, csc1 = mods[nb:nb + 1, 0:d], mods[nb:nb + 1, d:2 * d]

    w_in_b = w_in[lyr].astype(BF16)
    cw, cb = conv_w[lyr], conv_b[lyr].reshape(1, d_rnn)
    n1r = norm1[lyr].reshape(1, d)
    wg = jnp.concatenate([gate_a_w[lyr], gate_x_w[lyr]], axis=-1).astype(BF16)
    sp = jax.nn.softplus(-lru_lambda[lyr].astype(F32))
    scan = lambda u, dr, h0, name: _scan_call(
        u, wg[dr], gate_a_b[lyr, dr].reshape(1, -1), gate_x_b[lyr, dr].reshape(1, -1),
        sp[dr].reshape(1, -1), h0, bool(dr), name)

    u_ctx = _ctx_in_call(ctx_t, csc1, csh1, n1r, w_in_b[:, d_f:d_f + d_rnn], cw, cb, nb)
    h_zero = jnp.zeros((nb, d_rnn), F32)
    _, h_ctx_f = scan(u_ctx, 0, h_zero, "ctx_scan_fwd")
    _, h_ctx_b = scan(u_ctx, 1, h_zero, "ctx_scan_bwd")

    a_w, b_w, u, gg, gf, gr = _in_proj_call(xt, sc1, sh1, n1r, w_in_b, cw, cb, deint, cs, nb,
                                            d_f, d_rnn)
    y_wide = _fft_call(a_w, b_w)
    y_f, _ = scan(u, 0, h_ctx_f, "scan_fwd")
    y_b, _ = scan(u, 1, h_ctx_b, "scan_bwd")
    x1, hmod, top_i, top_w, rank, counts = _merge_call(
        xt, y_f, y_b, gg, gf, gr, y_wide, inter, w_fourier[lyr].astype(BF16),
        w_rnn[lyr].astype(BF16), w_out[lyr].astype(BF16), g1, norm2[lyr].reshape(1, d), sc2, sh2,
        w_router[lyr], b_router[lyr].reshape(1, ne), tri, nb, d_f)

    cnt = counts.reshape(ne).astype(jnp.int32)
    padded = (cnt + MOE_ROWS - 1) // MOE_ROWS * MOE_ROWS
    ends = jnp.cumsum(padded)
    starts = ends - padded
    expert_ids = jnp.arange(ne, dtype=jnp.int32)
    start_of = jnp.sum(jnp.where(top_i[..., None] == expert_ids, starts, 0), axis=-1)
    dest_km = (start_of + rank).T
    nblk = (n * TOP_K + ne * (MOE_ROWS - 1) + MOE_ROWS - 1) // MOE_ROWS
    block_start = jnp.arange(nblk, dtype=jnp.int32) * MOE_ROWS
    block_expert = jnp.minimum(jnp.sum(block_start[:, None] >= ends[None, :], axis=1),
                               ne - 1).astype(jnp.int32)
    block_valid = jnp.clip((starts + cnt)[block_expert] - block_start, 0, MOE_ROWS)
    block_valid = jnp.where(block_start < ends[-1], block_valid, 0).astype(jnp.int32)

    xs = _sc_scatter_rows(hmod, [dest_km[k] for k in range(TOP_K)], nblk * MOE_ROWS)
    ys = _expert_call(xs, block_expert, block_valid, w_gu[lyr].astype(BF16), b_gu[lyr],
                      w_down[lyr].astype(BF16), b_down[lyr])
    yg = _sc_gather_rows(ys, dest_km.reshape(TOP_K * n)).reshape(TOP_K, n, d // 2)
    out_t = _combine_call(x1, yg, top_w, g2, norm_f.reshape(1, d))
    return jnp.transpose(out_t.reshape(seq, nb, d), (1, 0, 2))
```

```python
import functools
import math

import numpy as np
import jax
import jax.numpy as jnp
from jax import lax
from jax.experimental import pallas as pl
from jax.experimental.pallas import tpu as pltpu
from jax.experimental.pallas import tpu_sc as plsc

F32 = jnp.float32
BF16 = jnp.bfloat16

GRID_W = 64
CONV_W = 4
LRU_C = 8.0
TOP_K = 4
F_GROUP_DIM = 128
RNN_BLOCK = 128
SWIGLU_ALPHA = 1.702
SWIGLU_LIMIT = 7.0
EPS = 1e-6

V7X_LANES = 128
V7X_SUBLANES = 8
V7X_BF16_ROWS = 16
V7X_VMEM_LIMIT = 56 * 1024 * 1024

ROW_TILE = GRID_W * V7X_SUBLANES
FFT_PERM_ROWS = V7X_BF16_ROWS * GRID_W
FFT_COLS = 256
SCAN_ROWS = 2048
MOE_ROWS = 512
SC_WINDOW = 128
HIGH_HALF = -65536
LOW_HALF = 65535


def _dot(a, b):
    return jnp.dot(a, b, preferred_element_type=F32)


def _split_bf16(a):
    hi = a.astype(BF16)
    return hi, (a - hi.astype(F32)).astype(BF16)


def _dot3(a, b):
    ah, al = _split_bf16(a)
    bh, bl = _split_bf16(b)
    return _dot(ah, bh) + _dot(ah, bl) + _dot(al, bh)


def _sigmoid(x):
    return 0.5 * (jnp.tanh(0.5 * x) + 1.0)


def _gelu_tanh(x):
    c = math.sqrt(2.0 / math.pi)
    return 0.5 * x * (1.0 + jnp.tanh(c * (x + 0.044715 * (x * x * x))))


def _pack_bf16_pairs(v):
    c = v.shape[1] // 2
    bits = pltpu.bitcast(v.astype(BF16).astype(F32), jnp.int32)
    return (bits[:, :c] & HIGH_HALF) | ((bits[:, c:] >> 16) & LOW_HALF)


def _unpack_bf16_pairs(p):
    hi = pltpu.bitcast(p & HIGH_HALF, F32)
    lo = pltpu.bitcast(p << 16, F32)
    return jnp.concatenate([hi, lo], axis=1)


def _rms(x, g):
    return x * lax.rsqrt(jnp.mean(x * x, axis=-1, keepdims=True) + EPS) * g


def _modulate(xn, sc, sh):
    if sc.shape[0] == 1:
        return xn * (1.0 + sc) + sh
    r, d = xn.shape
    nb = sc.shape[0]
    x3 = xn.reshape(r // nb, nb, d)
    return (x3 * (1.0 + sc)[None] + sh[None]).reshape(r, d)


def _to_time_major(x3):
    nb, t, d = x3.shape
    return pltpu.einshape("btd->tbd", x3).reshape(t * nb, d)


def _to_batch_major(v, nb):
    r, d = v.shape
    return pltpu.einshape("tbd->btd", v.reshape(r // nb, nb, d))


def _per_batch(v, g):
    r, d = v.shape
    nb = g.shape[0]
    return (v.reshape(r // nb, nb, d) * g[None]).reshape(r, d)


def _conv_time_major(z, cw, cb, nb):
    r, c = z.shape

    def shifted(s):
        k = abs(s) * nb
        zero = jnp.zeros((k, c), z.dtype)
        if s > 0:
            return jnp.concatenate([zero, z[:r - k]], axis=0)
        return jnp.concatenate([z[k:], zero], axis=0)

    return (cb + shifted(2) * cw[0:1] + shifted(1) * cw[1:2] + z * cw[2:3]
            + shifted(-1) * cw[3:4])


def _ada_kernel(c_ref, w_ref, b_ref, o_ref):
    cnd = c_ref[...]
    o_ref[...] = _dot3(cnd * _sigmoid(cnd), w_ref[...]) + b_ref[...]


def _ada_call(cond, w, b):
    r, d = cond.shape
    n = w.shape[1]
    return pl.pallas_call(
        _ada_kernel,
        grid=(n // d,),
        in_specs=[pl.BlockSpec((r, d), lambda j: (0, 0)),
                  pl.BlockSpec((d, d), lambda j: (0, j)),
                  pl.BlockSpec((1, d), lambda j: (0, j))],
        out_specs=pl.BlockSpec((r, d), lambda j: (0, j)),
        out_shape=jax.ShapeDtypeStruct((r, n), F32),
        name="ada_modulation",
    )(cond, w, b)


def _ctx_in_kernel(x_ref, sc_ref, sh_ref, g_ref, w_ref, cw_ref, cb_ref, u_ref, *, nb):
    h = _modulate(_rms(x_ref[...], g_ref[...]), sc_ref[...], sh_ref[...])
    z = _dot(h.astype(BF16), w_ref[...])
    u_ref[...] = _conv_time_major(z, cw_ref[...], cb_ref[...], nb).astype(u_ref.dtype)


def _ctx_in_call(ctx_t, sc, sh, g, w_r, cw, cb, nb):
    r, d = ctx_t.shape
    c = w_r.shape[1]
    full = lambda shape: pl.BlockSpec(shape, lambda i: (0,) * len(shape))
    return pl.pallas_call(
        functools.partial(_ctx_in_kernel, nb=nb),
        grid=(1,),
        in_specs=[full((r, d)), full((1, d)), full((1, d)), full((1, d)), full((d, c)),
                  full((CONV_W, c)), full((1, c))],
        out_specs=full((r, c)),
        out_shape=jax.ShapeDtypeStruct((r, c), BF16),
        compiler_params=pltpu.CompilerParams(vmem_limit_bytes=V7X_VMEM_LIMIT),
        name="ctx_in_proj",
    )(ctx_t, sc, sh, g, w_r, cw, cb)


def _in_proj_kernel(x_ref, sc_ref, sh_ref, g_ref, w_ref, cw_ref, cb_ref, pd_ref, cs_ref,
                    a_ref, b_ref, u_ref, gg_ref, gf_ref, gr_ref, *, nb, d_f, d_rnn):
    d = x_ref.shape[-1]
    x = _to_time_major(x_ref[...])
    h = _modulate(_rms(x, g_ref[...]), sc_ref[...], sh_ref[...]).astype(BF16)
    c_r, c_g, c_m = d_f, d_f + d_rnn, d_f + 2 * d_rnn

    zf = _dot(h, w_ref[:, 0:c_r]).astype(BF16)
    zf = _dot(pd_ref[...], zf).astype(BF16)
    for grp in range(d_f // F_GROUP_DIM):
        ab = _dot(zf[:, grp * F_GROUP_DIM:(grp + 1) * F_GROUP_DIM], cs_ref[...])
        for bi in range(nb):
            rows = slice(bi * GRID_W, (bi + 1) * GRID_W)
            col = bi * d_f + grp * F_GROUP_DIM
            a_ref[:, col:col + F_GROUP_DIM] = ab[rows, :F_GROUP_DIM].astype(BF16)
            b_ref[:, col:col + F_GROUP_DIM] = ab[rows, F_GROUP_DIM:].astype(BF16)

    zr = _dot(h, w_ref[:, c_r:c_g])
    u_ref[...] = _conv_time_major(zr, cw_ref[...], cb_ref[...], nb).astype(BF16)
    gg_ref[...] = _gelu_tanh(_dot(h, w_ref[:, c_g:c_m])).astype(BF16)
    gf_ref[...] = _sigmoid(_dot(h, w_ref[:, c_m:c_m + d])).astype(BF16)
    gr_ref[...] = _sigmoid(_dot(h, w_ref[:, c_m + d:c_m + 2 * d])).astype(BF16)


def _in_proj_call(x, sc, sh, g, w_in, cw, cb, pd, cs, d_f, d_rnn):
    nb, seq, d = x.shape
    n = nb * seq
    d_in = w_in.shape[1]
    nt = n // ROW_TILE
    const = lambda shape: pl.BlockSpec(shape, lambda i: (0,) * len(shape))
    row = lambda c: pl.BlockSpec((ROW_TILE, c), lambda i: (i, 0))
    wide = pl.BlockSpec((GRID_W, nb * d_f), lambda i: (i, 0))
    return pl.pallas_call(
        functools.partial(_in_proj_kernel, nb=nb, d_f=d_f, d_rnn=d_rnn),
        grid=(nt,),
        in_specs=[pl.BlockSpec((nb, GRID_W, d), lambda i: (0, i, 0)), const((nb, d)),
                  const((nb, d)), const((1, d)), const((d, d_in)),
                  const((CONV_W, d_rnn)), const((1, d_rnn)), const((ROW_TILE, ROW_TILE)),
                  const((F_GROUP_DIM, 2 * F_GROUP_DIM))],
        out_specs=[wide, wide, row(d_rnn), row(d_rnn), row(d), row(d)],
        out_shape=[jax.ShapeDtypeStruct((seq, nb * d_f), BF16),
                   jax.ShapeDtypeStruct((seq, nb * d_f), BF16),
                   jax.ShapeDtypeStruct((n, d_rnn), BF16),
                   jax.ShapeDtypeStruct((n, d_rnn), BF16),
                   jax.ShapeDtypeStruct((n, d), BF16),
                   jax.ShapeDtypeStruct((n, d), BF16)],
        compiler_params=pltpu.CompilerParams(dimension_semantics=("parallel",),
                                             vmem_limit_bytes=V7X_VMEM_LIMIT),
        name="latent_in_proj",
    )(x, sc, sh, g, w_in, cw, cb, pd, cs)


def _fft_kernel(a_ref, b_ref, p_ref, d1_ref, m3c_ref, m3s_ref, o_ref, sa, sb, *, n1, scale):
    tn = a_ref.shape[1]
    blk = V7X_BF16_ROWS
    for j in range(n1 // blk):
        rows = slice(j * FFT_PERM_ROWS, (j + 1) * FFT_PERM_ROWS)
        sa[:, j * blk:(j + 1) * blk, :] = (
            _dot(p_ref[...], a_ref[rows, :]).astype(BF16).reshape(GRID_W, blk, tn))
        sb[:, j * blk:(j + 1) * blk, :] = (
            _dot(p_ref[...], b_ref[rows, :]).astype(BF16).reshape(GRID_W, blk, tn))

    def stage1(l2, carry):
        dm = d1_ref[l2]
        ca = _dot(dm, sa[l2])
        cb = _dot(dm, sb[l2])
        sa[l2] = (ca[:n1] - cb[n1:]).astype(BF16)
        sb[l2] = (-(cb[:n1] + ca[n1:])).astype(BF16)
        return carry

    lax.fori_loop(0, GRID_W, stage1, 0)

    for i in range(n1 // blk):
        xr = sa[:, i * blk:(i + 1) * blk, :].reshape(FFT_PERM_ROWS, tn)
        xi = sb[:, i * blk:(i + 1) * blk, :].reshape(FFT_PERM_ROWS, tn)
        y = (_dot(m3c_ref[...], xr) + _dot(m3s_ref[...], xi)) * scale
        o_ref[:, i * blk:(i + 1) * blk, :] = y.astype(BF16).reshape(GRID_W, blk, tn)


def _fft_constants(seq):
    n1 = seq // GRID_W
    blk = V7X_BF16_ROWS
    perm = np.zeros((FFT_PERM_ROWS, FFT_PERM_ROWS), np.float32)
    l1s, l2 = np.meshgrid(np.arange(blk), np.arange(GRID_W), indexing="ij")
    perm[(l2 * blk + l1s).ravel(), (l1s * GRID_W + l2).ravel()] = 1.0
    k1 = np.arange(n1)[None, :, None]
    l1 = np.arange(n1)[None, None, :]
    l2v = np.arange(GRID_W)[:, None, None]
    ang = 2.0 * np.pi * ((k1 * (GRID_W * l1 + l2v)) % seq) / seq
    d1 = np.concatenate([np.cos(ang), np.sin(ang)], axis=1)
    k2 = np.arange(GRID_W)[:, None]
    l2m = np.arange(GRID_W)[None, :]
    ang3 = 2.0 * np.pi * ((k2 * l2m) % GRID_W) / GRID_W
    eye = np.eye(blk)
    m3c = np.kron(np.cos(ang3), eye)
    m3s = np.kron(np.sin(ang3), eye)
    as_bf16 = lambda v: jnp.asarray(v, F32).astype(BF16)
    return as_bf16(perm), as_bf16(d1), as_bf16(m3c), as_bf16(m3s)


def _fft_call(a_w, b_w):
    seq, nw = a_w.shape
    n1 = seq // GRID_W
    tn = min(FFT_COLS, nw)
    perm, d1, m3c, m3s = _fft_constants(seq)
    scale = 1.0 / math.sqrt(seq * F_GROUP_DIM)
    const = lambda shape: pl.BlockSpec(shape, lambda j: (0,) * len(shape))
    strip = pl.BlockSpec((seq, tn), lambda j: (0, j))
    out = pl.pallas_call(
        functools.partial(_fft_kernel, n1=n1, scale=scale),
        grid=(nw // tn,),
        in_specs=[strip, strip, const(perm.shape), const(d1.shape), const(m3c.shape),
                  const(m3s.shape)],
        out_specs=pl.BlockSpec((GRID_W, n1, tn), lambda j: (0, 0, j)),
        out_shape=jax.ShapeDtypeStruct((GRID_W, n1, nw), BF16),
        scratch_shapes=[pltpu.VMEM((GRID_W, n1, tn), BF16), pltpu.VMEM((GRID_W, n1, tn), BF16)],
        compiler_params=pltpu.CompilerParams(dimension_semantics=("parallel",),
                                             vmem_limit_bytes=V7X_VMEM_LIMIT),
        name="position_dft",
    )(a_w, b_w, perm, d1, m3c, m3s)
    return out.reshape(seq, nw)


def _scan_kernel(u_ref, wg_ref, ba_ref, bx_ref, sp_ref, h0_ref, y_ref, hf_ref, a_s, b_s, h_s,
                 *, reverse, steps, heads, nb):
    @pl.when(pl.program_id(0) == 0)
    def _():
        h_s[...] = h0_ref[...]

    u = u_ref[...]
    for hd in range(heads):
        sl = slice(hd * RNN_BLOCK, (hd + 1) * RNN_BLOCK)
        ub = u[:, sl]
        g = _dot(ub, wg_ref[hd])
        r = _sigmoid(g[:, :RNN_BLOCK] + ba_ref[:, sl])
        ig = _sigmoid(g[:, RNN_BLOCK:] + bx_ref[:, sl])
        log_a = (-LRU_C) * r * sp_ref[:, sl]
        a = jnp.exp(log_a)
        mult = jnp.sqrt(-jnp.tanh(log_a) * (a * a + 1.0))
        a_s[:, sl] = a
        b_s[:, sl] = mult * ig * ub.astype(F32)

    def step(j, h):
        t = (steps - 1 - j) if reverse else j
        r0 = pl.multiple_of(t * nb, nb)
        h = a_s[pl.ds(r0, nb), :] * h + b_s[pl.ds(r0, nb), :]
        b_s[pl.ds(r0, nb), :] = h
        return h

    h = lax.fori_loop(0, steps, step, h_s[...], unroll=8)
    h_s[...] = h
    hf_ref[...] = h
    y_ref[...] = b_s[...].astype(y_ref.dtype)


def _scan_call(u, wg, ba, bx, sp, h0, reverse, name):
    n, c = u.shape
    nb = h0.shape[0]
    heads = wg.shape[0]
    tr = min(SCAN_ROWS, n)
    nt = n // tr
    const = lambda shape: pl.BlockSpec(shape, lambda i: (0,) * len(shape))
    tile = pl.BlockSpec((tr, c), (lambda i: (nt - 1 - i, 0)) if reverse else (lambda i: (i, 0)))
    return pl.pallas_call(
        functools.partial(_scan_kernel, reverse=reverse, steps=tr // nb, heads=heads, nb=nb),
        grid=(nt,),
        in_specs=[tile, const(wg.shape), const((1, c)), const((1, c)), const((1, c)),
                  const((nb, c))],
        out_specs=[tile, const((nb, c))],
        out_shape=[jax.ShapeDtypeStruct((n, c), BF16), jax.ShapeDtypeStruct((nb, c), F32)],
        scratch_shapes=[pltpu.VMEM((tr, c), F32), pltpu.VMEM((tr, c), F32),
                        pltpu.VMEM((nb, c), F32)],
        compiler_params=pltpu.CompilerParams(dimension_semantics=("arbitrary",),
                                             vmem_limit_bytes=V7X_VMEM_LIMIT),
        name=name,
    )(u, wg, ba, bx, sp, h0)


def _merge_kernel(x_ref, yf_ref, yb_ref, gg_ref, gf_ref, gr_ref, yw_ref, pi_ref, wf_ref,
                  wr_ref, wo_ref, g1_ref, n2_ref, sc2_ref, sh2_ref, wrt_ref, brt_ref, tri_ref,
                  x1_ref, hm_ref, ti_ref, tw_ref, rk_ref, cnt_ref, carry, *, nb, d_f):
    @pl.when(pl.program_id(0) == 0)
    def _():
        carry[...] = jnp.zeros_like(carry)

    yw = jnp.concatenate([yw_ref[:, bi * d_f:(bi + 1) * d_f] for bi in range(nb)], axis=0)
    yfm = _dot(pi_ref[...], yw).astype(BF16)
    f_out = _dot(yfm, wf_ref[...])
    y = yf_ref[...].astype(F32) + yb_ref[...].astype(F32)
    r_out = _dot((y * gg_ref[...].astype(F32)).astype(BF16), wr_ref[...])
    mix = gf_ref[...].astype(F32) * f_out + gr_ref[...].astype(F32) * r_out
    x1 = _to_time_major(x_ref[...]) + _per_batch(_dot(mix.astype(BF16), wo_ref[...]), g1_ref[...])
    x1_ref[...] = x1

    h2 = _modulate(_rms(x1, n2_ref[...]), sc2_ref[...], sh2_ref[...])
    hm_ref[...] = _pack_bf16_pairs(h2)

    lg = _dot3(h2, wrt_ref[...]) + brt_ref[...]
    tm, ne = lg.shape
    lane = lax.broadcasted_iota(jnp.int32, (tm, ne), 1)
    lane_k = lax.broadcasted_iota(jnp.int32, (tm, TOP_K), 1)
    vals, hots = [], []
    top_i = jnp.zeros((tm, TOP_K), jnp.int32)
    for k in range(TOP_K):
        m = jnp.max(lg, axis=-1, keepdims=True)
        idx = jnp.min(jnp.where(lg == m, lane, ne), axis=-1, keepdims=True)
        hit = lane == idx
        vals.append(m)
        hots.append(hit.astype(F32))
        top_i = jnp.where(lane_k == k, idx, top_i)
        lg = jnp.where(hit, -jnp.inf, lg)
    exps = [jnp.exp(v - vals[0]) for v in vals]
    inv = 1.0 / (exps[0] + exps[1] + exps[2] + exps[3])
    top_w = jnp.zeros((tm, TOP_K), F32)
    for k in range(TOP_K):
        top_w = jnp.where(lane_k == k, exps[k] * inv, top_w)
    ti_ref[...] = top_i
    tw_ref[...] = top_w

    per_tok = hots[0] + hots[1] + hots[2] + hots[3]
    before = _dot(tri_ref[...], per_tok.astype(BF16)) + carry[...]
    rank = jnp.zeros((tm, TOP_K), F32)
    for k in range(TOP_K):
        rank = jnp.where(lane_k == k, jnp.sum(hots[k] * before, axis=-1, keepdims=True), rank)
    rk_ref[...] = rank.astype(jnp.int32)
    carry[...] = carry[...] + jnp.sum(per_tok, axis=0, keepdims=True)
    cnt_ref[...] = carry[...]


def _merge_call(x, y_f, y_b, gg, gf, gr, yw, pint, w_f, w_r, w_o, g1, n2, sc2, sh2, w_rt, b_rt,
                tri, d_f):
    nb, seq, d = x.shape
    n = nb * seq
    d_rnn = y_f.shape[1]
    ne = w_rt.shape[1]
    nt = n // ROW_TILE
    const = lambda shape: pl.BlockSpec(shape, lambda i: (0,) * len(shape))
    row = lambda c: pl.BlockSpec((ROW_TILE, c), lambda i: (i, 0))
    return pl.pallas_call(
        functools.partial(_merge_kernel, nb=nb, d_f=d_f),
        grid=(nt,),
        in_specs=[pl.BlockSpec((nb, GRID_W, d), lambda i: (0, i, 0)), row(d_rnn), row(d_rnn),
                  row(d_rnn), row(d), row(d), pl.BlockSpec((GRID_W, nb * d_f), lambda i: (i, 0)),
                  const((ROW_TILE, ROW_TILE)), const((d_f, d)), const((d_rnn, d)), const((d, d)),
                  const((nb, d)), const((1, d)), const((nb, d)), const((nb, d)),
                  const((d, ne)), const((1, ne)), const((ROW_TILE, ROW_TILE))],
        out_specs=[row(d), row(d // 2), row(TOP_K), row(TOP_K), row(TOP_K), const((1, ne))],
        out_shape=[jax.ShapeDtypeStruct((n, d), F32), jax.ShapeDtypeStruct((n, d // 2), jnp.int32),
                   jax.ShapeDtypeStruct((n, TOP_K), jnp.int32),
                   jax.ShapeDtypeStruct((n, TOP_K), F32),
                   jax.ShapeDtypeStruct((n, TOP_K), jnp.int32),
                   jax.ShapeDtypeStruct((1, ne), F32)],
        scratch_shapes=[pltpu.VMEM((1, ne), F32)],
        compiler_params=pltpu.CompilerParams(dimension_semantics=("arbitrary",),
                                             vmem_limit_bytes=V7X_VMEM_LIMIT),
        name="merge_and_route",
    )(x, y_f, y_b, gg, gf, gr, yw, pint, w_f, w_r, w_o, g1, n2, sc2, sh2, w_rt, b_rt, tri)


def _sc_mesh():
    return plsc.VectorSubcoreMesh(core_axis_name="core", subcore_axis_name="subcore")


def _sc_scatter_rows(x, idx_rows, n_out):
    n, c = x.shape
    kk = len(idx_rows)
    mesh = _sc_mesh()
    workers = mesh.num_cores * mesh.num_subcores
    per_worker = n // workers
    assert per_worker % SC_WINDOW == 0

    @functools.partial(
        pl.kernel, out_type=jax.ShapeDtypeStruct((n_out, c), x.dtype), mesh=mesh,
        scratch_types=[pltpu.VMEM((SC_WINDOW, c), x.dtype)]
        + [pltpu.VMEM((SC_WINDOW,), jnp.int32)] * kk + [pltpu.SemaphoreType.DMA],
        name="sc_dispatch_scatter")
    def scatter(x_hbm, *refs):
        idx_hbm, o_hbm, rows_v = refs[:kk], refs[kk], refs[kk + 1]
        idx_v, sem = refs[kk + 2:2 * kk + 2], refs[2 * kk + 2]
        wid = lax.axis_index("subcore") * mesh.num_cores + lax.axis_index("core")

        @pl.loop(0, per_worker // SC_WINDOW)
        def _(j):
            base = pl.multiple_of(wid * per_worker + j * SC_WINDOW, SC_WINDOW)
            pltpu.sync_copy(x_hbm.at[pl.ds(base, SC_WINDOW)], rows_v)
            for ih, iv in zip(idx_hbm, idx_v):
                pltpu.sync_copy(ih.at[pl.ds(base, SC_WINDOW)], iv)
            copies = [pltpu.async_copy(rows_v, o_hbm.at[iv], sem) for iv in idx_v]
            for cp in copies:
                cp.wait()

    return scatter(x, *idx_rows)


def _sc_gather_rows(src, idx):
    n_out = idx.shape[0]
    c = src.shape[1]
    mesh = _sc_mesh()
    workers = mesh.num_cores * mesh.num_subcores
    per_worker = n_out // workers
    assert per_worker % SC_WINDOW == 0

    @functools.partial(
        pl.kernel, out_type=jax.ShapeDtypeStruct((n_out, c), src.dtype), mesh=mesh,
        scratch_types=[pltpu.VMEM((SC_WINDOW, c), src.dtype), pltpu.VMEM((SC_WINDOW,), jnp.int32),
                       pltpu.SemaphoreType.DMA],
        name="sc_combine_gather")
    def gather(src_hbm, idx_hbm, o_hbm, rows_v, idx_v, sem):
        wid = lax.axis_index("subcore") * mesh.num_cores + lax.axis_index("core")

        @pl.loop(0, per_worker // SC_WINDOW)
        def _(j):
            base = pl.multiple_of(wid * per_worker + j * SC_WINDOW, SC_WINDOW)
            pltpu.sync_copy(idx_hbm.at[pl.ds(base, SC_WINDOW)], idx_v)
            pltpu.async_copy(src_hbm.at[idx_v], rows_v, sem).wait()
            pltpu.sync_copy(rows_v, o_hbm.at[pl.ds(base, SC_WINDOW)])

    return gather(src, idx)


def _expert_kernel(be_ref, nv_ref, x_ref, wgu_ref, bgu_ref, wd_ref, bd_ref, y_ref, wgu_s, wd_s,
                   *, d_e):
    i = pl.program_id(0)
    valid = nv_ref[i]

    @pl.when((i == 0) | (be_ref[i] != be_ref[jnp.maximum(i - 1, 0)]))
    def _():
        wgu_s[...] = wgu_ref[...].astype(BF16)
        wd_s[...] = wd_ref[...].astype(BF16)

    @pl.when(valid > 0)
    def _():
        rows = lax.broadcasted_iota(jnp.int32, x_ref.shape, 0)
        x = _unpack_bf16_pairs(jnp.where(rows < valid, x_ref[...], 0)).astype(BF16)
        gu = _dot(x, wgu_s[...]) + bgu_ref[...]
        gate = jnp.minimum(gu[:, :d_e], SWIGLU_LIMIT)
        up = jnp.clip(gu[:, d_e:], -SWIGLU_LIMIT, SWIGLU_LIMIT)
        act = (up + 1.0) * gate * _sigmoid(SWIGLU_ALPHA * gate)
        y_ref[...] = _pack_bf16_pairs(_dot(act.astype(BF16), wd_s[...]) + bd_ref[...])

    @pl.when(valid <= 0)
    def _():
        y_ref[...] = jnp.zeros_like(y_ref)


def _expert_call(xs, block_expert, block_valid, w_gu, b_gu, w_down, b_down):
    p, dh = xs.shape
    ne, d, d_e2 = w_gu.shape
    d_e = d_e2 // 2
    nblk = p // MOE_ROWS
    grid_spec = pltpu.PrefetchScalarGridSpec(
        num_scalar_prefetch=2,
        grid=(nblk,),
        in_specs=[pl.BlockSpec((MOE_ROWS, dh), lambda i, be, nv: (i, 0)),
                  pl.BlockSpec((None, d, d_e2), lambda i, be, nv: (be[i], 0, 0)),
                  pl.BlockSpec((None, 1, d_e2), lambda i, be, nv: (be[i], 0, 0)),
                  pl.BlockSpec((None, d_e, d), lambda i, be, nv: (be[i], 0, 0)),
                  pl.BlockSpec((None, 1, d), lambda i, be, nv: (be[i], 0, 0))],
        out_specs=pl.BlockSpec((MOE_ROWS, dh), lambda i, be, nv: (i, 0)),
        scratch_shapes=[pltpu.VMEM((d, d_e2), BF16), pltpu.VMEM((d_e, d), BF16)],
    )
    return pl.pallas_call(
        functools.partial(_expert_kernel, d_e=d_e),
        grid_spec=grid_spec,
        out_shape=jax.ShapeDtypeStruct((p, dh), jnp.int32),
        compiler_params=pltpu.CompilerParams(dimension_semantics=("arbitrary",),
                                             vmem_limit_bytes=V7X_VMEM_LIMIT),
        name="expert_ffn",
    )(block_expert, block_valid, xs, w_gu, b_gu.reshape(ne, 1, d_e2), w_down,
      b_down.reshape(ne, 1, d))


def _combine_kernel(x1_ref, yg_ref, tw_ref, g2_ref, nf_ref, o_ref):
    w = tw_ref[...]
    moe = w[:, 0:1] * _unpack_bf16_pairs(yg_ref[0])
    for k in range(1, TOP_K):
        moe = moe + w[:, k:k + 1] * _unpack_bf16_pairs(yg_ref[k])
    out = _rms(x1_ref[...] + _per_batch(moe, g2_ref[...]), nf_ref[...])
    o_ref[...] = _to_batch_major(out, g2_ref.shape[0])


def _combine_call(x1, yg, top_w, g2, nf):
    n, d = x1.shape
    nb = g2.shape[0]
    seq = n // nb
    const = lambda shape: pl.BlockSpec(shape, lambda i: (0,) * len(shape))
    row = lambda c: pl.BlockSpec((ROW_TILE, c), lambda i: (i, 0))
    return pl.pallas_call(
        _combine_kernel,
        grid=(n // ROW_TILE,),
        in_specs=[row(d), pl.BlockSpec((TOP_K, ROW_TILE, d // 2), lambda i: (0, i, 0)),
                  row(TOP_K), const((nb, d)), const((1, d))],
        out_specs=pl.BlockSpec((nb, GRID_W, d), lambda i: (0, i, 0)),
        out_shape=jax.ShapeDtypeStruct((nb, seq, d), F32),
        compiler_params=pltpu.CompilerParams(dimension_semantics=("parallel",),
                                             vmem_limit_bytes=V7X_VMEM_LIMIT),
        name="combine_final_norm",
    )(x1, yg, top_w, g2, nf)


def _tile_constants(nb):
    t = np.arange(GRID_W)[:, None]
    b = np.arange(nb)[None, :]
    deint = np.zeros((ROW_TILE, ROW_TILE), np.float32)
    deint[(b * GRID_W + t).ravel(), (t * nb + b).ravel()] = 1.0
    c = np.arange(F_GROUP_DIM)
    ang = 2.0 * np.pi * ((c[:, None] * c[None, :]) % F_GROUP_DIM) / F_GROUP_DIM
    cs = np.concatenate([np.cos(ang), np.sin(ang)], axis=1)
    tri = np.tril(np.ones((ROW_TILE, ROW_TILE), np.float32), -1)
    as_bf16 = lambda v: jnp.asarray(v, F32).astype(BF16)
    return as_bf16(deint), as_bf16(deint.T), as_bf16(cs), as_bf16(tri)


def kernel(x, c, ctx, c_ctx, w_ada, b_ada, norm1, w_in, conv_w, conv_b, gate_a_w, gate_a_b,
           gate_x_w, gate_x_b, lru_lambda, w_fourier, w_rnn, w_out, norm2, w_router, b_router,
           w_gu, b_gu, w_down, b_down, norm_f):
    nb, seq, d = x.shape
    depth = w_ada.shape[0]
    assert depth == 1 and nb == V7X_SUBLANES and seq % FFT_PERM_ROWS == 0
    n = nb * seq
    d_rnn = conv_w.shape[-1]
    d_f = w_fourier.shape[1]
    ne = w_router.shape[-1]
    lyr = 0
    deint, inter, cs, tri = _tile_constants(nb)

    ctx_t = jnp.transpose(ctx, (1, 0, 2)).reshape(-1, d)

    cond = jnp.zeros((2 * V7X_SUBLANES, d), F32).at[:nb].set(c).at[nb].set(c_ctx)
    mods = _ada_call(cond, w_ada[lyr], b_ada[lyr].reshape(1, -1))
    sh1, sc1, g1, sh2, sc2, g2 = [mods[:nb, k * d:(k + 1) * d] for k in range(6)]
    csh1, csc1 = mods[nb:nb + 1, 0:d], mods[nb:nb + 1, d:2 * d]

    w_in_b = w_in[lyr].astype(BF16)
    cw, cb = conv_w[lyr], conv_b[lyr].reshape(1, d_rnn)
    n1r = norm1[lyr].reshape(1, d)
    wg = jnp.concatenate([gate_a_w[lyr], gate_x_w[lyr]], axis=-1).astype(BF16)
    sp = jax.nn.softplus(-lru_lambda[lyr].astype(F32))
    scan = lambda u, dr, h0, name: _scan_call(
        u, wg[dr], gate_a_b[lyr, dr].reshape(1, -1), gate_x_b[lyr, dr].reshape(1, -1),
        sp[dr].reshape(1, -1), h0, bool(dr), name)

    u_ctx = _ctx_in_call(ctx_t, csc1, csh1, n1r, w_in_b[:, d_f:d_f + d_rnn], cw, cb, nb)
    h_zero = jnp.zeros((nb, d_rnn), F32)
    _, h_ctx_f = scan(u_ctx, 0, h_zero, "ctx_scan_fwd")
    _, h_ctx_b = scan(u_ctx, 1, h_zero, "ctx_scan_bwd")

    a_w, b_w, u, gg, gf, gr = _in_proj_call(x, sc1, sh1, n1r, w_in_b, cw, cb, deint, cs, d_f,
                                            d_rnn)
    y_wide = _fft_call(a_w, b_w)
    y_f, _ = scan(u, 0, h_ctx_f, "scan_fwd")
    y_b, _ = scan(u, 1, h_ctx_b, "scan_bwd")
    x1, hmod, top_i, top_w, rank, counts = _merge_call(
        x, y_f, y_b, gg, gf, gr, y_wide, inter, w_fourier[lyr].astype(BF16),
        w_rnn[lyr].astype(BF16), w_out[lyr].astype(BF16), g1, norm2[lyr].reshape(1, d), sc2, sh2,
        w_router[lyr], b_router[lyr].reshape(1, ne), tri, d_f)

    cnt = counts.reshape(ne).astype(jnp.int32)
    padded = (cnt + MOE_ROWS - 1) // MOE_ROWS * MOE_ROWS
    ends = jnp.cumsum(padded)
    starts = ends - padded
    expert_ids = jnp.arange(ne, dtype=jnp.int32)
    start_of = jnp.sum(jnp.where(top_i[..., None] == expert_ids, starts, 0), axis=-1)
    dest_km = (start_of + rank).T
    nblk = (n * TOP_K + ne * (MOE_ROWS - 1) + MOE_ROWS - 1) // MOE_ROWS
    block_start = jnp.arange(nblk, dtype=jnp.int32) * MOE_ROWS
    block_expert = jnp.minimum(jnp.sum(block_start[:, None] >= ends[None, :], axis=1),
                               ne - 1).astype(jnp.int32)
    block_valid = jnp.clip((starts + cnt)[block_expert] - block_start, 0, MOE_ROWS)
    block_valid = jnp.where(block_start < ends[-1], block_valid, 0).astype(jnp.int32)

    xs = _sc_scatter_rows(hmod, [dest_km[k] for k in range(TOP_K)], nblk * MOE_ROWS)
    ys = _expert_call(xs, block_expert, block_valid, w_gu[lyr], b_gu[lyr], w_down[lyr], b_down[lyr])
    yg = _sc_gather_rows(ys, dest_km.reshape(TOP_K * n)).reshape(TOP_K, n, d // 2)
    return _combine_call(x1, yg, top_w, g2, norm_f.reshape(1, d))
```

```python
import functools
import math

import numpy as np
import jax
import jax.numpy as jnp
from jax import lax
from jax.experimental import pallas as pl
from jax.experimental.pallas import tpu as pltpu
from jax.experimental.pallas import tpu_sc as plsc

F32 = jnp.float32
BF16 = jnp.bfloat16

GRID_W = 64
CONV_W = 4
LRU_C = 8.0
TOP_K = 4
F_GROUP_DIM = 128
RNN_BLOCK = 128
SWIGLU_ALPHA = 1.702
SWIGLU_LIMIT = 7.0
EPS = 1e-6

V7X_LANES = 128
V7X_SUBLANES = 8
V7X_BF16_ROWS = 16
V7X_VMEM_LIMIT = 56 * 1024 * 1024

ROW_TILE = GRID_W * V7X_SUBLANES
FFT_PERM_ROWS = V7X_BF16_ROWS * GRID_W
FFT_COLS = 256
SCAN_ROWS = 2048
MOE_ROWS = 512
SC_WINDOW = 128
HIGH_HALF = -65536
LOW_HALF = 65535


def _dot(a, b):
    return jnp.dot(a, b, preferred_element_type=F32)


def _dot_nt(a, b):
    return lax.dot_general(a, b, (((1,), (1,)), ((), ())), preferred_element_type=F32)


def _split_bf16(a):
    hi = a.astype(BF16)
    return hi, (a - hi.astype(F32)).astype(BF16)


def _dot3(a, b):
    ah, al = _split_bf16(a)
    bh, bl = _split_bf16(b)
    return _dot(ah, bh) + _dot(ah, bl) + _dot(al, bh)


def _sigmoid(x):
    return 0.5 * (jnp.tanh(0.5 * x) + 1.0)


def _gelu_tanh(x):
    c = math.sqrt(2.0 / math.pi)
    return 0.5 * x * (1.0 + jnp.tanh(c * (x + 0.044715 * (x * x * x))))


def _pack_bf16_pairs(v):
    c = v.shape[1] // 2
    bits = pltpu.bitcast(v.astype(BF16).astype(F32), jnp.int32)
    return (bits[:, :c] & HIGH_HALF) | ((bits[:, c:] >> 16) & LOW_HALF)


def _unpack_bf16_pairs(p):
    hi = pltpu.bitcast(p & HIGH_HALF, F32)
    lo = pltpu.bitcast(p << 16, F32)
    return jnp.concatenate([hi, lo], axis=1)


def _rms(x, g):
    return x * lax.rsqrt(jnp.mean(x * x, axis=-1, keepdims=True) + EPS) * g


def _modulate(xn, sc, sh):
    if sc.shape[0] == 1:
        return xn * (1.0 + sc) + sh
    r, d = xn.shape
    nb = sc.shape[0]
    x3 = xn.reshape(r // nb, nb, d)
    return (x3 * (1.0 + sc)[None] + sh[None]).reshape(r, d)


def _to_time_major(x3):
    nb, t, d = x3.shape
    return pltpu.einshape("btd->tbd", x3).reshape(t * nb, d)


def _to_batch_major(v, nb):
    r, d = v.shape
    return pltpu.einshape("tbd->btd", v.reshape(r // nb, nb, d))


def _per_batch(v, g):
    r, d = v.shape
    nb = g.shape[0]
    return (v.reshape(r // nb, nb, d) * g[None]).reshape(r, d)


def _conv_time_major(z, cw, cb, nb):
    r, c = z.shape

    def shifted(s):
        k = abs(s) * nb
        zero = jnp.zeros((k, c), z.dtype)
        if s > 0:
            return jnp.concatenate([zero, z[:r - k]], axis=0)
        return jnp.concatenate([z[k:], zero], axis=0)

    return (cb + shifted(2) * cw[0:1] + shifted(1) * cw[1:2] + z * cw[2:3]
            + shifted(-1) * cw[3:4])


def _ada_kernel(c_ref, w_ref, b_ref, o_ref):
    cnd = c_ref[...]
    o_ref[...] = _dot3(cnd * _sigmoid(cnd), w_ref[...]) + b_ref[...]


def _ada_call(cond, w, b):
    r, d = cond.shape
    n = w.shape[1]
    return pl.pallas_call(
        _ada_kernel,
        grid=(n // d,),
        in_specs=[pl.BlockSpec((r, d), lambda j: (0, 0)),
                  pl.BlockSpec((d, d), lambda j: (0, j)),
                  pl.BlockSpec((1, d), lambda j: (0, j))],
        out_specs=pl.BlockSpec((r, d), lambda j: (0, j)),
        out_shape=jax.ShapeDtypeStruct((r, n), F32),
        name="ada_modulation",
    )(cond, w, b)


def _ctx_in_kernel(x_ref, sc_ref, sh_ref, g_ref, w_ref, cw_ref, cb_ref, u_ref, *, nb):
    h = _modulate(_rms(x_ref[...], g_ref[...]), sc_ref[...], sh_ref[...])
    z = _dot(h.astype(BF16), w_ref[...])
    u_ref[...] = _conv_time_major(z, cw_ref[...], cb_ref[...], nb).astype(u_ref.dtype)


def _ctx_in_call(ctx_t, sc, sh, g, w_r, cw, cb, nb):
    r, d = ctx_t.shape
    c = w_r.shape[1]
    full = lambda shape: pl.BlockSpec(shape, lambda i: (0,) * len(shape))
    return pl.pallas_call(
        functools.partial(_ctx_in_kernel, nb=nb),
        grid=(1,),
        in_specs=[full((r, d)), full((1, d)), full((1, d)), full((1, d)), full((d, c)),
                  full((CONV_W, c)), full((1, c))],
        out_specs=full((r, c)),
        out_shape=jax.ShapeDtypeStruct((r, c), BF16),
        compiler_params=pltpu.CompilerParams(vmem_limit_bytes=V7X_VMEM_LIMIT),
        name="ctx_in_proj",
    )(ctx_t, sc, sh, g, w_r, cw, cb)


def _in_proj_kernel(x_ref, sc_ref, sh_ref, g_ref, w_ref, cw_ref, cb_ref, pd_ref, cs_ref,
                    a_ref, b_ref, u_ref, gg_ref, gf_ref, gr_ref, *, nb, d_f, d_rnn):
    d = x_ref.shape[-1]
    x = _to_time_major(x_ref[...])
    h = _modulate(_rms(x, g_ref[...]), sc_ref[...], sh_ref[...]).astype(BF16)
    c_r, c_g, c_m = d_f, d_f + d_rnn, d_f + 2 * d_rnn

    zf = _dot(h, w_ref[:, 0:c_r]).astype(BF16)
    zf = _dot(pd_ref[...], zf).astype(BF16)
    for grp in range(d_f // F_GROUP_DIM):
        ab = _dot(zf[:, grp * F_GROUP_DIM:(grp + 1) * F_GROUP_DIM], cs_ref[...])
        for bi in range(nb):
            rows = slice(bi * GRID_W, (bi + 1) * GRID_W)
            col = bi * d_f + grp * F_GROUP_DIM
            a_ref[:, col:col + F_GROUP_DIM] = ab[rows, :F_GROUP_DIM].astype(BF16)
            b_ref[:, col:col + F_GROUP_DIM] = ab[rows, F_GROUP_DIM:].astype(BF16)

    zr = _dot(h, w_ref[:, c_r:c_g])
    u_ref[...] = _conv_time_major(zr, cw_ref[...], cb_ref[...], nb).astype(BF16)
    gg_ref[...] = _gelu_tanh(_dot(h, w_ref[:, c_g:c_m])).astype(BF16)
    gf_ref[...] = _sigmoid(_dot(h, w_ref[:, c_m:c_m + d])).astype(BF16)
    gr_ref[...] = _sigmoid(_dot(h, w_ref[:, c_m + d:c_m + 2 * d])).astype(BF16)


def _in_proj_call(x, sc, sh, g, w_in, cw, cb, pd, cs, d_f, d_rnn):
    nb, seq, d = x.shape
    n = nb * seq
    d_in = w_in.shape[1]
    nt = n // ROW_TILE
    const = lambda shape: pl.BlockSpec(shape, lambda i: (0,) * len(shape))
    row = lambda c: pl.BlockSpec((ROW_TILE, c), lambda i: (i, 0))
    wide = pl.BlockSpec((GRID_W, nb * d_f), lambda i: (i, 0))
    return pl.pallas_call(
        functools.partial(_in_proj_kernel, nb=nb, d_f=d_f, d_rnn=d_rnn),
        grid=(nt,),
        in_specs=[pl.BlockSpec((nb, GRID_W, d), lambda i: (0, i, 0)), const((nb, d)),
                  const((nb, d)), const((1, d)), const((d, d_in)),
                  const((CONV_W, d_rnn)), const((1, d_rnn)), const((ROW_TILE, ROW_TILE)),
                  const((F_GROUP_DIM, 2 * F_GROUP_DIM))],
        out_specs=[wide, wide, row(d_rnn), row(d_rnn), row(d), row(d)],
        out_shape=[jax.ShapeDtypeStruct((seq, nb * d_f), BF16),
                   jax.ShapeDtypeStruct((seq, nb * d_f), BF16),
                   jax.ShapeDtypeStruct((n, d_rnn), BF16),
                   jax.ShapeDtypeStruct((n, d_rnn), BF16),
                   jax.ShapeDtypeStruct((n, d), BF16),
                   jax.ShapeDtypeStruct((n, d), BF16)],
        compiler_params=pltpu.CompilerParams(dimension_semantics=("parallel",),
                                             vmem_limit_bytes=V7X_VMEM_LIMIT),
        name="latent_in_proj",
    )(x, sc, sh, g, w_in, cw, cb, pd, cs)


def _fft_kernel(a_ref, b_ref, p_ref, d1_ref, m3c_ref, m3s_ref, o_ref, sa, sb, *, n1, scale):
    tn = a_ref.shape[1]
    blk = V7X_BF16_ROWS
    for j in range(n1 // blk):
        rows = slice(j * FFT_PERM_ROWS, (j + 1) * FFT_PERM_ROWS)
        sa[:, j * blk:(j + 1) * blk, :] = (
            _dot(p_ref[...], a_ref[rows, :]).astype(BF16).reshape(GRID_W, blk, tn))
        sb[:, j * blk:(j + 1) * blk, :] = (
            _dot(p_ref[...], b_ref[rows, :]).astype(BF16).reshape(GRID_W, blk, tn))

    def stage1(l2, carry):
        dm = d1_ref[l2]
        ca = _dot(dm, sa[l2])
        cb = _dot(dm, sb[l2])
        sa[l2] = (ca[:n1] - cb[n1:]).astype(BF16)
        sb[l2] = (-(cb[:n1] + ca[n1:])).astype(BF16)
        return carry

    lax.fori_loop(0, GRID_W, stage1, 0)

    for i in range(n1 // blk):
        xr = sa[:, i * blk:(i + 1) * blk, :].reshape(FFT_PERM_ROWS, tn)
        xi = sb[:, i * blk:(i + 1) * blk, :].reshape(FFT_PERM_ROWS, tn)
        y = (_dot(m3c_ref[...], xr) + _dot(m3s_ref[...], xi)) * scale
        o_ref[:, i * blk:(i + 1) * blk, :] = y.astype(BF16).reshape(GRID_W, blk, tn)


def _fft_constants(seq):
    n1 = seq // GRID_W
    blk = V7X_BF16_ROWS
    perm = np.zeros((FFT_PERM_ROWS, FFT_PERM_ROWS), np.float32)
    l1s, l2 = np.meshgrid(np.arange(blk), np.arange(GRID_W), indexing="ij")
    perm[(l2 * blk + l1s).ravel(), (l1s * GRID_W + l2).ravel()] = 1.0
    k1 = np.arange(n1)[None, :, None]
    l1 = np.arange(n1)[None, None, :]
    l2v = np.arange(GRID_W)[:, None, None]
    ang = 2.0 * np.pi * ((k1 * (GRID_W * l1 + l2v)) % seq) / seq
    d1 = np.concatenate([np.cos(ang), np.sin(ang)], axis=1)
    k2 = np.arange(GRID_W)[:, None]
    l2m = np.arange(GRID_W)[None, :]
    ang3 = 2.0 * np.pi * ((k2 * l2m) % GRID_W) / GRID_W
    eye = np.eye(blk)
    m3c = np.kron(np.cos(ang3), eye)
    m3s = np.kron(np.sin(ang3), eye)
    as_bf16 = lambda v: jnp.asarray(v, F32).astype(BF16)
    return as_bf16(perm), as_bf16(d1), as_bf16(m3c), as_bf16(m3s)


def _fft_call(a_w, b_w):
    seq, nw = a_w.shape
    n1 = seq // GRID_W
    tn = min(FFT_COLS, nw)
    perm, d1, m3c, m3s = _fft_constants(seq)
    scale = 1.0 / math.sqrt(seq * F_GROUP_DIM)
    const = lambda shape: pl.BlockSpec(shape, lambda j: (0,) * len(shape))
    strip = pl.BlockSpec((seq, tn), lambda j: (0, j))
    out = pl.pallas_call(
        functools.partial(_fft_kernel, n1=n1, scale=scale),
        grid=(nw // tn,),
        in_specs=[strip, strip, const(perm.shape), const(d1.shape), const(m3c.shape),
                  const(m3s.shape)],
        out_specs=pl.BlockSpec((GRID_W, n1, tn), lambda j: (0, 0, j)),
        out_shape=jax.ShapeDtypeStruct((GRID_W, n1, nw), BF16),
        scratch_shapes=[pltpu.VMEM((GRID_W, n1, tn), BF16), pltpu.VMEM((GRID_W, n1, tn), BF16)],
        compiler_params=pltpu.CompilerParams(dimension_semantics=("parallel",),
                                             vmem_limit_bytes=V7X_VMEM_LIMIT),
        name="position_dft",
    )(a_w, b_w, perm, d1, m3c, m3s)
    return out.reshape(seq, nw)


def _scan_kernel(u_ref, wg_ref, ba_ref, bx_ref, sp_ref, h0_ref, y_ref, hf_ref, a_s, b_s, h_s,
                 *, reverse, steps, heads, nb):
    @pl.when(pl.program_id(0) == 0)
    def _():
        h_s[...] = h0_ref[...]

    u = u_ref[...]
    for hd in range(heads):
        sl = slice(hd * RNN_BLOCK, (hd + 1) * RNN_BLOCK)
        ub = u[:, sl]
        g = _dot(ub, wg_ref[hd])
        tr = jnp.tanh(g[:, :RNN_BLOCK] + ba_ref[:, sl])
        ti = jnp.tanh(g[:, RNN_BLOCK:] + bx_ref[:, sl])
        log_a = sp_ref[:, sl] * (1.0 + tr)
        a = jnp.exp(log_a)
        q = (a * a + 1.0) * (-jnp.tanh(log_a))
        mult = jnp.where(q > 0.0, q * lax.rsqrt(q), 0.0)
        a_s[:, sl] = a
        b_s[:, sl] = mult * ((1.0 + ti) * (0.5 * ub.astype(F32)))

    def step(j, h):
        t = (steps - 1 - j) if reverse else j
        r0 = pl.multiple_of(t * nb, nb)
        h = a_s[pl.ds(r0, nb), :] * h + b_s[pl.ds(r0, nb), :]
        b_s[pl.ds(r0, nb), :] = h
        return h

    h = lax.fori_loop(0, steps, step, h_s[...], unroll=8)
    h_s[...] = h
    hf_ref[...] = h
    y_ref[...] = b_s[...].astype(y_ref.dtype)


def _scan_call(u, wg, ba, bx, sp, h0, reverse, name):
    n, c = u.shape
    nb = h0.shape[0]
    heads = wg.shape[0]
    tr = min(SCAN_ROWS, n)
    nt = n // tr
    const = lambda shape: pl.BlockSpec(shape, lambda i: (0,) * len(shape))
    tile = pl.BlockSpec((tr, c), (lambda i: (nt - 1 - i, 0)) if reverse else (lambda i: (i, 0)))
    return pl.pallas_call(
        functools.partial(_scan_kernel, reverse=reverse, steps=tr // nb, heads=heads, nb=nb),
        grid=(nt,),
        in_specs=[tile, const(wg.shape), const((1, c)), const((1, c)), const((1, c)),
                  const((nb, c))],
        out_specs=[tile, const((nb, c))],
        out_shape=[jax.ShapeDtypeStruct((n, c), BF16), jax.ShapeDtypeStruct((nb, c), F32)],
        scratch_shapes=[pltpu.VMEM((tr, c), F32), pltpu.VMEM((tr, c), F32),
                        pltpu.VMEM((nb, c), F32)],
        compiler_params=pltpu.CompilerParams(dimension_semantics=("arbitrary",),
                                             vmem_limit_bytes=V7X_VMEM_LIMIT),
        name=name,
    )(u, wg, ba, bx, sp, h0)


def _merge_kernel(x_ref, yf_ref, yb_ref, gg_ref, gf_ref, gr_ref, yw_ref, pi_ref, wf_ref,
                  wr_ref, wo_ref, g1_ref, n2_ref, sc2_ref, sh2_ref, wrt_ref, brt_ref, tri_ref,
                  x1_ref, hm_ref, ti_ref, tw_ref, rk_ref, cnt_ref, carry, *, nb, d_f):
    @pl.when(pl.program_id(0) == 0)
    def _():
        carry[...] = jnp.zeros_like(carry)

    yw = jnp.concatenate([yw_ref[:, bi * d_f:(bi + 1) * d_f] for bi in range(nb)], axis=0)
    yfm = _dot(pi_ref[...], yw).astype(BF16)
    f_out = _dot(yfm, wf_ref[...])
    y = yf_ref[...].astype(F32) + yb_ref[...].astype(F32)
    r_out = _dot((y * gg_ref[...].astype(F32)).astype(BF16), wr_ref[...])
    mix = gf_ref[...].astype(F32) * f_out + gr_ref[...].astype(F32) * r_out
    x1 = _to_time_major(x_ref[...]) + _per_batch(_dot(mix.astype(BF16), wo_ref[...]), g1_ref[...])
    x1_ref[...] = x1

    h2 = _modulate(_rms(x1, n2_ref[...]), sc2_ref[...], sh2_ref[...])
    hm_ref[...] = _pack_bf16_pairs(h2)

    hh, hl = _split_bf16(h2)
    wh, wl = _split_bf16(wrt_ref[...])
    lg = _dot_nt(wh, hh) + _dot_nt(wl, hh) + _dot_nt(wh, hl) + brt_ref[...]
    ne, tm = lg.shape
    eid = lax.broadcasted_iota(jnp.int32, (ne, tm), 0)
    row_k = lax.broadcasted_iota(jnp.int32, (TOP_K, tm), 0)
    vals, hots = [], []
    top_i = jnp.zeros((TOP_K, tm), jnp.int32)
    for k in range(TOP_K):
        m = jnp.max(lg, axis=0, keepdims=True)
        idx = jnp.min(jnp.where(lg == m, eid, ne), axis=0, keepdims=True)
        hit = eid == idx
        vals.append(m)
        hots.append(hit.astype(F32))
        top_i = jnp.where(row_k == k, idx, top_i)
        lg = jnp.where(hit, -jnp.inf, lg)
    exps = [jnp.exp(v - vals[0]) for v in vals]
    inv = 1.0 / (exps[0] + exps[1] + exps[2] + exps[3])
    ti_ref[...] = top_i
    w_pad = jnp.concatenate([e * inv for e in exps]
                            + [jnp.zeros((V7X_LANES - TOP_K, tm), F32)], axis=0)
    tw_ref[...] = jnp.transpose(w_pad)[:, :TOP_K]

    per_tok = hots[0] + hots[1] + hots[2] + hots[3]
    before = _dot(per_tok.astype(BF16), tri_ref[...]) + carry[...]
    rank = jnp.zeros((TOP_K, tm), F32)
    for k in range(TOP_K):
        rank = jnp.where(row_k == k, jnp.sum(hots[k] * before, axis=0, keepdims=True), rank)
    rk_ref[...] = rank.astype(jnp.int32)
    carry[...] = carry[...] + jnp.sum(per_tok, axis=1, keepdims=True)
    cnt_ref[...] = carry[...]


def _merge_call(x, y_f, y_b, gg, gf, gr, yw, pint, w_f, w_r, w_o, g1, n2, sc2, sh2, w_rt, b_rt,
                tri, d_f):
    nb, seq, d = x.shape
    n = nb * seq
    d_rnn = y_f.shape[1]
    ne = w_rt.shape[0]
    nt = n // ROW_TILE
    const = lambda shape: pl.BlockSpec(shape, lambda i: (0,) * len(shape))
    row = lambda c: pl.BlockSpec((ROW_TILE, c), lambda i: (i, 0))
    k_major = pl.BlockSpec((TOP_K, ROW_TILE), lambda i: (0, i))
    return pl.pallas_call(
        functools.partial(_merge_kernel, nb=nb, d_f=d_f),
        grid=(nt,),
        in_specs=[pl.BlockSpec((nb, GRID_W, d), lambda i: (0, i, 0)), row(d_rnn), row(d_rnn),
                  row(d_rnn), row(d), row(d), pl.BlockSpec((GRID_W, nb * d_f), lambda i: (i, 0)),
                  const((ROW_TILE, ROW_TILE)), const((d_f, d)), const((d_rnn, d)), const((d, d)),
                  const((nb, d)), const((1, d)), const((nb, d)), const((nb, d)),
                  const((ne, d)), const((ne, 1)), const((ROW_TILE, ROW_TILE))],
        out_specs=[row(d), row(d // 2), k_major, row(TOP_K), k_major, const((ne, 1))],
        out_shape=[jax.ShapeDtypeStruct((n, d), F32), jax.ShapeDtypeStruct((n, d // 2), jnp.int32),
                   jax.ShapeDtypeStruct((TOP_K, n), jnp.int32),
                   jax.ShapeDtypeStruct((n, TOP_K), F32),
                   jax.ShapeDtypeStruct((TOP_K, n), jnp.int32),
                   jax.ShapeDtypeStruct((ne, 1), F32)],
        scratch_shapes=[pltpu.VMEM((ne, 1), F32)],
        compiler_params=pltpu.CompilerParams(dimension_semantics=("arbitrary",),
                                             vmem_limit_bytes=V7X_VMEM_LIMIT),
        name="merge_and_route",
    )(x, y_f, y_b, gg, gf, gr, yw, pint, w_f, w_r, w_o, g1, n2, sc2, sh2, w_rt, b_rt, tri)


def _sc_mesh():
    return plsc.VectorSubcoreMesh(core_axis_name="core", subcore_axis_name="subcore")


def _sc_scatter_rows(x, idx_rows, n_out):
    n, c = x.shape
    kk = len(idx_rows)
    mesh = _sc_mesh()
    workers = mesh.num_cores * mesh.num_subcores
    per_worker = n // workers
    assert per_worker % SC_WINDOW == 0

    @functools.partial(
        pl.kernel, out_type=jax.ShapeDtypeStruct((n_out, c), x.dtype), mesh=mesh,
        scratch_types=[pltpu.VMEM((SC_WINDOW, c), x.dtype)]
        + [pltpu.VMEM((SC_WINDOW,), jnp.int32)] * kk + [pltpu.SemaphoreType.DMA],
        name="sc_dispatch_scatter")
    def scatter(x_hbm, *refs):
        idx_hbm, o_hbm, rows_v = refs[:kk], refs[kk], refs[kk + 1]
        idx_v, sem = refs[kk + 2:2 * kk + 2], refs[2 * kk + 2]
        wid = lax.axis_index("subcore") * mesh.num_cores + lax.axis_index("core")

        @pl.loop(0, per_worker // SC_WINDOW)
        def _(j):
            base = pl.multiple_of(wid * per_worker + j * SC_WINDOW, SC_WINDOW)
            pltpu.sync_copy(x_hbm.at[pl.ds(base, SC_WINDOW)], rows_v)
            for ih, iv in zip(idx_hbm, idx_v):
                pltpu.sync_copy(ih.at[pl.ds(base, SC_WINDOW)], iv)
            copies = [pltpu.async_copy(rows_v, o_hbm.at[iv], sem) for iv in idx_v]
            for cp in copies:
                cp.wait()

    return scatter(x, *idx_rows)


def _sc_gather_rows(src, idx):
    n_out = idx.shape[0]
    c = src.shape[1]
    mesh = _sc_mesh()
    workers = mesh.num_cores * mesh.num_subcores
    per_worker = n_out // workers
    assert per_worker % SC_WINDOW == 0

    @functools.partial(
        pl.kernel, out_type=jax.ShapeDtypeStruct((n_out, c), src.dtype), mesh=mesh,
        scratch_types=[pltpu.VMEM((SC_WINDOW, c), src.dtype), pltpu.VMEM((SC_WINDOW,), jnp.int32),
                       pltpu.SemaphoreType.DMA],
        name="sc_combine_gather")
    def gather(src_hbm, idx_hbm, o_hbm, rows_v, idx_v, sem):
        wid = lax.axis_index("subcore") * mesh.num_cores + lax.axis_index("core")

        @pl.loop(0, per_worker // SC_WINDOW)
        def _(j):
            base = pl.multiple_of(wid * per_worker + j * SC_WINDOW, SC_WINDOW)
            pltpu.sync_copy(idx_hbm.at[pl.ds(base, SC_WINDOW)], idx_v)
            pltpu.async_copy(src_hbm.at[idx_v], rows_v, sem).wait()
            pltpu.sync_copy(rows_v, o_hbm.at[pl.ds(base, SC_WINDOW)])

    return gather(src, idx)


def _expert_kernel(be_ref, nv_ref, x_ref, wgu_ref, bgu_ref, wd_ref, bd_ref, y_ref, wgu_s, wd_s,
                   *, d_e):
    i = pl.program_id(0)
    valid = nv_ref[i]

    @pl.when((i == 0) | (be_ref[i] != be_ref[jnp.maximum(i - 1, 0)]))
    def _():
        wgu_s[...] = wgu_ref[...].astype(BF16)
        wd_s[...] = wd_ref[...].astype(BF16)

    @pl.when(valid > 0)
    def _():
        rows = lax.broadcasted_iota(jnp.int32, x_ref.shape, 0)
        x = _unpack_bf16_pairs(jnp.where(rows < valid, x_ref[...], 0)).astype(BF16)
        gu = _dot(x, wgu_s[...]) + bgu_ref[...]
        gate = jnp.minimum(gu[:, :d_e], SWIGLU_LIMIT)
        up = jnp.clip(gu[:, d_e:], -SWIGLU_LIMIT, SWIGLU_LIMIT)
        act = (up + 1.0) * gate * _sigmoid(SWIGLU_ALPHA * gate)
        y_ref[...] = _pack_bf16_pairs(_dot(act.astype(BF16), wd_s[...]) + bd_ref[...])

    @pl.when(valid <= 0)
    def _():
        y_ref[...] = jnp.zeros_like(y_ref)


def _expert_call(xs, block_expert, block_valid, w_gu, b_gu, w_down, b_down):
    p, dh = xs.shape
    ne, d, d_e2 = w_gu.shape
    d_e = d_e2 // 2
    nblk = p // MOE_ROWS
    grid_spec = pltpu.PrefetchScalarGridSpec(
        num_scalar_prefetch=2,
        grid=(nblk,),
        in_specs=[pl.BlockSpec((MOE_ROWS, dh), lambda i, be, nv: (i, 0)),
                  pl.BlockSpec((None, d, d_e2), lambda i, be, nv: (be[i], 0, 0)),
                  pl.BlockSpec((None, 1, d_e2), lambda i, be, nv: (be[i], 0, 0)),
                  pl.BlockSpec((None, d_e, d), lambda i, be, nv: (be[i], 0, 0)),
                  pl.BlockSpec((None, 1, d), lambda i, be, nv: (be[i], 0, 0))],
        out_specs=pl.BlockSpec((MOE_ROWS, dh), lambda i, be, nv: (i, 0)),
        scratch_shapes=[pltpu.VMEM((d, d_e2), BF16), pltpu.VMEM((d_e, d), BF16)],
    )
    return pl.pallas_call(
        functools.partial(_expert_kernel, d_e=d_e),
        grid_spec=grid_spec,
        out_shape=jax.ShapeDtypeStruct((p, dh), jnp.int32),
        compiler_params=pltpu.CompilerParams(dimension_semantics=("arbitrary",),
                                             vmem_limit_bytes=V7X_VMEM_LIMIT),
        name="expert_ffn",
    )(block_expert, block_valid, xs, w_gu, b_gu.reshape(ne, 1, d_e2), w_down,
      b_down.reshape(ne, 1, d))


def _combine_kernel(x1_ref, yg_ref, tw_ref, g2_ref, nf_ref, o_ref):
    w = tw_ref[...]
    moe = w[:, 0:1] * _unpack_bf16_pairs(yg_ref[0])
    for k in range(1, TOP_K):
        moe = moe + w[:, k:k + 1] * _unpack_bf16_pairs(yg_ref[k])
    out = _rms(x1_ref[...] + _per_batch(moe, g2_ref[...]), nf_ref[...])
    o_ref[...] = _to_batch_major(out, g2_ref.shape[0])


def _combine_call(x1, yg, top_w, g2, nf):
    n, d = x1.shape
    nb = g2.shape[0]
    seq = n // nb
    const = lambda shape: pl.BlockSpec(shape, lambda i: (0,) * len(shape))
    row = lambda c: pl.BlockSpec((ROW_TILE, c), lambda i: (i, 0))
    return pl.pallas_call(
        _combine_kernel,
        grid=(n // ROW_TILE,),
        in_specs=[row(d), pl.BlockSpec((TOP_K, ROW_TILE, d // 2), lambda i: (0, i, 0)),
                  row(TOP_K), const((nb, d)), const((1, d))],
        out_specs=pl.BlockSpec((nb, GRID_W, d), lambda i: (0, i, 0)),
        out_shape=jax.ShapeDtypeStruct((nb, seq, d), F32),
        compiler_params=pltpu.CompilerParams(dimension_semantics=("parallel",),
                                             vmem_limit_bytes=V7X_VMEM_LIMIT),
        name="combine_final_norm",
    )(x1, yg, top_w, g2, nf)


def _tile_constants(nb):
    t = np.arange(GRID_W)[:, None]
    b = np.arange(nb)[None, :]
    deint = np.zeros((ROW_TILE, ROW_TILE), np.float32)
    deint[(b * GRID_W + t).ravel(), (t * nb + b).ravel()] = 1.0
    c = np.arange(F_GROUP_DIM)
    ang = 2.0 * np.pi * ((c[:, None] * c[None, :]) % F_GROUP_DIM) / F_GROUP_DIM
    cs = np.concatenate([np.cos(ang), np.sin(ang)], axis=1)
    tri = np.triu(np.ones((ROW_TILE, ROW_TILE), np.float32), 1)
    as_bf16 = lambda v: jnp.asarray(v, F32).astype(BF16)
    return as_bf16(deint), as_bf16(deint.T), as_bf16(cs), as_bf16(tri)


def kernel(x, c, ctx, c_ctx, w_ada, b_ada, norm1, w_in, conv_w, conv_b, gate_a_w, gate_a_b,
           gate_x_w, gate_x_b, lru_lambda, w_fourier, w_rnn, w_out, norm2, w_router, b_router,
           w_gu, b_gu, w_down, b_down, norm_f):
    nb, seq, d = x.shape
    depth = w_ada.shape[0]
    assert depth == 1 and nb == V7X_SUBLANES and seq % FFT_PERM_ROWS == 0
    n = nb * seq
    d_rnn = conv_w.shape[-1]
    d_f = w_fourier.shape[1]
    ne = w_router.shape[-1]
    lyr = 0
    deint, inter, cs, tri = _tile_constants(nb)

    ctx_t = jnp.transpose(ctx, (1, 0, 2)).reshape(-1, d)

    cond = jnp.zeros((2 * V7X_SUBLANES, d), F32).at[:nb].set(c).at[nb].set(c_ctx)
    mods = _ada_call(cond, w_ada[lyr], b_ada[lyr].reshape(1, -1))
    sh1, sc1, g1, sh2, sc2, g2 = [mods[:nb, k * d:(k + 1) * d] for k in range(6)]
    csh1, csc1 = mods[nb:nb + 1, 0:d], mods[nb:nb + 1, d:2 * d]

    w_in_b = w_in[lyr].astype(BF16)
    cw, cb = conv_w[lyr], conv_b[lyr].reshape(1, d_rnn)
    n1r = norm1[lyr].reshape(1, d)
    wg = (0.5 * jnp.concatenate([gate_a_w[lyr], gate_x_w[lyr]], axis=-1)).astype(BF16)
    sp = (-0.5 * LRU_C) * jax.nn.softplus(-lru_lambda[lyr].astype(F32))
    scan = lambda u, dr, h0, name: _scan_call(
        u, wg[dr], 0.5 * gate_a_b[lyr, dr].reshape(1, -1), 0.5 * gate_x_b[lyr, dr].reshape(1, -1),
        sp[dr].reshape(1, -1), h0, bool(dr), name)

    u_ctx = _ctx_in_call(ctx_t, csc1, csh1, n1r, w_in_b[:, d_f:d_f + d_rnn], cw, cb, nb)
    h_zero = jnp.zeros((nb, d_rnn), F32)
    _, h_ctx_f = scan(u_ctx, 0, h_zero, "ctx_scan_fwd")
    _, h_ctx_b = scan(u_ctx, 1, h_zero, "ctx_scan_bwd")

    a_w, b_w, u, gg, gf, gr = _in_proj_call(x, sc1, sh1, n1r, w_in_b, cw, cb, deint, cs, d_f,
                                            d_rnn)
    y_wide = _fft_call(a_w, b_w)
    y_f, _ = scan(u, 0, h_ctx_f, "scan_fwd")
    y_b, _ = scan(u, 1, h_ctx_b, "scan_bwd")
    x1, hmod, top_i, top_w, rank, counts = _merge_call(
        x, y_f, y_b, gg, gf, gr, y_wide, inter, w_fourier[lyr].astype(BF16),
        w_rnn[lyr].astype(BF16), w_out[lyr].astype(BF16), g1, norm2[lyr].reshape(1, d), sc2, sh2,
        w_router[lyr].T, b_router[lyr].reshape(ne, 1), tri, d_f)

    cnt = counts.reshape(ne).astype(jnp.int32)
    padded = (cnt + MOE_ROWS - 1) // MOE_ROWS * MOE_ROWS
    ends = jnp.cumsum(padded)
    starts = ends - padded
    expert_ids = jnp.arange(ne, dtype=jnp.int32)[:, None, None]
    start_of = jnp.sum(jnp.where(top_i[None] == expert_ids, starts[:, None, None], 0), axis=0)
    dest_km = start_of + rank
    nblk = (n * TOP_K + ne * (MOE_ROWS - 1) + MOE_ROWS - 1) // MOE_ROWS
    block_start = jnp.arange(nblk, dtype=jnp.int32) * MOE_ROWS
    block_expert = jnp.minimum(jnp.sum(block_start[:, None] >= ends[None, :], axis=1),
                               ne - 1).astype(jnp.int32)
    block_valid = jnp.clip((starts + cnt)[block_expert] - block_start, 0, MOE_ROWS)
    block_valid = jnp.where(block_start < ends[-1], block_valid, 0).astype(jnp.int32)

    xs = _sc_scatter_rows(hmod, [dest_km[k] for k in range(TOP_K)], nblk * MOE_ROWS)
    ys = _expert_call(xs, block_expert, block_valid, w_gu[lyr], b_gu[lyr], w_down[lyr], b_down[lyr])
    yg = _sc_gather_rows(ys, dest_km.reshape(TOP_K * n)).reshape(TOP_K, n, d // 2)
    return _combine_call(x1, yg, top_w, g2, norm_f.reshape(1, d))
```

```python
import functools
import math

import numpy as np
import jax
import jax.numpy as jnp
from jax import lax
from jax.experimental import pallas as pl
from jax.experimental.pallas import tpu as pltpu
from jax.experimental.pallas import tpu_sc as plsc

F32 = jnp.float32
BF16 = jnp.bfloat16

GRID_W = 64
CONV_W = 4
LRU_C = 8.0
TOP_K = 4
F_GROUP_DIM = 128
RNN_BLOCK = 128
SWIGLU_ALPHA = 1.702
SWIGLU_LIMIT = 7.0
EPS = 1e-6

V7X_LANES = 128
V7X_SUBLANES = 8
V7X_BF16_ROWS = 16
V7X_VMEM_LIMIT = 56 * 1024 * 1024

ROW_TILE = GRID_W * V7X_SUBLANES
FFT_PERM_ROWS = V7X_BF16_ROWS * GRID_W
FFT_COLS = 256
SCAN_ROWS = 2048
MOE_ROWS = 512
SC_WINDOW = 128
HIGH_HALF = -65536
LOW_HALF = 65535


def _dot(a, b):
    return jnp.dot(a, b, preferred_element_type=F32)


def _dot_nt(a, b):
    return lax.dot_general(a, b, (((1,), (1,)), ((), ())), preferred_element_type=F32)


def _split_bf16(a):
    hi = a.astype(BF16)
    return hi, (a - hi.astype(F32)).astype(BF16)


def _dot3(a, b):
    ah, al = _split_bf16(a)
    bh, bl = _split_bf16(b)
    return _dot(ah, bh) + _dot(ah, bl) + _dot(al, bh)


def _sigmoid(x):
    return 0.5 * (jnp.tanh(0.5 * x) + 1.0)


def _gelu_tanh(x):
    c = math.sqrt(2.0 / math.pi)
    return 0.5 * x * (1.0 + jnp.tanh(c * (x + 0.044715 * (x * x * x))))


def _pack_bf16_pairs(v):
    c = v.shape[1] // 2
    bits = pltpu.bitcast(v.astype(BF16).astype(F32), jnp.int32)
    return (bits[:, :c] & HIGH_HALF) | ((bits[:, c:] >> 16) & LOW_HALF)


def _unpack_bf16_pairs(p):
    hi = pltpu.bitcast(p & HIGH_HALF, F32)
    lo = pltpu.bitcast(p << 16, F32)
    return jnp.concatenate([hi, lo], axis=1)


def _rms(x, g):
    return x * lax.rsqrt(jnp.mean(x * x, axis=-1, keepdims=True) + EPS) * g


def _modulate(xn, sc, sh):
    if sc.shape[0] == 1:
        return xn * (1.0 + sc) + sh
    r, d = xn.shape
    nb = sc.shape[0]
    x3 = xn.reshape(r // nb, nb, d)
    return (x3 * (1.0 + sc)[None] + sh[None]).reshape(r, d)


def _to_time_major(x3):
    nb, t, d = x3.shape
    return pltpu.einshape("btd->tbd", x3).reshape(t * nb, d)


def _to_batch_major(v, nb):
    r, d = v.shape
    return pltpu.einshape("tbd->btd", v.reshape(r // nb, nb, d))


def _per_batch(v, g):
    r, d = v.shape
    nb = g.shape[0]
    return (v.reshape(r // nb, nb, d) * g[None]).reshape(r, d)


def _conv_time_major(z, cw, cb, nb):
    r, c = z.shape

    def shifted(s):
        k = abs(s) * nb
        zero = jnp.zeros((k, c), z.dtype)
        if s > 0:
            return jnp.concatenate([zero, z[:r - k]], axis=0)
        return jnp.concatenate([z[k:], zero], axis=0)

    return (cb + shifted(2) * cw[0:1] + shifted(1) * cw[1:2] + z * cw[2:3]
            + shifted(-1) * cw[3:4])


def _ada_kernel(c_ref, w_ref, b_ref, o_ref):
    cnd = c_ref[...]
    o_ref[...] = _dot3(cnd * _sigmoid(cnd), w_ref[...]) + b_ref[...]


def _ada_call(cond, w, b):
    r, d = cond.shape
    n = w.shape[1]
    return pl.pallas_call(
        _ada_kernel,
        grid=(n // d,),
        in_specs=[pl.BlockSpec((r, d), lambda j: (0, 0)),
                  pl.BlockSpec((d, d), lambda j: (0, j)),
                  pl.BlockSpec((1, d), lambda j: (0, j))],
        out_specs=pl.BlockSpec((r, d), lambda j: (0, j)),
        out_shape=jax.ShapeDtypeStruct((r, n), F32),
        name="ada_modulation",
    )(cond, w, b)


def _ctx_in_kernel(x_ref, sc_ref, sh_ref, g_ref, w_ref, cw_ref, cb_ref, u_ref, *, nb):
    h = _modulate(_rms(x_ref[...], g_ref[...]), sc_ref[...], sh_ref[...])
    z = _dot(h.astype(BF16), w_ref[...])
    u_ref[...] = _conv_time_major(z, cw_ref[...], cb_ref[...], nb).astype(u_ref.dtype)


def _ctx_in_call(ctx_t, sc, sh, g, w_r, cw, cb, nb):
    r, d = ctx_t.shape
    c = w_r.shape[1]
    full = lambda shape: pl.BlockSpec(shape, lambda i: (0,) * len(shape))
    return pl.pallas_call(
        functools.partial(_ctx_in_kernel, nb=nb),
        grid=(1,),
        in_specs=[full((r, d)), full((1, d)), full((1, d)), full((1, d)), full((d, c)),
                  full((CONV_W, c)), full((1, c))],
        out_specs=full((r, c)),
        out_shape=jax.ShapeDtypeStruct((r, c), BF16),
        compiler_params=pltpu.CompilerParams(vmem_limit_bytes=V7X_VMEM_LIMIT),
        name="ctx_in_proj",
    )(ctx_t, sc, sh, g, w_r, cw, cb)


def _in_proj_kernel(x_ref, sc_ref, sh_ref, g_ref, w_ref, cw_ref, cb_ref, pd_ref, cs_ref,
                    a_ref, b_ref, u_ref, gg_ref, gf_ref, gr_ref, *, nb, d_f, d_rnn):
    d = x_ref.shape[-1]
    x = _to_time_major(x_ref[...])
    h = _modulate(_rms(x, g_ref[...]), sc_ref[...], sh_ref[...]).astype(BF16)
    c_r, c_g, c_m = d_f, d_f + d_rnn, d_f + 2 * d_rnn

    zf = _dot(h, w_ref[:, 0:c_r]).astype(BF16)
    zf = _dot(pd_ref[...], zf).astype(BF16)
    for grp in range(d_f // F_GROUP_DIM):
        ab = _dot(zf[:, grp * F_GROUP_DIM:(grp + 1) * F_GROUP_DIM], cs_ref[...])
        for bi in range(nb):
            rows = slice(bi * GRID_W, (bi + 1) * GRID_W)
            col = bi * d_f + grp * F_GROUP_DIM
            a_ref[:, col:col + F_GROUP_DIM] = ab[rows, :F_GROUP_DIM].astype(BF16)
            b_ref[:, col:col + F_GROUP_DIM] = ab[rows, F_GROUP_DIM:].astype(BF16)

    zr = _dot(h, w_ref[:, c_r:c_g])
    u_ref[...] = _conv_time_major(zr, cw_ref[...], cb_ref[...], nb).astype(BF16)
    gg_ref[...] = _gelu_tanh(_dot(h, w_ref[:, c_g:c_m])).astype(BF16)
    gf_ref[...] = _sigmoid(_dot(h, w_ref[:, c_m:c_m + d])).astype(BF16)
    gr_ref[...] = _sigmoid(_dot(h, w_ref[:, c_m + d:c_m + 2 * d])).astype(BF16)


def _in_proj_call(x, sc, sh, g, w_in, cw, cb, pd, cs, d_f, d_rnn):
    nb, seq, d = x.shape
    n = nb * seq
    d_in = w_in.shape[1]
    nt = n // ROW_TILE
    const = lambda shape: pl.BlockSpec(shape, lambda i: (0,) * len(shape))
    row = lambda c: pl.BlockSpec((ROW_TILE, c), lambda i: (i, 0))
    wide = pl.BlockSpec((GRID_W, nb * d_f), lambda i: (i, 0))
    return pl.pallas_call(
        functools.partial(_in_proj_kernel, nb=nb, d_f=d_f, d_rnn=d_rnn),
        grid=(nt,),
        in_specs=[pl.BlockSpec((nb, GRID_W, d), lambda i: (0, i, 0)), const((nb, d)),
                  const((nb, d)), const((1, d)), const((d, d_in)),
                  const((CONV_W, d_rnn)), const((1, d_rnn)), const((ROW_TILE, ROW_TILE)),
                  const((F_GROUP_DIM, 2 * F_GROUP_DIM))],
        out_specs=[wide, wide, row(d_rnn), row(d_rnn), row(d), row(d)],
        out_shape=[jax.ShapeDtypeStruct((seq, nb * d_f), BF16),
                   jax.ShapeDtypeStruct((seq, nb * d_f), BF16),
                   jax.ShapeDtypeStruct((n, d_rnn), BF16),
                   jax.ShapeDtypeStruct((n, d_rnn), BF16),
                   jax.ShapeDtypeStruct((n, d), BF16),
                   jax.ShapeDtypeStruct((n, d), BF16)],
        compiler_params=pltpu.CompilerParams(dimension_semantics=("parallel",),
                                             vmem_limit_bytes=V7X_VMEM_LIMIT),
        name="latent_in_proj",
    )(x, sc, sh, g, w_in, cw, cb, pd, cs)


def _fft_kernel(a_ref, b_ref, d1_ref, m3c_ref, m3s_ref, o_ref, s_ref, *, n1, scale):
    tn = a_ref.shape[1]
    blk = V7X_BF16_ROWS
    for j in range(n1 // blk):
        rows = slice(j * FFT_PERM_ROWS, (j + 1) * FFT_PERM_ROWS)
        s_ref[:, j * blk:(j + 1) * blk, :] = pltpu.einshape(
            "abd->bad", a_ref[rows, :].reshape(blk, GRID_W, tn))
        s_ref[:, n1 + j * blk:n1 + (j + 1) * blk, :] = pltpu.einshape(
            "abd->bad", b_ref[rows, :].reshape(blk, GRID_W, tn))

    def stage1(l2, carry):
        s_ref[l2] = _dot(d1_ref[l2], s_ref[l2]).astype(BF16)
        return carry

    lax.fori_loop(0, GRID_W, stage1, 0)

    for i in range(n1 // blk):
        xr = s_ref[:, i * blk:(i + 1) * blk, :].reshape(FFT_PERM_ROWS, tn)
        xi = s_ref[:, n1 + i * blk:n1 + (i + 1) * blk, :].reshape(FFT_PERM_ROWS, tn)
        y = (_dot(m3c_ref[...], xr) + _dot(m3s_ref[...], xi)) * scale
        o_ref[:, i * blk:(i + 1) * blk, :] = y.astype(BF16).reshape(GRID_W, blk, tn)


def _fft_constants(seq):
    n1 = seq // GRID_W
    blk = V7X_BF16_ROWS
    k1 = np.arange(n1)[None, :, None]
    l1 = np.arange(n1)[None, None, :]
    l2v = np.arange(GRID_W)[:, None, None]
    ang = 2.0 * np.pi * ((k1 * (GRID_W * l1 + l2v)) % seq) / seq
    dc, ds = np.cos(ang), np.sin(ang)
    d1 = np.concatenate([np.concatenate([dc, -ds], axis=2),
                         np.concatenate([-ds, -dc], axis=2)], axis=1)
    k2 = np.arange(GRID_W)[:, None]
    l2m = np.arange(GRID_W)[None, :]
    ang3 = 2.0 * np.pi * ((k2 * l2m) % GRID_W) / GRID_W
    eye = np.eye(blk)
    m3c = np.kron(np.cos(ang3), eye)
    m3s = np.kron(np.sin(ang3), eye)
    as_bf16 = lambda v: jnp.asarray(v, F32).astype(BF16)
    return as_bf16(d1), as_bf16(m3c), as_bf16(m3s)


def _fft_call(a_w, b_w):
    seq, nw = a_w.shape
    n1 = seq // GRID_W
    tn = min(FFT_COLS, nw)
    d1, m3c, m3s = _fft_constants(seq)
    scale = 1.0 / math.sqrt(seq * F_GROUP_DIM)
    const = lambda shape: pl.BlockSpec(shape, lambda j: (0,) * len(shape),
                                       pipeline_mode=pl.Buffered(1))
    strip = pl.BlockSpec((seq, tn), lambda j: (0, j))
    out = pl.pallas_call(
        functools.partial(_fft_kernel, n1=n1, scale=scale),
        grid=(nw // tn,),
        in_specs=[strip, strip, const(d1.shape), const(m3c.shape), const(m3s.shape)],
        out_specs=pl.BlockSpec((GRID_W, n1, tn), lambda j: (0, 0, j)),
        out_shape=jax.ShapeDtypeStruct((GRID_W, n1, nw), BF16),
        scratch_shapes=[pltpu.VMEM((GRID_W, 2 * n1, tn), BF16)],
        compiler_params=pltpu.CompilerParams(dimension_semantics=("parallel",),
                                             vmem_limit_bytes=V7X_VMEM_LIMIT),
        name="position_dft",
    )(a_w, b_w, d1, m3c, m3s)
    return out.reshape(seq, nw)


def _scan_kernel(u_ref, wg_ref, ba_ref, bx_ref, sp_ref, h0_ref, y_ref, hf_ref, a_s, b_s, h_s,
                 *, reverse, steps, heads, nb):
    @pl.when(pl.program_id(0) == 0)
    def _():
        h_s[...] = h0_ref[...]

    u = u_ref[...]
    for hd in range(heads):
        sl = slice(hd * RNN_BLOCK, (hd + 1) * RNN_BLOCK)
        ub = u[:, sl]
        g = _dot(ub, wg_ref[hd])
        tr = jnp.tanh(g[:, :RNN_BLOCK] + ba_ref[:, sl])
        ti = jnp.tanh(g[:, RNN_BLOCK:] + bx_ref[:, sl])
        log_a = sp_ref[:, sl] * (1.0 + tr)
        a = jnp.exp(log_a)
        q = (a * a + 1.0) * (-jnp.tanh(log_a))
        mult = jnp.where(q > 0.0, q * lax.rsqrt(q), 0.0)
        a_s[:, sl] = a
        b_s[:, sl] = mult * ((1.0 + ti) * (0.5 * ub.astype(F32)))

    def step(j, h):
        t = (steps - 1 - j) if reverse else j
        r0 = pl.multiple_of(t * nb, nb)
        h = a_s[pl.ds(r0, nb), :] * h + b_s[pl.ds(r0, nb), :]
        b_s[pl.ds(r0, nb), :] = h
        return h

    h = lax.fori_loop(0, steps, step, h_s[...], unroll=8)
    h_s[...] = h
    hf_ref[...] = h
    y_ref[...] = b_s[...].astype(y_ref.dtype)


def _scan_call(u, wg, ba, bx, sp, h0, reverse, name):
    n, c = u.shape
    nb = h0.shape[0]
    heads = wg.shape[0]
    tr = min(SCAN_ROWS, n)
    nt = n // tr
    const = lambda shape: pl.BlockSpec(shape, lambda i: (0,) * len(shape))
    tile = pl.BlockSpec((tr, c), (lambda i: (nt - 1 - i, 0)) if reverse else (lambda i: (i, 0)))
    return pl.pallas_call(
        functools.partial(_scan_kernel, reverse=reverse, steps=tr // nb, heads=heads, nb=nb),
        grid=(nt,),
        in_specs=[tile, const(wg.shape), const((1, c)), const((1, c)), const((1, c)),
                  const((nb, c))],
        out_specs=[tile, const((nb, c))],
        out_shape=[jax.ShapeDtypeStruct((n, c), BF16), jax.ShapeDtypeStruct((nb, c), F32)],
        scratch_shapes=[pltpu.VMEM((tr, c), F32), pltpu.VMEM((tr, c), F32),
                        pltpu.VMEM((nb, c), F32)],
        compiler_params=pltpu.CompilerParams(dimension_semantics=("arbitrary",),
                                             vmem_limit_bytes=V7X_VMEM_LIMIT),
        name=name,
    )(u, wg, ba, bx, sp, h0)


def _merge_kernel(x_ref, yf_ref, yb_ref, gg_ref, gf_ref, gr_ref, yw_ref, pi_ref, wf_ref,
                  wr_ref, wo_ref, g1_ref, n2_ref, sc2_ref, sh2_ref, wrt_ref, brt_ref, tri_ref,
                  x1_ref, hm_ref, ti_ref, tw_ref, rk_ref, cnt_ref, carry, *, nb, d_f):
    @pl.when(pl.program_id(0) == 0)
    def _():
        carry[...] = jnp.zeros_like(carry)

    yw = jnp.concatenate([yw_ref[:, bi * d_f:(bi + 1) * d_f] for bi in range(nb)], axis=0)
    yfm = _dot(pi_ref[...], yw).astype(BF16)
    f_out = _dot(yfm, wf_ref[...])
    y = yf_ref[...].astype(F32) + yb_ref[...].astype(F32)
    r_out = _dot((y * gg_ref[...].astype(F32)).astype(BF16), wr_ref[...])
    mix = gf_ref[...].astype(F32) * f_out + gr_ref[...].astype(F32) * r_out
    x1 = _to_time_major(x_ref[...]) + _per_batch(_dot(mix.astype(BF16), wo_ref[...]), g1_ref[...])
    x1_ref[...] = x1

    h2 = _modulate(_rms(x1, n2_ref[...]), sc2_ref[...], sh2_ref[...])
    hm_ref[...] = _pack_bf16_pairs(h2)

    hh, hl = _split_bf16(h2)
    wh, wl = _split_bf16(wrt_ref[...])
    lg = _dot_nt(wh, hh) + _dot_nt(wl, hh) + _dot_nt(wh, hl) + brt_ref[...]
    ne, tm = lg.shape
    eid = lax.broadcasted_iota(jnp.int32, (ne, tm), 0)
    row_k = lax.broadcasted_iota(jnp.int32, (TOP_K, tm), 0)
    vals, hots = [], []
    top_i = jnp.zeros((TOP_K, tm), jnp.int32)
    for k in range(TOP_K):
        m = jnp.max(lg, axis=0, keepdims=True)
        idx = jnp.min(jnp.where(lg == m, eid, ne), axis=0, keepdims=True)
        hit = eid == idx
        vals.append(m)
        hots.append(hit.astype(F32))
        top_i = jnp.where(row_k == k, idx, top_i)
        lg = jnp.where(hit, -jnp.inf, lg)
    exps = [jnp.exp(v - vals[0]) for v in vals]
    inv = 1.0 / (exps[0] + exps[1] + exps[2] + exps[3])
    ti_ref[...] = top_i
    w_pad = jnp.concatenate([e * inv for e in exps]
                            + [jnp.zeros((V7X_LANES - TOP_K, tm), F32)], axis=0)
    tw_ref[...] = jnp.transpose(w_pad)[:, :TOP_K]

    per_tok = hots[0] + hots[1] + hots[2] + hots[3]
    before = _dot(per_tok.astype(BF16), tri_ref[...]) + carry[...]
    rank = jnp.zeros((TOP_K, tm), F32)
    for k in range(TOP_K):
        rank = jnp.where(row_k == k, jnp.sum(hots[k] * before, axis=0, keepdims=True), rank)
    rk_ref[...] = rank.astype(jnp.int32)
    carry[...] = carry[...] + jnp.sum(per_tok, axis=1, keepdims=True)
    cnt_ref[...] = carry[...]


def _merge_call(x, y_f, y_b, gg, gf, gr, yw, pint, w_f, w_r, w_o, g1, n2, sc2, sh2, w_rt, b_rt,
                tri, d_f):
    nb, seq, d = x.shape
    n = nb * seq
    d_rnn = y_f.shape[1]
    ne = w_rt.shape[0]
    nt = n // ROW_TILE
    const = lambda shape: pl.BlockSpec(shape, lambda i: (0,) * len(shape))
    row = lambda c: pl.BlockSpec((ROW_TILE, c), lambda i: (i, 0))
    k_major = pl.BlockSpec((TOP_K, ROW_TILE), lambda i: (0, i))
    return pl.pallas_call(
        functools.partial(_merge_kernel, nb=nb, d_f=d_f),
        grid=(nt,),
        in_specs=[pl.BlockSpec((nb, GRID_W, d), lambda i: (0, i, 0)), row(d_rnn), row(d_rnn),
                  row(d_rnn), row(d), row(d), pl.BlockSpec((GRID_W, nb * d_f), lambda i: (i, 0)),
                  const((ROW_TILE, ROW_TILE)), const((d_f, d)), const((d_rnn, d)), const((d, d)),
                  const((nb, d)), const((1, d)), const((nb, d)), const((nb, d)),
                  const((ne, d)), const((ne, 1)), const((ROW_TILE, ROW_TILE))],
        out_specs=[row(d), row(d // 2), k_major, row(TOP_K), k_major, const((ne, 1))],
        out_shape=[jax.ShapeDtypeStruct((n, d), F32), jax.ShapeDtypeStruct((n, d // 2), jnp.int32),
                   jax.ShapeDtypeStruct((TOP_K, n), jnp.int32),
                   jax.ShapeDtypeStruct((n, TOP_K), F32),
                   jax.ShapeDtypeStruct((TOP_K, n), jnp.int32),
                   jax.ShapeDtypeStruct((ne, 1), F32)],
        scratch_shapes=[pltpu.VMEM((ne, 1), F32)],
        compiler_params=pltpu.CompilerParams(dimension_semantics=("arbitrary",),
                                             vmem_limit_bytes=V7X_VMEM_LIMIT),
        name="merge_and_route",
    )(x, y_f, y_b, gg, gf, gr, yw, pint, w_f, w_r, w_o, g1, n2, sc2, sh2, w_rt, b_rt, tri)


def _sc_mesh():
    return plsc.VectorSubcoreMesh(core_axis_name="core", subcore_axis_name="subcore")


def _sc_scatter_rows(x, idx_rows, n_out):
    n, c = x.shape
    kk = len(idx_rows)
    mesh = _sc_mesh()
    workers = mesh.num_cores * mesh.num_subcores
    per_worker = n // workers
    assert per_worker % SC_WINDOW == 0

    @functools.partial(
        pl.kernel, out_type=jax.ShapeDtypeStruct((n_out, c), x.dtype), mesh=mesh,
        scratch_types=[pltpu.VMEM((SC_WINDOW, c), x.dtype)]
        + [pltpu.VMEM((SC_WINDOW,), jnp.int32)] * kk + [pltpu.SemaphoreType.DMA],
        name="sc_dispatch_scatter")
    def scatter(x_hbm, *refs):
        idx_hbm, o_hbm, rows_v = refs[:kk], refs[kk], refs[kk + 1]
        idx_v, sem = refs[kk + 2:2 * kk + 2], refs[2 * kk + 2]
        wid = lax.axis_index("subcore") * mesh.num_cores + lax.axis_index("core")

        @pl.loop(0, per_worker // SC_WINDOW)
        def _(j):
            base = pl.multiple_of(wid * per_worker + j * SC_WINDOW, SC_WINDOW)
            pltpu.sync_copy(x_hbm.at[pl.ds(base, SC_WINDOW)], rows_v)
            for ih, iv in zip(idx_hbm, idx_v):
                pltpu.sync_copy(ih.at[pl.ds(base, SC_WINDOW)], iv)
            copies = [pltpu.async_copy(rows_v, o_hbm.at[iv], sem) for iv in idx_v]
            for cp in copies:
                cp.wait()

    return scatter(x, *idx_rows)


def _sc_gather_rows(src, idx):
    n_out = idx.shape[0]
    c = src.shape[1]
    mesh = _sc_mesh()
    workers = mesh.num_cores * mesh.num_subcores
    per_worker = n_out // workers
    assert per_worker % SC_WINDOW == 0

    @functools.partial(
        pl.kernel, out_type=jax.ShapeDtypeStruct((n_out, c), src.dtype), mesh=mesh,
        scratch_types=[pltpu.VMEM((SC_WINDOW, c), src.dtype), pltpu.VMEM((SC_WINDOW,), jnp.int32),
                       pltpu.SemaphoreType.DMA],
        name="sc_combine_gather")
    def gather(src_hbm, idx_hbm, o_hbm, rows_v, idx_v, sem):
        wid = lax.axis_index("subcore") * mesh.num_cores + lax.axis_index("core")

        @pl.loop(0, per_worker // SC_WINDOW)
        def _(j):
            base = pl.multiple_of(wid * per_worker + j * SC_WINDOW, SC_WINDOW)
            pltpu.sync_copy(idx_hbm.at[pl.ds(base, SC_WINDOW)], idx_v)
            pltpu.async_copy(src_hbm.at[idx_v], rows_v, sem).wait()
            pltpu.sync_copy(rows_v, o_hbm.at[pl.ds(base, SC_WINDOW)])

    return gather(src, idx)


def _expert_kernel(be_ref, nv_ref, x_ref, wgu_ref, bgu_ref, wd_ref, bd_ref, y_ref, wgu_s, wd_s,
                   *, d_e):
    i = pl.program_id(0)
    valid = nv_ref[i]

    @pl.when((i == 0) | (be_ref[i] != be_ref[jnp.maximum(i - 1, 0)]))
    def _():
        wgu_s[...] = wgu_ref[...].astype(BF16)
        wd_s[...] = wd_ref[...].astype(BF16)

    @pl.when(valid > 0)
    def _():
        rows = lax.broadcasted_iota(jnp.int32, x_ref.shape, 0)
        x = _unpack_bf16_pairs(jnp.where(rows < valid, x_ref[...], 0)).astype(BF16)
        gu = _dot(x, wgu_s[...]) + bgu_ref[...]
        gate = jnp.minimum(gu[:, :d_e], SWIGLU_LIMIT)
        up = jnp.clip(gu[:, d_e:], -SWIGLU_LIMIT, SWIGLU_LIMIT)
        act = (up + 1.0) * gate * _sigmoid(SWIGLU_ALPHA * gate)
        y_ref[...] = _pack_bf16_pairs(_dot(act.astype(BF16), wd_s[...]) + bd_ref[...])

    @pl.when(valid <= 0)
    def _():
        y_ref[...] = jnp.zeros_like(y_ref)


def _expert_call(xs, block_expert, block_valid, w_gu, b_gu, w_down, b_down):
    p, dh = xs.shape
    ne, d, d_e2 = w_gu.shape
    d_e = d_e2 // 2
    nblk = p // MOE_ROWS
    grid_spec = pltpu.PrefetchScalarGridSpec(
        num_scalar_prefetch=2,
        grid=(nblk,),
        in_specs=[pl.BlockSpec((MOE_ROWS, dh), lambda i, be, nv: (i, 0)),
                  pl.BlockSpec((None, d, d_e2), lambda i, be, nv: (be[i], 0, 0)),
                  pl.BlockSpec((None, 1, d_e2), lambda i, be, nv: (be[i], 0, 0)),
                  pl.BlockSpec((None, d_e, d), lambda i, be, nv: (be[i], 0, 0)),
                  pl.BlockSpec((None, 1, d), lambda i, be, nv: (be[i], 0, 0))],
        out_specs=pl.BlockSpec((MOE_ROWS, dh), lambda i, be, nv: (i, 0)),
        scratch_shapes=[pltpu.VMEM((d, d_e2), BF16), pltpu.VMEM((d_e, d), BF16)],
    )
    return pl.pallas_call(
        functools.partial(_expert_kernel, d_e=d_e),
        grid_spec=grid_spec,
        out_shape=jax.ShapeDtypeStruct((p, dh), jnp.int32),
        compiler_params=pltpu.CompilerParams(dimension_semantics=("arbitrary",),
                                             vmem_limit_bytes=V7X_VMEM_LIMIT),
        name="expert_ffn",
    )(block_expert, block_valid, xs, w_gu, b_gu.reshape(ne, 1, d_e2), w_down,
      b_down.reshape(ne, 1, d))


def _combine_kernel(x1_ref, yg_ref, tw_ref, g2_ref, nf_ref, o_ref):
    w = tw_ref[...]
    moe = w[:, 0:1] * _unpack_bf16_pairs(yg_ref[0])
    for k in range(1, TOP_K):
        moe = moe + w[:, k:k + 1] * _unpack_bf16_pairs(yg_ref[k])
    out = _rms(x1_ref[...] + _per_batch(moe, g2_ref[...]), nf_ref[...])
    o_ref[...] = _to_batch_major(out, g2_ref.shape[0])


def _combine_call(x1, yg, top_w, g2, nf):
    n, d = x1.shape
    nb = g2.shape[0]
    seq = n // nb
    const = lambda shape: pl.BlockSpec(shape, lambda i: (0,) * len(shape))
    row = lambda c: pl.BlockSpec((ROW_TILE, c), lambda i: (i, 0))
    return pl.pallas_call(
        _combine_kernel,
        grid=(n // ROW_TILE,),
        in_specs=[row(d), pl.BlockSpec((TOP_K, ROW_TILE, d // 2), lambda i: (0, i, 0)),
                  row(TOP_K), const((nb, d)), const((1, d))],
        out_specs=pl.BlockSpec((nb, GRID_W, d), lambda i: (0, i, 0)),
        out_shape=jax.ShapeDtypeStruct((nb, seq, d), F32),
        compiler_params=pltpu.CompilerParams(dimension_semantics=("parallel",),
                                             vmem_limit_bytes=V7X_VMEM_LIMIT),
        name="combine_final_norm",
    )(x1, yg, top_w, g2, nf)


def _tile_constants(nb):
    t = np.arange(GRID_W)[:, None]
    b = np.arange(nb)[None, :]
    deint = np.zeros((ROW_TILE, ROW_TILE), np.float32)
    deint[(b * GRID_W + t).ravel(), (t * nb + b).ravel()] = 1.0
    c = np.arange(F_GROUP_DIM)
    ang = 2.0 * np.pi * ((c[:, None] * c[None, :]) % F_GROUP_DIM) / F_GROUP_DIM
    cs = np.concatenate([np.cos(ang), np.sin(ang)], axis=1)
    tri = np.triu(np.ones((ROW_TILE, ROW_TILE), np.float32), 1)
    as_bf16 = lambda v: jnp.asarray(v, F32).astype(BF16)
    return as_bf16(deint), as_bf16(deint.T), as_bf16(cs), as_bf16(tri)


def kernel(x, c, ctx, c_ctx, w_ada, b_ada, norm1, w_in, conv_w, conv_b, gate_a_w, gate_a_b,
           gate_x_w, gate_x_b, lru_lambda, w_fourier, w_rnn, w_out, norm2, w_router, b_router,
           w_gu, b_gu, w_down, b_down, norm_f):
    nb, seq, d = x.shape
    depth = w_ada.shape[0]
    assert depth == 1 and nb == V7X_SUBLANES and seq % FFT_PERM_ROWS == 0
    n = nb * seq
    d_rnn = conv_w.shape[-1]
    d_f = w_fourier.shape[1]
    ne = w_router.shape[-1]
    lyr = 0
    deint, inter, cs, tri = _tile_constants(nb)

    ctx_t = jnp.transpose(ctx, (1, 0, 2)).reshape(-1, d)

    cond = jnp.zeros((2 * V7X_SUBLANES, d), F32).at[:nb].set(c).at[nb].set(c_ctx)
    mods = _ada_call(cond, w_ada[lyr], b_ada[lyr].reshape(1, -1))
    sh1, sc1, g1, sh2, sc2, g2 = [mods[:nb, k * d:(k + 1) * d] for k in range(6)]
    csh1, csc1 = mods[nb:nb + 1, 0:d], mods[nb:nb + 1, d:2 * d]

    w_in_b = w_in[lyr].astype(BF16)
    cw, cb = conv_w[lyr], conv_b[lyr].reshape(1, d_rnn)
    n1r = norm1[lyr].reshape(1, d)
    wg = (0.5 * jnp.concatenate([gate_a_w[lyr], gate_x_w[lyr]], axis=-1)).astype(BF16)
    sp = (-0.5 * LRU_C) * jax.nn.softplus(-lru_lambda[lyr].astype(F32))
    scan = lambda u, dr, h0, name: _scan_call(
        u, wg[dr], 0.5 * gate_a_b[lyr, dr].reshape(1, -1), 0.5 * gate_x_b[lyr, dr].reshape(1, -1),
        sp[dr].reshape(1, -1), h0, bool(dr), name)

    u_ctx = _ctx_in_call(ctx_t, csc1, csh1, n1r, w_in_b[:, d_f:d_f + d_rnn], cw, cb, nb)
    h_zero = jnp.zeros((nb, d_rnn), F32)
    _, h_ctx_f = scan(u_ctx, 0, h_zero, "ctx_scan_fwd")
    _, h_ctx_b = scan(u_ctx, 1, h_zero, "ctx_scan_bwd")

    a_w, b_w, u, gg, gf, gr = _in_proj_call(x, sc1, sh1, n1r, w_in_b, cw, cb, deint, cs, d_f,
                                            d_rnn)
    y_wide = _fft_call(a_w, b_w)
    y_f, _ = scan(u, 0, h_ctx_f, "scan_fwd")
    y_b, _ = scan(u, 1, h_ctx_b, "scan_bwd")
    x1, hmod, top_i, top_w, rank, counts = _merge_call(
        x, y_f, y_b, gg, gf, gr, y_wide, inter, w_fourier[lyr].astype(BF16),
        w_rnn[lyr].astype(BF16), w_out[lyr].astype(BF16), g1, norm2[lyr].reshape(1, d), sc2, sh2,
        w_router[lyr].T, b_router[lyr].reshape(ne, 1), tri, d_f)

    cnt = counts.reshape(ne).astype(jnp.int32)
    padded = (cnt + MOE_ROWS - 1) // MOE_ROWS * MOE_ROWS
    ends = jnp.cumsum(padded)
    starts = ends - padded
    expert_ids = jnp.arange(ne, dtype=jnp.int32)[:, None, None]
    start_of = jnp.sum(jnp.where(top_i[None] == expert_ids, starts[:, None, None], 0), axis=0)
    dest_km = start_of + rank
    nblk = (n * TOP_K + ne * (MOE_ROWS - 1) + MOE_ROWS - 1) // MOE_ROWS
    block_start = jnp.arange(nblk, dtype=jnp.int32) * MOE_ROWS
    block_expert = jnp.minimum(jnp.sum(block_start[:, None] >= ends[None, :], axis=1),
                               ne - 1).astype(jnp.int32)
    block_valid = jnp.clip((starts + cnt)[block_expert] - block_start, 0, MOE_ROWS)
    block_valid = jnp.where(block_start < ends[-1], block_valid, 0).astype(jnp.int32)

    xs = _sc_scatter_rows(hmod, [dest_km[k] for k in range(TOP_K)], nblk * MOE_ROWS)
    ys = _expert_call(xs, block_expert, block_valid, w_gu[lyr], b_gu[lyr], w_down[lyr], b_down[lyr])
    yg = _sc_gather_rows(ys, dest_km.reshape(TOP_K * n)).reshape(TOP_K, n, d // 2)
    return _combine_call(x1, yg, top_w, g2, norm_f.reshape(1, d))
```

```python
import functools
import math

import numpy as np
import jax
import jax.numpy as jnp
from jax import lax
from jax.experimental import pallas as pl
from jax.experimental.pallas import tpu as pltpu
from jax.experimental.pallas import tpu_sc as plsc

F32 = jnp.float32
BF16 = jnp.bfloat16

GRID_W = 64
CONV_W = 4
LRU_C = 8.0
TOP_K = 4
F_GROUP_DIM = 128
RNN_BLOCK = 128
SWIGLU_ALPHA = 1.702
SWIGLU_LIMIT = 7.0
EPS = 1e-6

V7X_LANES = 128
V7X_SUBLANES = 8
V7X_BF16_ROWS = 16
V7X_VMEM_LIMIT = 56 * 1024 * 1024

ROW_TILE = GRID_W * V7X_SUBLANES
FFT_PERM_ROWS = V7X_BF16_ROWS * GRID_W
FFT_COLS = 256
SCAN_ROWS = 2048
MOE_ROWS = 512
MOE_CHUNKS = 2
SC_WINDOW = 128
HIGH_HALF = -65536
LOW_HALF = 65535


def _dot(a, b):
    return jnp.dot(a, b, preferred_element_type=F32)


def _dot_nt(a, b):
    return lax.dot_general(a, b, (((1,), (1,)), ((), ())), preferred_element_type=F32)


def _split_bf16(a):
    hi = a.astype(BF16)
    return hi, (a - hi.astype(F32)).astype(BF16)


def _dot3(a, b):
    ah, al = _split_bf16(a)
    bh, bl = _split_bf16(b)
    return _dot(ah, bh) + _dot(ah, bl) + _dot(al, bh)


def _sigmoid(x):
    return 0.5 * (jnp.tanh(0.5 * x) + 1.0)


def _gelu_tanh(x):
    c = math.sqrt(2.0 / math.pi)
    return 0.5 * x * (1.0 + jnp.tanh(c * (x + 0.044715 * (x * x * x))))


def _pack_bf16_pairs(v):
    c = v.shape[1] // 2
    bits = pltpu.bitcast(v.astype(BF16).astype(F32), jnp.int32)
    return (bits[:, :c] & HIGH_HALF) | ((bits[:, c:] >> 16) & LOW_HALF)


def _unpack_bf16_pairs(p):
    hi = pltpu.bitcast(p & HIGH_HALF, F32)
    lo = pltpu.bitcast(p << 16, F32)
    return jnp.concatenate([hi, lo], axis=1)


def _rms(x, g):
    return x * lax.rsqrt(jnp.mean(x * x, axis=-1, keepdims=True) + EPS) * g


def _modulate(xn, sc, sh):
    if sc.shape[0] == 1:
        return xn * (1.0 + sc) + sh
    r, d = xn.shape
    nb = sc.shape[0]
    x3 = xn.reshape(r // nb, nb, d)
    return (x3 * (1.0 + sc)[None] + sh[None]).reshape(r, d)


def _to_time_major(x3):
    nb, t, d = x3.shape
    return pltpu.einshape("btd->tbd", x3).reshape(t * nb, d)


def _to_batch_major(v, nb):
    r, d = v.shape
    return pltpu.einshape("tbd->btd", v.reshape(r // nb, nb, d))


def _per_batch(v, g):
    r, d = v.shape
    nb = g.shape[0]
    return (v.reshape(r // nb, nb, d) * g[None]).reshape(r, d)


def _conv_time_major(z, cw, cb, nb):
    r, c = z.shape

    def shifted(s):
        k = abs(s) * nb
        zero = jnp.zeros((k, c), z.dtype)
        if s > 0:
            return jnp.concatenate([zero, z[:r - k]], axis=0)
        return jnp.concatenate([z[k:], zero], axis=0)

    return (cb + shifted(2) * cw[0:1] + shifted(1) * cw[1:2] + z * cw[2:3]
            + shifted(-1) * cw[3:4])


def _ada_kernel(c_ref, w_ref, b_ref, o_ref):
    cnd = c_ref[...]
    o_ref[...] = _dot3(cnd * _sigmoid(cnd), w_ref[...]) + b_ref[...]


def _ada_call(cond, w, b):
    r, d = cond.shape
    n = w.shape[1]
    return pl.pallas_call(
        _ada_kernel,
        grid=(n // d,),
        in_specs=[pl.BlockSpec((r, d), lambda j: (0, 0)),
                  pl.BlockSpec((d, d), lambda j: (0, j)),
                  pl.BlockSpec((1, d), lambda j: (0, j))],
        out_specs=pl.BlockSpec((r, d), lambda j: (0, j)),
        out_shape=jax.ShapeDtypeStruct((r, n), F32),
        name="ada_modulation",
    )(cond, w, b)


def _ctx_in_kernel(x_ref, sc_ref, sh_ref, g_ref, w_ref, cw_ref, cb_ref, u_ref, *, nb):
    h = _modulate(_rms(x_ref[...], g_ref[...]), sc_ref[...], sh_ref[...])
    z = _dot(h.astype(BF16), w_ref[...])
    u_ref[...] = _conv_time_major(z, cw_ref[...], cb_ref[...], nb).astype(u_ref.dtype)


def _ctx_in_call(ctx_t, sc, sh, g, w_r, cw, cb, nb):
    r, d = ctx_t.shape
    c = w_r.shape[1]
    full = lambda shape: pl.BlockSpec(shape, lambda i: (0,) * len(shape))
    return pl.pallas_call(
        functools.partial(_ctx_in_kernel, nb=nb),
        grid=(1,),
        in_specs=[full((r, d)), full((1, d)), full((1, d)), full((1, d)), full((d, c)),
                  full((CONV_W, c)), full((1, c))],
        out_specs=full((r, c)),
        out_shape=jax.ShapeDtypeStruct((r, c), BF16),
        compiler_params=pltpu.CompilerParams(vmem_limit_bytes=V7X_VMEM_LIMIT),
        name="ctx_in_proj",
    )(ctx_t, sc, sh, g, w_r, cw, cb)


def _in_proj_kernel(x_ref, sc_ref, sh_ref, g_ref, w_ref, cw_ref, cb_ref, pd_ref, cs_ref,
                    a_ref, b_ref, u_ref, gg_ref, gf_ref, gr_ref, *, nb, d_f, d_rnn):
    d = x_ref.shape[-1]
    x = _to_time_major(x_ref[...])
    h = _modulate(_rms(x, g_ref[...]), sc_ref[...], sh_ref[...]).astype(BF16)
    c_r, c_g, c_m = d_f, d_f + d_rnn, d_f + 2 * d_rnn

    zf = _dot(h, w_ref[:, 0:c_r]).astype(BF16)
    zf = _dot(pd_ref[...], zf).astype(BF16)
    for grp in range(d_f // F_GROUP_DIM):
        ab = _dot(zf[:, grp * F_GROUP_DIM:(grp + 1) * F_GROUP_DIM], cs_ref[...])
        for bi in range(nb):
            rows = slice(bi * GRID_W, (bi + 1) * GRID_W)
            col = bi * d_f + grp * F_GROUP_DIM
            a_ref[:, col:col + F_GROUP_DIM] = ab[rows, :F_GROUP_DIM].astype(BF16)
            b_ref[:, col:col + F_GROUP_DIM] = ab[rows, F_GROUP_DIM:].astype(BF16)

    zr = _dot(h, w_ref[:, c_r:c_g])
    u_ref[...] = _conv_time_major(zr, cw_ref[...], cb_ref[...], nb).astype(BF16)
    gg_ref[...] = _gelu_tanh(_dot(h, w_ref[:, c_g:c_m])).astype(BF16)
    gf_ref[...] = _sigmoid(_dot(h, w_ref[:, c_m:c_m + d])).astype(BF16)
    gr_ref[...] = _sigmoid(_dot(h, w_ref[:, c_m + d:c_m + 2 * d])).astype(BF16)


def _in_proj_call(x, sc, sh, g, w_in, cw, cb, pd, cs, d_f, d_rnn):
    nb, seq, d = x.shape
    n = nb * seq
    d_in = w_in.shape[1]
    nt = n // ROW_TILE
    const = lambda shape: pl.BlockSpec(shape, lambda i: (0,) * len(shape))
    row = lambda c: pl.BlockSpec((ROW_TILE, c), lambda i: (i, 0))
    wide = pl.BlockSpec((GRID_W, nb * d_f), lambda i: (i, 0))
    return pl.pallas_call(
        functools.partial(_in_proj_kernel, nb=nb, d_f=d_f, d_rnn=d_rnn),
        grid=(nt,),
        in_specs=[pl.BlockSpec((nb, GRID_W, d), lambda i: (0, i, 0)), const((nb, d)),
                  const((nb, d)), const((1, d)), const((d, d_in)),
                  const((CONV_W, d_rnn)), const((1, d_rnn)), const((ROW_TILE, ROW_TILE)),
                  const((F_GROUP_DIM, 2 * F_GROUP_DIM))],
        out_specs=[wide, wide, row(d_rnn), row(d_rnn), row(d), row(d)],
        out_shape=[jax.ShapeDtypeStruct((seq, nb * d_f), BF16),
                   jax.ShapeDtypeStruct((seq, nb * d_f), BF16),
                   jax.ShapeDtypeStruct((n, d_rnn), BF16),
                   jax.ShapeDtypeStruct((n, d_rnn), BF16),
                   jax.ShapeDtypeStruct((n, d), BF16),
                   jax.ShapeDtypeStruct((n, d), BF16)],
        compiler_params=pltpu.CompilerParams(dimension_semantics=("parallel",),
                                             vmem_limit_bytes=V7X_VMEM_LIMIT),
        name="latent_in_proj",
    )(x, sc, sh, g, w_in, cw, cb, pd, cs)


def _fft_kernel(a_ref, b_ref, d1_ref, m3c_ref, m3s_ref, o_ref, s_ref, *, n1, scale):
    tn = a_ref.shape[1]
    blk = V7X_BF16_ROWS
    for j in range(n1 // blk):
        rows = slice(j * FFT_PERM_ROWS, (j + 1) * FFT_PERM_ROWS)
        s_ref[:, j * blk:(j + 1) * blk, :] = pltpu.einshape(
            "abd->bad", a_ref[rows, :].reshape(blk, GRID_W, tn))
        s_ref[:, n1 + j * blk:n1 + (j + 1) * blk, :] = pltpu.einshape(
            "abd->bad", b_ref[rows, :].reshape(blk, GRID_W, tn))

    def stage1(l2, carry):
        s_ref[l2] = _dot(d1_ref[l2], s_ref[l2]).astype(BF16)
        return carry

    lax.fori_loop(0, GRID_W, stage1, 0)

    for i in range(n1 // blk):
        xr = s_ref[:, i * blk:(i + 1) * blk, :].reshape(FFT_PERM_ROWS, tn)
        xi = s_ref[:, n1 + i * blk:n1 + (i + 1) * blk, :].reshape(FFT_PERM_ROWS, tn)
        y = (_dot(m3c_ref[...], xr) + _dot(m3s_ref[...], xi)) * scale
        o_ref[:, i * blk:(i + 1) * blk, :] = y.astype(BF16).reshape(GRID_W, blk, tn)


def _fft_constants(seq):
    n1 = seq // GRID_W
    blk = V7X_BF16_ROWS
    k1 = np.arange(n1)[None, :, None]
    l1 = np.arange(n1)[None, None, :]
    l2v = np.arange(GRID_W)[:, None, None]
    ang = 2.0 * np.pi * ((k1 * (GRID_W * l1 + l2v)) % seq) / seq
    dc, ds = np.cos(ang), np.sin(ang)
    d1 = np.concatenate([np.concatenate([dc, -ds], axis=2),
                         np.concatenate([-ds, -dc], axis=2)], axis=1)
    k2 = np.arange(GRID_W)[:, None]
    l2m = np.arange(GRID_W)[None, :]
    ang3 = 2.0 * np.pi * ((k2 * l2m) % GRID_W) / GRID_W
    eye = np.eye(blk)
    m3c = np.kron(np.cos(ang3), eye)
    m3s = np.kron(np.sin(ang3), eye)
    as_bf16 = lambda v: jnp.asarray(v, F32).astype(BF16)
    return as_bf16(d1), as_bf16(m3c), as_bf16(m3s)


def _fft_call(a_w, b_w):
    seq, nw = a_w.shape
    n1 = seq // GRID_W
    tn = min(FFT_COLS, nw)
    d1, m3c, m3s = _fft_constants(seq)
    scale = 1.0 / math.sqrt(seq * F_GROUP_DIM)
    const = lambda shape: pl.BlockSpec(shape, lambda j: (0,) * len(shape),
                                       pipeline_mode=pl.Buffered(1))
    strip = pl.BlockSpec((seq, tn), lambda j: (0, j))
    out = pl.pallas_call(
        functools.partial(_fft_kernel, n1=n1, scale=scale),
        grid=(nw // tn,),
        in_specs=[strip, strip, const(d1.shape), const(m3c.shape), const(m3s.shape)],
        out_specs=pl.BlockSpec((GRID_W, n1, tn), lambda j: (0, 0, j)),
        out_shape=jax.ShapeDtypeStruct((GRID_W, n1, nw), BF16),
        scratch_shapes=[pltpu.VMEM((GRID_W, 2 * n1, tn), BF16)],
        compiler_params=pltpu.CompilerParams(dimension_semantics=("parallel",),
                                             vmem_limit_bytes=V7X_VMEM_LIMIT),
        name="position_dft",
    )(a_w, b_w, d1, m3c, m3s)
    return out.reshape(seq, nw)


def _scan_kernel(u_ref, wg_ref, ba_ref, bx_ref, sp_ref, h0_ref, y_ref, hf_ref, a_s, b_s, h_s,
                 *, reverse, steps, heads, nb):
    @pl.when(pl.program_id(0) == 0)
    def _():
        h_s[...] = h0_ref[...]

    u = u_ref[...]
    for hd in range(heads):
        sl = slice(hd * RNN_BLOCK, (hd + 1) * RNN_BLOCK)
        ub = u[:, sl]
        g = _dot(ub, wg_ref[hd])
        tr = jnp.tanh(g[:, :RNN_BLOCK] + ba_ref[:, sl])
        ti = jnp.tanh(g[:, RNN_BLOCK:] + bx_ref[:, sl])
        log_a = sp_ref[:, sl] * (1.0 + tr)
        a = jnp.exp(log_a)
        q = (a * a + 1.0) * (-jnp.tanh(log_a))
        mult = jnp.where(q > 0.0, q * lax.rsqrt(q), 0.0)
        a_s[:, sl] = a
        b_s[:, sl] = mult * ((1.0 + ti) * (0.5 * ub.astype(F32)))

    def step(j, h):
        t = (steps - 1 - j) if reverse else j
        r0 = pl.multiple_of(t * nb, nb)
        h = a_s[pl.ds(r0, nb), :] * h + b_s[pl.ds(r0, nb), :]
        b_s[pl.ds(r0, nb), :] = h
        return h

    h = lax.fori_loop(0, steps, step, h_s[...], unroll=8)
    h_s[...] = h
    hf_ref[...] = h
    y_ref[...] = b_s[...].astype(y_ref.dtype)


def _scan_call(u, wg, ba, bx, sp, h0, reverse, name):
    n, c = u.shape
    nb = h0.shape[0]
    heads = wg.shape[0]
    tr = min(SCAN_ROWS, n)
    nt = n // tr
    const = lambda shape: pl.BlockSpec(shape, lambda i: (0,) * len(shape))
    tile = pl.BlockSpec((tr, c), (lambda i: (nt - 1 - i, 0)) if reverse else (lambda i: (i, 0)))
    return pl.pallas_call(
        functools.partial(_scan_kernel, reverse=reverse, steps=tr // nb, heads=heads, nb=nb),
        grid=(nt,),
        in_specs=[tile, const(wg.shape), const((1, c)), const((1, c)), const((1, c)),
                  const((nb, c))],
        out_specs=[tile, const((nb, c))],
        out_shape=[jax.ShapeDtypeStruct((n, c), BF16), jax.ShapeDtypeStruct((nb, c), F32)],
        scratch_shapes=[pltpu.VMEM((tr, c), F32), pltpu.VMEM((tr, c), F32),
                        pltpu.VMEM((nb, c), F32)],
        compiler_params=pltpu.CompilerParams(dimension_semantics=("arbitrary",),
                                             vmem_limit_bytes=V7X_VMEM_LIMIT),
        name=name,
    )(u, wg, ba, bx, sp, h0)


def _merge_kernel(x_ref, yf_ref, yb_ref, gg_ref, gf_ref, gr_ref, yw_ref, pi_ref, wf_ref,
                  wr_ref, wo_ref, g1_ref, n2_ref, sc2_ref, sh2_ref, wrt_ref, brt_ref, tri_ref,
                  x1_ref, hm_ref, ti_ref, tw_ref, rk_ref, cnt_ref, carry, *, nb, d_f):
    @pl.when(pl.program_id(0) == 0)
    def _():
        carry[...] = jnp.zeros_like(carry)

    yw = jnp.concatenate([yw_ref[:, bi * d_f:(bi + 1) * d_f] for bi in range(nb)], axis=0)
    yfm = _dot(pi_ref[...], yw).astype(BF16)
    f_out = _dot(yfm, wf_ref[...])
    y = yf_ref[...].astype(F32) + yb_ref[...].astype(F32)
    r_out = _dot((y * gg_ref[...].astype(F32)).astype(BF16), wr_ref[...])
    mix = gf_ref[...].astype(F32) * f_out + gr_ref[...].astype(F32) * r_out
    x1 = _to_time_major(x_ref[...]) + _per_batch(_dot(mix.astype(BF16), wo_ref[...]), g1_ref[...])
    x1_ref[...] = x1

    h2 = _modulate(_rms(x1, n2_ref[...]), sc2_ref[...], sh2_ref[...])
    hm_ref[...] = _pack_bf16_pairs(h2)

    hh, hl = _split_bf16(h2)
    wh, wl = _split_bf16(wrt_ref[...])
    lg = _dot_nt(wh, hh) + _dot_nt(wl, hh) + _dot_nt(wh, hl) + brt_ref[...]
    ne, tm = lg.shape
    eid = lax.broadcasted_iota(jnp.int32, (ne, tm), 0)
    row_k = lax.broadcasted_iota(jnp.int32, (TOP_K, tm), 0)
    vals, hots = [], []
    top_i = jnp.zeros((TOP_K, tm), jnp.int32)
    for k in range(TOP_K):
        m = jnp.max(lg, axis=0, keepdims=True)
        idx = jnp.min(jnp.where(lg == m, eid, ne), axis=0, keepdims=True)
        hit = eid == idx
        vals.append(m)
        hots.append(hit.astype(F32))
        top_i = jnp.where(row_k == k, idx, top_i)
        lg = jnp.where(hit, -jnp.inf, lg)
    exps = [jnp.exp(v - vals[0]) for v in vals]
    inv = 1.0 / (exps[0] + exps[1] + exps[2] + exps[3])
    ti_ref[...] = top_i
    w_pad = jnp.concatenate([e * inv for e in exps]
                            + [jnp.zeros((V7X_LANES - TOP_K, tm), F32)], axis=0)
    tw_ref[...] = jnp.transpose(w_pad)[:, :TOP_K]

    per_tok = hots[0] + hots[1] + hots[2] + hots[3]
    before = _dot(per_tok.astype(BF16), tri_ref[...]) + carry[...]
    rank = jnp.zeros((TOP_K, tm), F32)
    for k in range(TOP_K):
        rank = jnp.where(row_k == k, jnp.sum(hots[k] * before, axis=0, keepdims=True), rank)
    rk_ref[...] = rank.astype(jnp.int32)
    carry[...] = carry[...] + jnp.sum(per_tok, axis=1, keepdims=True)
    cnt_ref[...] = carry[...]


def _merge_call(x, y_f, y_b, gg, gf, gr, yw, pint, w_f, w_r, w_o, g1, n2, sc2, sh2, w_rt, b_rt,
                tri, d_f, chunk, n_chunks):
    nb, seq, d = x.shape
    n = nb * seq // n_chunks
    d_rnn = y_f.shape[1]
    ne = w_rt.shape[0]
    nt = n // ROW_TILE
    off = chunk * nt
    const = lambda shape: pl.BlockSpec(shape, lambda i: (0,) * len(shape))
    row = lambda c: pl.BlockSpec((ROW_TILE, c), lambda i: (i, 0))
    row_in = lambda c: pl.BlockSpec((ROW_TILE, c), lambda i: (i + off, 0))
    k_major = pl.BlockSpec((TOP_K, ROW_TILE), lambda i: (0, i))
    return pl.pallas_call(
        functools.partial(_merge_kernel, nb=nb, d_f=d_f),
        grid=(nt,),
        in_specs=[pl.BlockSpec((nb, GRID_W, d), lambda i: (0, i + off, 0)), row_in(d_rnn),
                  row_in(d_rnn), row_in(d_rnn), row_in(d), row_in(d),
                  pl.BlockSpec((GRID_W, nb * d_f), lambda i: (i + off, 0)),
                  const((ROW_TILE, ROW_TILE)), const((d_f, d)), const((d_rnn, d)), const((d, d)),
                  const((nb, d)), const((1, d)), const((nb, d)), const((nb, d)),
                  const((ne, d)), const((ne, 1)), const((ROW_TILE, ROW_TILE))],
        out_specs=[row(d), row(d // 2), k_major, row(TOP_K), k_major, const((ne, 1))],
        out_shape=[jax.ShapeDtypeStruct((n, d), F32), jax.ShapeDtypeStruct((n, d // 2), jnp.int32),
                   jax.ShapeDtypeStruct((TOP_K, n), jnp.int32),
                   jax.ShapeDtypeStruct((n, TOP_K), F32),
                   jax.ShapeDtypeStruct((TOP_K, n), jnp.int32),
                   jax.ShapeDtypeStruct((ne, 1), F32)],
        scratch_shapes=[pltpu.VMEM((ne, 1), F32)],
        compiler_params=pltpu.CompilerParams(dimension_semantics=("arbitrary",),
                                             vmem_limit_bytes=V7X_VMEM_LIMIT),
        name="merge_and_route",
    )(x, y_f, y_b, gg, gf, gr, yw, pint, w_f, w_r, w_o, g1, n2, sc2, sh2, w_rt, b_rt, tri)


def _sc_mesh():
    return plsc.VectorSubcoreMesh(core_axis_name="core", subcore_axis_name="subcore")


def _sc_scatter_rows(x, idx_rows, n_out):
    n, c = x.shape
    kk = len(idx_rows)
    mesh = _sc_mesh()
    workers = mesh.num_cores * mesh.num_subcores
    per_worker = n // workers
    assert per_worker % SC_WINDOW == 0

    @functools.partial(
        pl.kernel, out_type=jax.ShapeDtypeStruct((n_out, c), x.dtype), mesh=mesh,
        scratch_types=[pltpu.VMEM((SC_WINDOW, c), x.dtype)]
        + [pltpu.VMEM((SC_WINDOW,), jnp.int32)] * kk + [pltpu.SemaphoreType.DMA],
        name="sc_dispatch_scatter")
    def scatter(x_hbm, *refs):
        idx_hbm, o_hbm, rows_v = refs[:kk], refs[kk], refs[kk + 1]
        idx_v, sem = refs[kk + 2:2 * kk + 2], refs[2 * kk + 2]
        wid = lax.axis_index("subcore") * mesh.num_cores + lax.axis_index("core")

        @pl.loop(0, per_worker // SC_WINDOW)
        def _(j):
            base = pl.multiple_of(wid * per_worker + j * SC_WINDOW, SC_WINDOW)
            pltpu.sync_copy(x_hbm.at[pl.ds(base, SC_WINDOW)], rows_v)
            for ih, iv in zip(idx_hbm, idx_v):
                pltpu.sync_copy(ih.at[pl.ds(base, SC_WINDOW)], iv)
            copies = [pltpu.async_copy(rows_v, o_hbm.at[iv], sem) for iv in idx_v]
            for cp in copies:
                cp.wait()

    return scatter(x, *idx_rows)


def _sc_gather_rows(src, idx):
    n_out = idx.shape[0]
    c = src.shape[1]
    mesh = _sc_mesh()
    workers = mesh.num_cores * mesh.num_subcores
    per_worker = n_out // workers
    assert per_worker % SC_WINDOW == 0

    @functools.partial(
        pl.kernel, out_type=jax.ShapeDtypeStruct((n_out, c), src.dtype), mesh=mesh,
        scratch_types=[pltpu.VMEM((SC_WINDOW, c), src.dtype), pltpu.VMEM((SC_WINDOW,), jnp.int32),
                       pltpu.SemaphoreType.DMA],
        name="sc_combine_gather")
    def gather(src_hbm, idx_hbm, o_hbm, rows_v, idx_v, sem):
        wid = lax.axis_index("subcore") * mesh.num_cores + lax.axis_index("core")

        @pl.loop(0, per_worker // SC_WINDOW)
        def _(j):
            base = pl.multiple_of(wid * per_worker + j * SC_WINDOW, SC_WINDOW)
            pltpu.sync_copy(idx_hbm.at[pl.ds(base, SC_WINDOW)], idx_v)
            pltpu.async_copy(src_hbm.at[idx_v], rows_v, sem).wait()
            pltpu.sync_copy(rows_v, o_hbm.at[pl.ds(base, SC_WINDOW)])

    return gather(src, idx)


def _expert_kernel(be_ref, nv_ref, x_ref, wgu_ref, bgu_ref, wd_ref, bd_ref, y_ref, wgu_s, wd_s,
                   *, d_e):
    i = pl.program_id(0)
    valid = nv_ref[i]

    @pl.when((i == 0) | (be_ref[i] != be_ref[jnp.maximum(i - 1, 0)]))
    def _():
        wgu_s[...] = wgu_ref[...].astype(BF16)
        wd_s[...] = wd_ref[...].astype(BF16)

    @pl.when(valid > 0)
    def _():
        rows = lax.broadcasted_iota(jnp.int32, x_ref.shape, 0)
        x = _unpack_bf16_pairs(jnp.where(rows < valid, x_ref[...], 0)).astype(BF16)
        gu = _dot(x, wgu_s[...]) + bgu_ref[...]
        gate = jnp.minimum(gu[:, :d_e], SWIGLU_LIMIT)
        up = jnp.clip(gu[:, d_e:], -SWIGLU_LIMIT, SWIGLU_LIMIT)
        act = (up + 1.0) * gate * _sigmoid(SWIGLU_ALPHA * gate)
        y_ref[...] = _pack_bf16_pairs(_dot(act.astype(BF16), wd_s[...]) + bd_ref[...])

    @pl.when(valid <= 0)
    def _():
        y_ref[...] = jnp.zeros_like(y_ref)


def _expert_call(xs, block_expert, block_valid, w_gu, b_gu, w_down, b_down):
    p, dh = xs.shape
    ne, d, d_e2 = w_gu.shape
    d_e = d_e2 // 2
    nblk = p // MOE_ROWS
    grid_spec = pltpu.PrefetchScalarGridSpec(
        num_scalar_prefetch=2,
        grid=(nblk,),
        in_specs=[pl.BlockSpec((MOE_ROWS, dh), lambda i, be, nv: (i, 0)),
                  pl.BlockSpec((None, d, d_e2), lambda i, be, nv: (be[i], 0, 0)),
                  pl.BlockSpec((None, 1, d_e2), lambda i, be, nv: (be[i], 0, 0)),
                  pl.BlockSpec((None, d_e, d), lambda i, be, nv: (be[i], 0, 0)),
                  pl.BlockSpec((None, 1, d), lambda i, be, nv: (be[i], 0, 0))],
        out_specs=pl.BlockSpec((MOE_ROWS, dh), lambda i, be, nv: (i, 0)),
        scratch_shapes=[pltpu.VMEM((d, d_e2), BF16), pltpu.VMEM((d_e, d), BF16)],
    )
    return pl.pallas_call(
        functools.partial(_expert_kernel, d_e=d_e),
        grid_spec=grid_spec,
        out_shape=jax.ShapeDtypeStruct((p, dh), jnp.int32),
        compiler_params=pltpu.CompilerParams(dimension_semantics=("arbitrary",),
                                             vmem_limit_bytes=V7X_VMEM_LIMIT),
        name="expert_ffn",
    )(block_expert, block_valid, xs, w_gu, b_gu.reshape(ne, 1, d_e2), w_down,
      b_down.reshape(ne, 1, d))


def _combine_kernel(x1_ref, yg_ref, tw_ref, g2_ref, nf_ref, *rest):
    o_ref = rest[-1]
    w = tw_ref[...]
    moe = w[:, 0:1] * _unpack_bf16_pairs(yg_ref[0])
    for k in range(1, TOP_K):
        moe = moe + w[:, k:k + 1] * _unpack_bf16_pairs(yg_ref[k])
    out = _rms(x1_ref[...] + _per_batch(moe, g2_ref[...]), nf_ref[...])
    o_ref[...] = _to_batch_major(out, g2_ref.shape[0])


def _combine_call(x1, yg, top_w, g2, nf, chunk, n_chunks, prev_out):
    n, d = x1.shape
    nb = g2.shape[0]
    seq = n * n_chunks // nb
    nt = n // ROW_TILE
    off = chunk * nt
    const = lambda shape: pl.BlockSpec(shape, lambda i: (0,) * len(shape))
    row = lambda c: pl.BlockSpec((ROW_TILE, c), lambda i: (i, 0))
    in_specs = [row(d), pl.BlockSpec((TOP_K, ROW_TILE, d // 2), lambda i: (0, i, 0)),
                row(TOP_K), const((nb, d)), const((1, d))]
    args = [x1, yg, top_w, g2, nf]
    aliases = {}
    if prev_out is not None:
        in_specs.append(pl.BlockSpec(memory_space=pl.ANY))
        args.append(prev_out)
        aliases = {len(args) - 1: 0}
    return pl.pallas_call(
        _combine_kernel,
        grid=(nt,),
        in_specs=in_specs,
        out_specs=pl.BlockSpec((nb, GRID_W, d), lambda i: (0, i + off, 0)),
        out_shape=jax.ShapeDtypeStruct((nb, seq, d), F32),
        input_output_aliases=aliases,
        compiler_params=pltpu.CompilerParams(dimension_semantics=("parallel",),
                                             vmem_limit_bytes=V7X_VMEM_LIMIT),
        name="combine_final_norm",
    )(*args)


def _tile_constants(nb):
    t = np.arange(GRID_W)[:, None]
    b = np.arange(nb)[None, :]
    deint = np.zeros((ROW_TILE, ROW_TILE), np.float32)
    deint[(b * GRID_W + t).ravel(), (t * nb + b).ravel()] = 1.0
    c = np.arange(F_GROUP_DIM)
    ang = 2.0 * np.pi * ((c[:, None] * c[None, :]) % F_GROUP_DIM) / F_GROUP_DIM
    cs = np.concatenate([np.cos(ang), np.sin(ang)], axis=1)
    tri = np.triu(np.ones((ROW_TILE, ROW_TILE), np.float32), 1)
    as_bf16 = lambda v: jnp.asarray(v, F32).astype(BF16)
    return as_bf16(deint), as_bf16(deint.T), as_bf16(cs), as_bf16(tri)


def kernel(x, c, ctx, c_ctx, w_ada, b_ada, norm1, w_in, conv_w, conv_b, gate_a_w, gate_a_b,
           gate_x_w, gate_x_b, lru_lambda, w_fourier, w_rnn, w_out, norm2, w_router, b_router,
           w_gu, b_gu, w_down, b_down, norm_f):
    nb, seq, d = x.shape
    depth = w_ada.shape[0]
    assert depth == 1 and nb == V7X_SUBLANES and seq % FFT_PERM_ROWS == 0
    n = nb * seq
    d_rnn = conv_w.shape[-1]
    d_f = w_fourier.shape[1]
    ne = w_router.shape[-1]
    lyr = 0
    deint, inter, cs, tri = _tile_constants(nb)

    ctx_t = jnp.transpose(ctx, (1, 0, 2)).reshape(-1, d)

    cond = jnp.zeros((2 * V7X_SUBLANES, d), F32).at[:nb].set(c).at[nb].set(c_ctx)
    mods = _ada_call(cond, w_ada[lyr], b_ada[lyr].reshape(1, -1))
    sh1, sc1, g1, sh2, sc2, g2 = [mods[:nb, k * d:(k + 1) * d] for k in range(6)]
    csh1, csc1 = mods[nb:nb + 1, 0:d], mods[nb:nb + 1, d:2 * d]

    w_in_b = w_in[lyr].astype(BF16)
    cw, cb = conv_w[lyr], conv_b[lyr].reshape(1, d_rnn)
    n1r = norm1[lyr].reshape(1, d)
    wg = (0.5 * jnp.concatenate([gate_a_w[lyr], gate_x_w[lyr]], axis=-1)).astype(BF16)
    sp = (-0.5 * LRU_C) * jax.nn.softplus(-lru_lambda[lyr].astype(F32))
    scan = lambda u, dr, h0, name: _scan_call(
        u, wg[dr], 0.5 * gate_a_b[lyr, dr].reshape(1, -1), 0.5 * gate_x_b[lyr, dr].reshape(1, -1),
        sp[dr].reshape(1, -1), h0, bool(dr), name)

    u_ctx = _ctx_in_call(ctx_t, csc1, csh1, n1r, w_in_b[:, d_f:d_f + d_rnn], cw, cb, nb)
    h_zero = jnp.zeros((nb, d_rnn), F32)
    _, h_ctx_f = scan(u_ctx, 0, h_zero, "ctx_scan_fwd")
    _, h_ctx_b = scan(u_ctx, 1, h_zero, "ctx_scan_bwd")

    a_w, b_w, u, gg, gf, gr = _in_proj_call(x, sc1, sh1, n1r, w_in_b, cw, cb, deint, cs, d_f,
                                            d_rnn)
    y_wide = _fft_call(a_w, b_w)
    y_f, _ = scan(u, 0, h_ctx_f, "scan_fwd")
    y_b, _ = scan(u, 1, h_ctx_b, "scan_bwd")
    merge_consts = (inter, w_fourier[lyr].astype(BF16), w_rnn[lyr].astype(BF16),
                    w_out[lyr].astype(BF16), g1, norm2[lyr].reshape(1, d), sc2, sh2,
                    w_router[lyr].T, b_router[lyr].reshape(ne, 1), tri, d_f)
    n_c = n // MOE_CHUNKS
    nblk = (n_c * TOP_K + ne * (MOE_ROWS - 1) + MOE_ROWS - 1) // MOE_ROWS
    block_start = jnp.arange(nblk, dtype=jnp.int32) * MOE_ROWS
    expert_ids = jnp.arange(ne, dtype=jnp.int32)[:, None, None]
    out = None
    for chunk in range(MOE_CHUNKS):
        x1, hmod, top_i, top_w, rank, counts = _merge_call(
            x, y_f, y_b, gg, gf, gr, y_wide, *merge_consts, chunk, MOE_CHUNKS)

        cnt = counts.reshape(ne).astype(jnp.int32)
        padded = (cnt + MOE_ROWS - 1) // MOE_ROWS * MOE_ROWS
        ends = jnp.cumsum(padded)
        starts = ends - padded
        start_of = jnp.sum(jnp.where(top_i[None] == expert_ids, starts[:, None, None], 0), axis=0)
        dest_km = start_of + rank
        block_expert = jnp.minimum(jnp.sum(block_start[:, None] >= ends[None, :], axis=1),
                                   ne - 1).astype(jnp.int32)
        block_valid = jnp.clip((starts + cnt)[block_expert] - block_start, 0, MOE_ROWS)
        block_valid = jnp.where(block_start < ends[-1], block_valid, 0).astype(jnp.int32)

        xs = _sc_scatter_rows(hmod, [dest_km[k] for k in range(TOP_K)], nblk * MOE_ROWS)
        ys = _expert_call(xs, block_expert, block_valid, w_gu[lyr], b_gu[lyr], w_down[lyr],
                          b_down[lyr])
        yg = _sc_gather_rows(ys, dest_km.reshape(TOP_K * n_c)).reshape(TOP_K, n_c, d // 2)
        out = _combine_call(x1, yg, top_w, g2, norm_f.reshape(1, d), chunk, MOE_CHUNKS, out)
    return out
```

```python
import functools
import math

import numpy as np
import jax
import jax.numpy as jnp
from jax import lax
from jax.experimental import pallas as pl
from jax.experimental.pallas import tpu as pltpu
from jax.experimental.pallas import tpu_sc as plsc

F32 = jnp.float32
BF16 = jnp.bfloat16

GRID_W = 64
CONV_W = 4
LRU_C = 8.0
TOP_K = 4
F_GROUP_DIM = 128
RNN_BLOCK = 128
SWIGLU_ALPHA = 1.702
SWIGLU_LIMIT = 7.0
EPS = 1e-6
LOG2_E = 1.4426950408889634

V7X_LANES = 128
V7X_SUBLANES = 8
V7X_BF16_ROWS = 16
V7X_VMEM_LIMIT = 56 * 1024 * 1024

ROW_TILE = GRID_W * V7X_SUBLANES
FFT_PERM_ROWS = V7X_BF16_ROWS * GRID_W
FFT_COLS = 256
SCAN_ROWS = 2048
MOE_ROWS = 512
MOE_CHUNKS = 2
SC_WINDOW = 128
HIGH_HALF = -65536
LOW_HALF = 65535


def _dot(a, b):
    return jnp.dot(a, b, preferred_element_type=F32)


def _dot_nt(a, b):
    return lax.dot_general(a, b, (((1,), (1,)), ((), ())), preferred_element_type=F32)


def _split_bf16(a):
    hi = a.astype(BF16)
    return hi, (a - hi.astype(F32)).astype(BF16)


def _dot3(a, b):
    ah, al = _split_bf16(a)
    bh, bl = _split_bf16(b)
    return _dot(ah, bh) + _dot(ah, bl) + _dot(al, bh)


def _sigmoid(x):
    return 0.5 * (jnp.tanh(0.5 * x) + 1.0)


def _gelu_tanh(x):
    c = math.sqrt(2.0 / math.pi)
    return 0.5 * x * (1.0 + jnp.tanh(c * (x + 0.044715 * (x * x * x))))


def _pack_bf16_pairs(v):
    c = v.shape[1] // 2
    bits = pltpu.bitcast(v.astype(BF16).astype(F32), jnp.int32)
    return (bits[:, :c] & HIGH_HALF) | ((bits[:, c:] >> 16) & LOW_HALF)


def _unpack_bf16_pairs(p):
    hi = pltpu.bitcast(p & HIGH_HALF, F32)
    lo = pltpu.bitcast(p << 16, F32)
    return jnp.concatenate([hi, lo], axis=1)


def _rms(x, g):
    return x * lax.rsqrt(jnp.mean(x * x, axis=-1, keepdims=True) + EPS) * g


def _modulate(xn, sc, sh):
    if sc.shape[0] == 1:
        return xn * (1.0 + sc) + sh
    r, d = xn.shape
    nb = sc.shape[0]
    x3 = xn.reshape(r // nb, nb, d)
    return (x3 * (1.0 + sc)[None] + sh[None]).reshape(r, d)


def _to_time_major(x3):
    nb, t, d = x3.shape
    return pltpu.einshape("btd->tbd", x3).reshape(t * nb, d)


def _to_batch_major(v, nb):
    r, d = v.shape
    return pltpu.einshape("tbd->btd", v.reshape(r // nb, nb, d))


def _per_batch(v, g):
    r, d = v.shape
    nb = g.shape[0]
    return (v.reshape(r // nb, nb, d) * g[None]).reshape(r, d)


def _conv_time_major(z, cw, cb, nb):
    r, c = z.shape

    def shifted(s):
        k = abs(s) * nb
        zero = jnp.zeros((k, c), z.dtype)
        if s > 0:
            return jnp.concatenate([zero, z[:r - k]], axis=0)
        return jnp.concatenate([z[k:], zero], axis=0)

    return (cb + shifted(2) * cw[0:1] + shifted(1) * cw[1:2] + z * cw[2:3]
            + shifted(-1) * cw[3:4])


def _ada_kernel(c_ref, w_ref, b_ref, o_ref):
    cnd = c_ref[...]
    o_ref[...] = _dot3(cnd * _sigmoid(cnd), w_ref[...]) + b_ref[...]


def _ada_call(cond, w, b):
    r, d = cond.shape
    n = w.shape[1]
    return pl.pallas_call(
        _ada_kernel,
        grid=(n // d,),
        in_specs=[pl.BlockSpec((r, d), lambda j: (0, 0)),
                  pl.BlockSpec((d, d), lambda j: (0, j)),
                  pl.BlockSpec((1, d), lambda j: (0, j))],
        out_specs=pl.BlockSpec((r, d), lambda j: (0, j)),
        out_shape=jax.ShapeDtypeStruct((r, n), F32),
        name="ada_modulation",
    )(cond, w, b)


def _ctx_in_kernel(x_ref, sc_ref, sh_ref, g_ref, w_ref, cw_ref, cb_ref, u_ref, *, nb):
    h = _modulate(_rms(x_ref[...], g_ref[...]), sc_ref[...], sh_ref[...])
    z = _dot(h.astype(BF16), w_ref[...])
    u_ref[...] = (0.5 * _conv_time_major(z, cw_ref[...], cb_ref[...], nb)).astype(u_ref.dtype)


def _ctx_in_call(ctx_t, sc, sh, g, w_r, cw, cb, nb):
    r, d = ctx_t.shape
    c = w_r.shape[1]
    full = lambda shape: pl.BlockSpec(shape, lambda i: (0,) * len(shape))
    return pl.pallas_call(
        functools.partial(_ctx_in_kernel, nb=nb),
        grid=(1,),
        in_specs=[full((r, d)), full((1, d)), full((1, d)), full((1, d)), full((d, c)),
                  full((CONV_W, c)), full((1, c))],
        out_specs=full((r, c)),
        out_shape=jax.ShapeDtypeStruct((r, c), BF16),
        compiler_params=pltpu.CompilerParams(vmem_limit_bytes=V7X_VMEM_LIMIT),
        name="ctx_in_proj",
    )(ctx_t, sc, sh, g, w_r, cw, cb)


def _in_proj_kernel(x_ref, sc_ref, sh_ref, g_ref, w_ref, cw_ref, cb_ref, cs_ref,
                    a_ref, b_ref, u_ref, gg_ref, gf_ref, gr_ref, *, nb, d_f, d_rnn):
    d = x_ref.shape[-1]
    h3 = (_rms(x_ref[...], g_ref[...]) * (1.0 + sc_ref[...])[:, None, :]
          + sh_ref[...][:, None, :])
    h = _to_time_major(h3).astype(BF16)
    c_r, c_g, c_m = d_f, d_f + d_rnn, d_f + 2 * d_rnn

    zf = _dot(h3.reshape(nb * GRID_W, d).astype(BF16), w_ref[:, 0:c_r]).astype(BF16)
    for grp in range(d_f // F_GROUP_DIM):
        ab = _dot(zf[:, grp * F_GROUP_DIM:(grp + 1) * F_GROUP_DIM], cs_ref[...])
        for bi in range(nb):
            rows = slice(bi * GRID_W, (bi + 1) * GRID_W)
            col = bi * d_f + grp * F_GROUP_DIM
            a_ref[:, col:col + F_GROUP_DIM] = ab[rows, :F_GROUP_DIM].astype(BF16)
            b_ref[:, col:col + F_GROUP_DIM] = ab[rows, F_GROUP_DIM:].astype(BF16)

    zr = _dot(h, w_ref[:, c_r:c_g])
    u_ref[...] = (0.5 * _conv_time_major(zr, cw_ref[...], cb_ref[...], nb)).astype(BF16)
    gg_ref[...] = _gelu_tanh(_dot(h, w_ref[:, c_g:c_m])).astype(BF16)
    gf_ref[...] = _sigmoid(_dot(h, w_ref[:, c_m:c_m + d])).astype(BF16)
    gr_ref[...] = _sigmoid(_dot(h, w_ref[:, c_m + d:c_m + 2 * d])).astype(BF16)


def _in_proj_call(x, sc, sh, g, w_in, cw, cb, cs, d_f, d_rnn):
    nb, seq, d = x.shape
    n = nb * seq
    d_in = w_in.shape[1]
    nt = n // ROW_TILE
    const = lambda shape: pl.BlockSpec(shape, lambda i: (0,) * len(shape))
    row = lambda c: pl.BlockSpec((ROW_TILE, c), lambda i: (i, 0))
    wide = pl.BlockSpec((GRID_W, nb * d_f), lambda i: (i, 0))
    return pl.pallas_call(
        functools.partial(_in_proj_kernel, nb=nb, d_f=d_f, d_rnn=d_rnn),
        grid=(nt,),
        in_specs=[pl.BlockSpec((nb, GRID_W, d), lambda i: (0, i, 0)), const((nb, d)),
                  const((nb, d)), const((1, d)), const((d, d_in)),
                  const((CONV_W, d_rnn)), const((1, d_rnn)),
                  const((F_GROUP_DIM, 2 * F_GROUP_DIM))],
        out_specs=[wide, wide, row(d_rnn), row(d_rnn), row(d), row(d)],
        out_shape=[jax.ShapeDtypeStruct((seq, nb * d_f), BF16),
                   jax.ShapeDtypeStruct((seq, nb * d_f), BF16),
                   jax.ShapeDtypeStruct((n, d_rnn), BF16),
                   jax.ShapeDtypeStruct((n, d_rnn), BF16),
                   jax.ShapeDtypeStruct((n, d), BF16),
                   jax.ShapeDtypeStruct((n, d), BF16)],
        compiler_params=pltpu.CompilerParams(dimension_semantics=("parallel",),
                                             vmem_limit_bytes=V7X_VMEM_LIMIT),
        name="latent_in_proj",
    )(x, sc, sh, g, w_in, cw, cb, cs)


def _fft_kernel(a_ref, b_ref, d1_ref, m3c_ref, m3s_ref, o_ref, s_ref, *, n1, scale):
    tn = a_ref.shape[1]
    blk = V7X_BF16_ROWS
    for j in range(n1 // blk):
        rows = slice(j * FFT_PERM_ROWS, (j + 1) * FFT_PERM_ROWS)
        s_ref[:, j * blk:(j + 1) * blk, :] = pltpu.einshape(
            "abd->bad", a_ref[rows, :].reshape(blk, GRID_W, tn))
        s_ref[:, n1 + j * blk:n1 + (j + 1) * blk, :] = pltpu.einshape(
            "abd->bad", b_ref[rows, :].reshape(blk, GRID_W, tn))

    def stage1(l2, carry):
        s_ref[l2] = _dot(d1_ref[l2], s_ref[l2]).astype(BF16)
        return carry

    lax.fori_loop(0, GRID_W, stage1, 0)

    for i in range(n1 // blk):
        xr = s_ref[:, i * blk:(i + 1) * blk, :].reshape(FFT_PERM_ROWS, tn)
        xi = s_ref[:, n1 + i * blk:n1 + (i + 1) * blk, :].reshape(FFT_PERM_ROWS, tn)
        y = (_dot(m3c_ref[...], xr) + _dot(m3s_ref[...], xi)) * scale
        o_ref[:, i * blk:(i + 1) * blk, :] = y.astype(BF16).reshape(GRID_W, blk, tn)


def _fft_constants(seq):
    n1 = seq // GRID_W
    blk = V7X_BF16_ROWS
    k1 = np.arange(n1)[None, :, None]
    l1 = np.arange(n1)[None, None, :]
    l2v = np.arange(GRID_W)[:, None, None]
    ang = 2.0 * np.pi * ((k1 * (GRID_W * l1 + l2v)) % seq) / seq
    dc, ds = np.cos(ang), np.sin(ang)
    d1 = np.concatenate([np.concatenate([dc, -ds], axis=2),
                         np.concatenate([-ds, -dc], axis=2)], axis=1)
    k2 = np.arange(GRID_W)[:, None]
    l2m = np.arange(GRID_W)[None, :]
    ang3 = 2.0 * np.pi * ((k2 * l2m) % GRID_W) / GRID_W
    eye = np.eye(blk)
    m3c = np.kron(np.cos(ang3), eye)
    m3s = np.kron(np.sin(ang3), eye)
    as_bf16 = lambda v: jnp.asarray(v, F32).astype(BF16)
    return as_bf16(d1), as_bf16(m3c), as_bf16(m3s)


def _fft_call(a_w, b_w):
    seq, nw = a_w.shape
    n1 = seq // GRID_W
    tn = min(FFT_COLS, nw)
    d1, m3c, m3s = _fft_constants(seq)
    scale = 1.0 / math.sqrt(seq * F_GROUP_DIM)
    const = lambda shape: pl.BlockSpec(shape, lambda j: (0,) * len(shape),
                                       pipeline_mode=pl.Buffered(1))
    strip = pl.BlockSpec((seq, tn), lambda j: (0, j))
    out = pl.pallas_call(
        functools.partial(_fft_kernel, n1=n1, scale=scale),
        grid=(nw // tn,),
        in_specs=[strip, strip, const(d1.shape), const(m3c.shape), const(m3s.shape)],
        out_specs=pl.BlockSpec((GRID_W, n1, tn), lambda j: (0, 0, j)),
        out_shape=jax.ShapeDtypeStruct((GRID_W, n1, nw), BF16),
        scratch_shapes=[pltpu.VMEM((GRID_W, 2 * n1, tn), BF16)],
        compiler_params=pltpu.CompilerParams(dimension_semantics=("parallel",),
                                             vmem_limit_bytes=V7X_VMEM_LIMIT),
        name="position_dft",
    )(a_w, b_w, d1, m3c, m3s)
    return out.reshape(seq, nw)


def _scan_kernel(u_ref, wg_ref, ba_ref, bx_ref, sp_ref, h0_ref, y_ref, hf_ref, a_s, b_s, h_s,
                 *, reverse, steps, heads, nb):
    @pl.when(pl.program_id(0) == 0)
    def _():
        h_s[...] = h0_ref[...]

    u = u_ref[...]
    for hd in range(heads):
        sl = slice(hd * RNN_BLOCK, (hd + 1) * RNN_BLOCK)
        ub = u[:, sl]
        g = _dot(ub, wg_ref[hd])
        tr = jnp.tanh(g[:, :RNN_BLOCK] + ba_ref[:, sl])
        ti = jnp.tanh(g[:, RNN_BLOCK:] + bx_ref[:, sl])
        neg_log_a = sp_ref[:, sl] * (1.0 + tr)
        a = jnp.exp2(neg_log_a * (-LOG2_E))
        q = (a * a + 1.0) * jnp.tanh(neg_log_a)
        mult = jnp.where(q > 0.0, q * lax.rsqrt(q), 0.0)
        a_s[:, sl] = a
        b_s[:, sl] = mult * ((1.0 + ti) * ub.astype(F32))

    def step(j, h):
        t = (steps - 1 - j) if reverse else j
        r0 = pl.multiple_of(t * nb, nb)
        h = a_s[pl.ds(r0, nb), :] * h + b_s[pl.ds(r0, nb), :]
        b_s[pl.ds(r0, nb), :] = h
        return h

    h = lax.fori_loop(0, steps, step, h_s[...], unroll=8)
    h_s[...] = h
    hf_ref[...] = h
    y_ref[...] = b_s[...].astype(y_ref.dtype)


def _scan_call(u, wg, ba, bx, sp, h0, reverse, name):
    n, c = u.shape
    nb = h0.shape[0]
    heads = wg.shape[0]
    tr = min(SCAN_ROWS, n)
    nt = n // tr
    const = lambda shape: pl.BlockSpec(shape, lambda i: (0,) * len(shape))
    tile = pl.BlockSpec((tr, c), (lambda i: (nt - 1 - i, 0)) if reverse else (lambda i: (i, 0)))
    return pl.pallas_call(
        functools.partial(_scan_kernel, reverse=reverse, steps=tr // nb, heads=heads, nb=nb),
        grid=(nt,),
        in_specs=[tile, const(wg.shape), const((1, c)), const((1, c)), const((1, c)),
                  const((nb, c))],
        out_specs=[tile, const((nb, c))],
        out_shape=[jax.ShapeDtypeStruct((n, c), BF16), jax.ShapeDtypeStruct((nb, c), F32)],
        scratch_shapes=[pltpu.VMEM((tr, c), F32), pltpu.VMEM((tr, c), F32),
                        pltpu.VMEM((nb, c), F32)],
        compiler_params=pltpu.CompilerParams(dimension_semantics=("arbitrary",),
                                             vmem_limit_bytes=V7X_VMEM_LIMIT),
        name=name,
    )(u, wg, ba, bx, sp, h0)


def _merge_kernel(x_ref, yf_ref, yb_ref, gg_ref, gf_ref, gr_ref, yw_ref, pi_ref, wf_ref,
                  wr_ref, wo_ref, g1_ref, n2_ref, sc2_ref, sh2_ref, wrt_ref, brt_ref, tri_ref,
                  x1_ref, hm_ref, ti_ref, tw_ref, rk_ref, cnt_ref, carry, *, nb, d_f):
    @pl.when(pl.program_id(0) == 0)
    def _():
        carry[...] = jnp.zeros_like(carry)

    yw = jnp.concatenate([yw_ref[:, bi * d_f:(bi + 1) * d_f] for bi in range(nb)], axis=0)
    yfm = _dot(pi_ref[...], yw).astype(BF16)
    f_out = _dot(yfm, wf_ref[...])
    y = yf_ref[...].astype(F32) + yb_ref[...].astype(F32)
    r_out = _dot((y * gg_ref[...].astype(F32)).astype(BF16), wr_ref[...])
    mix = gf_ref[...].astype(F32) * f_out + gr_ref[...].astype(F32) * r_out
    x1 = _to_time_major(x_ref[...]) + _per_batch(_dot(mix.astype(BF16), wo_ref[...]), g1_ref[...])
    x1_ref[...] = x1

    h2 = _modulate(_rms(x1, n2_ref[...]), sc2_ref[...], sh2_ref[...])
    hm_ref[...] = _pack_bf16_pairs(h2)

    hh, hl = _split_bf16(h2)
    wh, wl = _split_bf16(wrt_ref[...])
    lg = _dot_nt(wh, hh) + _dot_nt(wl, hh) + _dot_nt(wh, hl) + brt_ref[...]
    ne, tm = lg.shape
    eid = lax.broadcasted_iota(jnp.int32, (ne, tm), 0)
    row_k = lax.broadcasted_iota(jnp.int32, (TOP_K, tm), 0)
    vals, hots = [], []
    top_i = jnp.zeros((TOP_K, tm), jnp.int32)
    for k in range(TOP_K):
        m = jnp.max(lg, axis=0, keepdims=True)
        idx = jnp.min(jnp.where(lg == m, eid, ne), axis=0, keepdims=True)
        hit = eid == idx
        vals.append(m)
        hots.append(hit.astype(F32))
        top_i = jnp.where(row_k == k, idx, top_i)
        lg = jnp.where(hit, -jnp.inf, lg)
    exps = [jnp.exp(v - vals[0]) for v in vals]
    inv = 1.0 / (exps[0] + exps[1] + exps[2] + exps[3])
    ti_ref[...] = top_i
    w_pad = jnp.concatenate([e * inv for e in exps]
                            + [jnp.zeros((V7X_LANES - TOP_K, tm), F32)], axis=0)
    tw_ref[...] = jnp.transpose(w_pad)[:, :TOP_K]

    per_tok = hots[0] + hots[1] + hots[2] + hots[3]
    before = _dot(per_tok.astype(BF16), tri_ref[...]) + carry[...]
    rank = jnp.zeros((TOP_K, tm), F32)
    for k in range(TOP_K):
        rank = jnp.where(row_k == k, jnp.sum(hots[k] * before, axis=0, keepdims=True), rank)
    rk_ref[...] = rank.astype(jnp.int32)
    carry[...] = carry[...] + jnp.sum(per_tok, axis=1, keepdims=True)
    cnt_ref[...] = carry[...]


def _merge_call(x, y_f, y_b, gg, gf, gr, yw, pint, w_f, w_r, w_o, g1, n2, sc2, sh2, w_rt, b_rt,
                tri, d_f, chunk, n_chunks):
    nb, seq, d = x.shape
    n = nb * seq // n_chunks
    d_rnn = y_f.shape[1]
    ne = w_rt.shape[0]
    nt = n // ROW_TILE
    off = chunk * nt
    const = lambda shape: pl.BlockSpec(shape, lambda i: (0,) * len(shape))
    row = lambda c: pl.BlockSpec((ROW_TILE, c), lambda i: (i, 0))
    row_in = lambda c: pl.BlockSpec((ROW_TILE, c), lambda i: (i + off, 0))
    k_major = pl.BlockSpec((TOP_K, ROW_TILE), lambda i: (0, i))
    return pl.pallas_call(
        functools.partial(_merge_kernel, nb=nb, d_f=d_f),
        grid=(nt,),
        in_specs=[pl.BlockSpec((nb, GRID_W, d), lambda i: (0, i + off, 0)), row_in(d_rnn),
                  row_in(d_rnn), row_in(d_rnn), row_in(d), row_in(d),
                  pl.BlockSpec((GRID_W, nb * d_f), lambda i: (i + off, 0)),
                  const((ROW_TILE, ROW_TILE)), const((d_f, d)), const((d_rnn, d)), const((d, d)),
                  const((nb, d)), const((1, d)), const((nb, d)), const((nb, d)),
                  const((ne, d)), const((ne, 1)), const((ROW_TILE, ROW_TILE))],
        out_specs=[row(d), row(d // 2), k_major, row(TOP_K), k_major, const((ne, 1))],
        out_shape=[jax.ShapeDtypeStruct((n, d), F32), jax.ShapeDtypeStruct((n, d // 2), jnp.int32),
                   jax.ShapeDtypeStruct((TOP_K, n), jnp.int32),
                   jax.ShapeDtypeStruct((n, TOP_K), F32),
                   jax.ShapeDtypeStruct((TOP_K, n), jnp.int32),
                   jax.ShapeDtypeStruct((ne, 1), F32)],
        scratch_shapes=[pltpu.VMEM((ne, 1), F32)],
        compiler_params=pltpu.CompilerParams(dimension_semantics=("arbitrary",),
                                             vmem_limit_bytes=V7X_VMEM_LIMIT),
        name="merge_and_route",
    )(x, y_f, y_b, gg, gf, gr, yw, pint, w_f, w_r, w_o, g1, n2, sc2, sh2, w_rt, b_rt, tri)


def _sc_mesh():
    return plsc.VectorSubcoreMesh(core_axis_name="core", subcore_axis_name="subcore")


def _sc_scatter_rows(x, idx_rows, n_out):
    n, c = x.shape
    kk = len(idx_rows)
    mesh = _sc_mesh()
    workers = mesh.num_cores * mesh.num_subcores
    per_worker = n // workers
    assert per_worker % SC_WINDOW == 0

    @functools.partial(
        pl.kernel, out_type=jax.ShapeDtypeStruct((n_out, c), x.dtype), mesh=mesh,
        scratch_types=[pltpu.VMEM((SC_WINDOW, c), x.dtype)]
        + [pltpu.VMEM((SC_WINDOW,), jnp.int32)] * kk + [pltpu.SemaphoreType.DMA],
        name="sc_dispatch_scatter")
    def scatter(x_hbm, *refs):
        idx_hbm, o_hbm, rows_v = refs[:kk], refs[kk], refs[kk + 1]
        idx_v, sem = refs[kk + 2:2 * kk + 2], refs[2 * kk + 2]
        wid = lax.axis_index("subcore") * mesh.num_cores + lax.axis_index("core")

        @pl.loop(0, per_worker // SC_WINDOW)
        def _(j):
            base = pl.multiple_of(wid * per_worker + j * SC_WINDOW, SC_WINDOW)
            pltpu.sync_copy(x_hbm.at[pl.ds(base, SC_WINDOW)], rows_v)
            for ih, iv in zip(idx_hbm, idx_v):
                pltpu.sync_copy(ih.at[pl.ds(base, SC_WINDOW)], iv)
            copies = [pltpu.async_copy(rows_v, o_hbm.at[iv], sem) for iv in idx_v]
            for cp in copies:
                cp.wait()

    return scatter(x, *idx_rows)


def _sc_gather_rows(src, idx):
    n_out = idx.shape[0]
    c = src.shape[1]
    mesh = _sc_mesh()
    workers = mesh.num_cores * mesh.num_subcores
    per_worker = n_out // workers
    assert per_worker % SC_WINDOW == 0

    @functools.partial(
        pl.kernel, out_type=jax.ShapeDtypeStruct((n_out, c), src.dtype), mesh=mesh,
        scratch_types=[pltpu.VMEM((SC_WINDOW, c), src.dtype), pltpu.VMEM((SC_WINDOW,), jnp.int32),
                       pltpu.SemaphoreType.DMA],
        name="sc_combine_gather")
    def gather(src_hbm, idx_hbm, o_hbm, rows_v, idx_v, sem):
        wid = lax.axis_index("subcore") * mesh.num_cores + lax.axis_index("core")

        @pl.loop(0, per_worker // SC_WINDOW)
        def _(j):
            base = pl.multiple_of(wid * per_worker + j * SC_WINDOW, SC_WINDOW)
            pltpu.sync_copy(idx_hbm.at[pl.ds(base, SC_WINDOW)], idx_v)
            pltpu.async_copy(src_hbm.at[idx_v], rows_v, sem).wait()
            pltpu.sync_copy(rows_v, o_hbm.at[pl.ds(base, SC_WINDOW)])

    return gather(src, idx)


def _expert_kernel(be_ref, nv_ref, x_ref, wgu_ref, bgu_ref, wd_ref, bd_ref, y_ref, wgu_s, wd_s,
                   *, d_e):
    i = pl.program_id(0)
    valid = nv_ref[i]

    @pl.when((i == 0) | (be_ref[i] != be_ref[jnp.maximum(i - 1, 0)]))
    def _():
        wgu_s[...] = wgu_ref[...].astype(BF16)
        wd_s[...] = wd_ref[...].astype(BF16)

    @pl.when(valid > 0)
    def _():
        rows = lax.broadcasted_iota(jnp.int32, x_ref.shape, 0)
        x = _unpack_bf16_pairs(jnp.where(rows < valid, x_ref[...], 0)).astype(BF16)
        gu = _dot(x, wgu_s[...]) + bgu_ref[...]
        gate = jnp.minimum(gu[:, :d_e], SWIGLU_LIMIT)
        up = jnp.clip(gu[:, d_e:], -SWIGLU_LIMIT, SWIGLU_LIMIT)
        act = (up + 1.0) * gate * _sigmoid(SWIGLU_ALPHA * gate)
        y_ref[...] = _pack_bf16_pairs(_dot(act.astype(BF16), wd_s[...]) + bd_ref[...])

    @pl.when(valid <= 0)
    def _():
        y_ref[...] = jnp.zeros_like(y_ref)


def _expert_call(xs, block_expert, block_valid, w_gu, b_gu, w_down, b_down):
    p, dh = xs.shape
    ne, d, d_e2 = w_gu.shape
    d_e = d_e2 // 2
    nblk = p // MOE_ROWS
    grid_spec = pltpu.PrefetchScalarGridSpec(
        num_scalar_prefetch=2,
        grid=(nblk,),
        in_specs=[pl.BlockSpec((MOE_ROWS, dh), lambda i, be, nv: (i, 0)),
                  pl.BlockSpec((None, d, d_e2), lambda i, be, nv: (be[i], 0, 0)),
                  pl.BlockSpec((None, 1, d_e2), lambda i, be, nv: (be[i], 0, 0)),
                  pl.BlockSpec((None, d_e, d), lambda i, be, nv: (be[i], 0, 0)),
                  pl.BlockSpec((None, 1, d), lambda i, be, nv: (be[i], 0, 0))],
        out_specs=pl.BlockSpec((MOE_ROWS, dh), lambda i, be, nv: (i, 0)),
        scratch_shapes=[pltpu.VMEM((d, d_e2), BF16), pltpu.VMEM((d_e, d), BF16)],
    )
    return pl.pallas_call(
        functools.partial(_expert_kernel, d_e=d_e),
        grid_spec=grid_spec,
        out_shape=jax.ShapeDtypeStruct((p, dh), jnp.int32),
        compiler_params=pltpu.CompilerParams(dimension_semantics=("arbitrary",),
                                             vmem_limit_bytes=V7X_VMEM_LIMIT),
        name="expert_ffn",
    )(block_expert, block_valid, xs, w_gu, b_gu.reshape(ne, 1, d_e2), w_down,
      b_down.reshape(ne, 1, d))


def _combine_kernel(x1_ref, yg_ref, tw_ref, g2_ref, nf_ref, *rest):
    o_ref = rest[-1]
    w = tw_ref[...]
    moe = w[:, 0:1] * _unpack_bf16_pairs(yg_ref[0])
    for k in range(1, TOP_K):
        moe = moe + w[:, k:k + 1] * _unpack_bf16_pairs(yg_ref[k])
    out = _rms(x1_ref[...] + _per_batch(moe, g2_ref[...]), nf_ref[...])
    o_ref[...] = _to_batch_major(out, g2_ref.shape[0])


def _combine_call(x1, yg, top_w, g2, nf, chunk, n_chunks, prev_out):
    n, d = x1.shape
    nb = g2.shape[0]
    seq = n * n_chunks // nb
    nt = n // ROW_TILE
    off = chunk * nt
    const = lambda shape: pl.BlockSpec(shape, lambda i: (0,) * len(shape))
    row = lambda c: pl.BlockSpec((ROW_TILE, c), lambda i: (i, 0))
    in_specs = [row(d), pl.BlockSpec((TOP_K, ROW_TILE, d // 2), lambda i: (0, i, 0)),
                row(TOP_K), const((nb, d)), const((1, d))]
    args = [x1, yg, top_w, g2, nf]
    aliases = {}
    if prev_out is not None:
        in_specs.append(pl.BlockSpec(memory_space=pl.ANY))
        args.append(prev_out)
        aliases = {len(args) - 1: 0}
    return pl.pallas_call(
        _combine_kernel,
        grid=(nt,),
        in_specs=in_specs,
        out_specs=pl.BlockSpec((nb, GRID_W, d), lambda i: (0, i + off, 0)),
        out_shape=jax.ShapeDtypeStruct((nb, seq, d), F32),
        input_output_aliases=aliases,
        compiler_params=pltpu.CompilerParams(dimension_semantics=("parallel",),
                                             vmem_limit_bytes=V7X_VMEM_LIMIT),
        name="combine_final_norm",
    )(*args)


def _tile_constants(nb):
    t = np.arange(GRID_W)[:, None]
    b = np.arange(nb)[None, :]
    inter = np.zeros((ROW_TILE, ROW_TILE), np.float32)
    inter[(t * nb + b).ravel(), (b * GRID_W + t).ravel()] = 1.0
    c = np.arange(F_GROUP_DIM)
    ang = 2.0 * np.pi * ((c[:, None] * c[None, :]) % F_GROUP_DIM) / F_GROUP_DIM
    cs = np.concatenate([np.cos(ang), np.sin(ang)], axis=1)
    tri = np.triu(np.ones((ROW_TILE, ROW_TILE), np.float32), 1)
    as_bf16 = lambda v: jnp.asarray(v, F32).astype(BF16)
    return as_bf16(inter), as_bf16(cs), as_bf16(tri)


def kernel(x, c, ctx, c_ctx, w_ada, b_ada, norm1, w_in, conv_w, conv_b, gate_a_w, gate_a_b,
           gate_x_w, gate_x_b, lru_lambda, w_fourier, w_rnn, w_out, norm2, w_router, b_router,
           w_gu, b_gu, w_down, b_down, norm_f):
    nb, seq, d = x.shape
    depth = w_ada.shape[0]
    assert depth == 1 and nb == V7X_SUBLANES and seq % FFT_PERM_ROWS == 0
    n = nb * seq
    d_rnn = conv_w.shape[-1]
    d_f = w_fourier.shape[1]
    ne = w_router.shape[-1]
    lyr = 0
    inter, cs, tri = _tile_constants(nb)

    ctx_t = jnp.transpose(ctx, (1, 0, 2)).reshape(-1, d)

    cond = jnp.zeros((2 * V7X_SUBLANES, d), F32).at[:nb].set(c).at[nb].set(c_ctx)
    mods = _ada_call(cond, w_ada[lyr], b_ada[lyr].reshape(1, -1))
    sh1, sc1, g1, sh2, sc2, g2 = [mods[:nb, k * d:(k + 1) * d] for k in range(6)]
    csh1, csc1 = mods[nb:nb + 1, 0:d], mods[nb:nb + 1, d:2 * d]

    w_in_b = w_in[lyr].astype(BF16)
    cw, cb = conv_w[lyr], conv_b[lyr].reshape(1, d_rnn)
    n1r = norm1[lyr].reshape(1, d)
    wg = jnp.concatenate([gate_a_w[lyr], gate_x_w[lyr]], axis=-1).astype(BF16)
    sp = (0.5 * LRU_C) * jax.nn.softplus(-lru_lambda[lyr].astype(F32))
    scan = lambda u, dr, h0, name: _scan_call(
        u, wg[dr], 0.5 * gate_a_b[lyr, dr].reshape(1, -1), 0.5 * gate_x_b[lyr, dr].reshape(1, -1),
        sp[dr].reshape(1, -1), h0, bool(dr), name)

    u_ctx = _ctx_in_call(ctx_t, csc1, csh1, n1r, w_in_b[:, d_f:d_f + d_rnn], cw, cb, nb)
    h_zero = jnp.zeros((nb, d_rnn), F32)
    _, h_ctx_f = scan(u_ctx, 0, h_zero, "ctx_scan_fwd")
    _, h_ctx_b = scan(u_ctx, 1, h_zero, "ctx_scan_bwd")

    a_w, b_w, u, gg, gf, gr = _in_proj_call(x, sc1, sh1, n1r, w_in_b, cw, cb, cs, d_f,
                                            d_rnn)
    y_wide = _fft_call(a_w, b_w)
    y_f, _ = scan(u, 0, h_ctx_f, "scan_fwd")
    y_b, _ = scan(u, 1, h_ctx_b, "scan_bwd")
    merge_consts = (inter, w_fourier[lyr].astype(BF16), w_rnn[lyr].astype(BF16),
                    w_out[lyr].astype(BF16), g1, norm2[lyr].reshape(1, d), sc2, sh2,
                    w_router[lyr].T, b_router[lyr].reshape(ne, 1), tri, d_f)
    n_c = n // MOE_CHUNKS
    nblk = (n_c * TOP_K + ne * (MOE_ROWS - 1) + MOE_ROWS - 1) // MOE_ROWS
    block_start = jnp.arange(nblk, dtype=jnp.int32) * MOE_ROWS
    expert_ids = jnp.arange(ne, dtype=jnp.int32)[:, None, None]
    out = None
    for chunk in range(MOE_CHUNKS):
        x1, hmod, top_i, top_w, rank, counts = _merge_call(
            x, y_f, y_b, gg, gf, gr, y_wide, *merge_consts, chunk, MOE_CHUNKS)

        cnt = counts.reshape(ne).astype(jnp.int32)
        padded = (cnt + MOE_ROWS - 1) // MOE_ROWS * MOE_ROWS
        ends = jnp.cumsum(padded)
        starts = ends - padded
        start_of = jnp.sum(jnp.where(top_i[None] == expert_ids, starts[:, None, None], 0), axis=0)
        dest_km = start_of + rank
        block_expert = jnp.minimum(jnp.sum(block_start[:, None] >= ends[None, :], axis=1),
                                   ne - 1).astype(jnp.int32)
        block_valid = jnp.clip((starts + cnt)[block_expert] - block_start, 0, MOE_ROWS)
        block_valid = jnp.where(block_start < ends[-1], block_valid, 0).astype(jnp.int32)

        xs = _sc_scatter_rows(hmod, [dest_km[k] for k in range(TOP_K)], nblk * MOE_ROWS)
        ys = _expert_call(xs, block_expert, block_valid, w_gu[lyr], b_gu[lyr], w_down[lyr],
                          b_down[lyr])
        yg = _sc_gather_rows(ys, dest_km.reshape(TOP_K * n_c)).reshape(TOP_K, n_c, d // 2)
        out = _combine_call(x1, yg, top_w, g2, norm_f.reshape(1, d), chunk, MOE_CHUNKS, out)
    return out
```

```python
import functools
import math

import numpy as np
import jax
import jax.numpy as jnp
from jax import lax
from jax.experimental import pallas as pl
from jax.experimental.pallas import tpu as pltpu
from jax.experimental.pallas import tpu_sc as plsc

F32 = jnp.float32
BF16 = jnp.bfloat16

GRID_W = 64
CONV_W = 4
LRU_C = 8.0
TOP_K = 4
F_GROUP_DIM = 128
RNN_BLOCK = 128
SWIGLU_ALPHA = 1.702
SWIGLU_LIMIT = 7.0
EPS = 1e-6
LOG2_E = 1.4426950408889634

V7X_LANES = 128
V7X_SUBLANES = 8
V7X_BF16_ROWS = 16
V7X_VMEM_LIMIT = 56 * 1024 * 1024

ROW_TILE = GRID_W * V7X_SUBLANES
FFT_PERM_ROWS = V7X_BF16_ROWS * GRID_W
FFT_COLS = 256
SCAN_ROWS = 2048
MOE_ROWS = 512
EXPERT_ROW_STEP = 128
MOE_CHUNKS = 2
SC_WINDOW = 128
HIGH_HALF = -65536
LOW_HALF = 65535


def _dot(a, b):
    return jnp.dot(a, b, preferred_element_type=F32)


def _dot_nt(a, b):
    return lax.dot_general(a, b, (((1,), (1,)), ((), ())), preferred_element_type=F32)


def _split_bf16(a):
    hi = a.astype(BF16)
    return hi, (a - hi.astype(F32)).astype(BF16)


def _dot3(a, b):
    ah, al = _split_bf16(a)
    bh, bl = _split_bf16(b)
    return _dot(ah, bh) + _dot(ah, bl) + _dot(al, bh)


def _sigmoid(x):
    return 0.5 * (jnp.tanh(0.5 * x) + 1.0)


def _gelu_tanh(x):
    c = math.sqrt(2.0 / math.pi)
    return 0.5 * x * (1.0 + jnp.tanh(c * (x + 0.044715 * (x * x * x))))


def _pack_bf16_pairs(v):
    c = v.shape[1] // 2
    bits = pltpu.bitcast(v.astype(BF16).astype(F32), jnp.int32)
    return (bits[:, :c] & HIGH_HALF) | ((bits[:, c:] >> 16) & LOW_HALF)


def _unpack_bf16_pairs(p):
    hi = pltpu.bitcast(p & HIGH_HALF, F32)
    lo = pltpu.bitcast(p << 16, F32)
    return jnp.concatenate([hi, lo], axis=1)


def _rms(x, g):
    return x * lax.rsqrt(jnp.mean(x * x, axis=-1, keepdims=True) + EPS) * g


def _modulate(xn, sc, sh):
    if sc.shape[0] == 1:
        return xn * (1.0 + sc) + sh
    r, d = xn.shape
    nb = sc.shape[0]
    x3 = xn.reshape(r // nb, nb, d)
    return (x3 * (1.0 + sc)[None] + sh[None]).reshape(r, d)


def _to_time_major(x3):
    nb, t, d = x3.shape
    return pltpu.einshape("btd->tbd", x3).reshape(t * nb, d)


def _to_batch_major(v, nb):
    r, d = v.shape
    return pltpu.einshape("tbd->btd", v.reshape(r // nb, nb, d))


def _per_batch(v, g):
    r, d = v.shape
    nb = g.shape[0]
    return (v.reshape(r // nb, nb, d) * g[None]).reshape(r, d)


def _conv_time_major(z, cw, cb, nb):
    r, c = z.shape

    def shifted(s):
        k = abs(s) * nb
        zero = jnp.zeros((k, c), z.dtype)
        if s > 0:
            return jnp.concatenate([zero, z[:r - k]], axis=0)
        return jnp.concatenate([z[k:], zero], axis=0)

    return (cb + shifted(2) * cw[0:1] + shifted(1) * cw[1:2] + z * cw[2:3]
            + shifted(-1) * cw[3:4])


def _ada_kernel(c_ref, w_ref, b_ref, o_ref):
    cnd = c_ref[...]
    o_ref[...] = _dot3(cnd * _sigmoid(cnd), w_ref[...]) + b_ref[...]


def _ada_call(cond, w, b):
    r, d = cond.shape
    n = w.shape[1]
    return pl.pallas_call(
        _ada_kernel,
        grid=(n // d,),
        in_specs=[pl.BlockSpec((r, d), lambda j: (0, 0)),
                  pl.BlockSpec((d, d), lambda j: (0, j)),
                  pl.BlockSpec((1, d), lambda j: (0, j))],
        out_specs=pl.BlockSpec((r, d), lambda j: (0, j)),
        out_shape=jax.ShapeDtypeStruct((r, n), F32),
        name="ada_modulation",
    )(cond, w, b)


def _ctx_in_kernel(x_ref, sc_ref, sh_ref, g_ref, w_ref, cw_ref, cb_ref, u_ref, *, nb):
    h = _modulate(_rms(x_ref[...], g_ref[...]), sc_ref[...], sh_ref[...])
    z = _dot(h.astype(BF16), w_ref[...])
    u_ref[...] = (0.5 * _conv_time_major(z, cw_ref[...], cb_ref[...], nb)).astype(u_ref.dtype)


def _ctx_in_call(ctx_t, sc, sh, g, w_r, cw, cb, nb):
    r, d = ctx_t.shape
    c = w_r.shape[1]
    full = lambda shape: pl.BlockSpec(shape, lambda i: (0,) * len(shape))
    return pl.pallas_call(
        functools.partial(_ctx_in_kernel, nb=nb),
        grid=(1,),
        in_specs=[full((r, d)), full((1, d)), full((1, d)), full((1, d)), full((d, c)),
                  full((CONV_W, c)), full((1, c))],
        out_specs=full((r, c)),
        out_shape=jax.ShapeDtypeStruct((r, c), BF16),
        compiler_params=pltpu.CompilerParams(vmem_limit_bytes=V7X_VMEM_LIMIT),
        name="ctx_in_proj",
    )(ctx_t, sc, sh, g, w_r, cw, cb)


def _in_proj_kernel(x_ref, sc_ref, sh_ref, g_ref, w_ref, cw_ref, cb_ref, cs_ref,
                    a_ref, b_ref, u_ref, gg_ref, gf_ref, gr_ref, *, nb, d_f, d_rnn):
    d = x_ref.shape[-1]
    h3 = (_rms(x_ref[...], g_ref[...]) * (1.0 + sc_ref[...])[:, None, :]
          + sh_ref[...][:, None, :])
    h = _to_time_major(h3).astype(BF16)
    c_r, c_g, c_m = d_f, d_f + d_rnn, d_f + 2 * d_rnn

    zf = _dot(h3.reshape(nb * GRID_W, d).astype(BF16), w_ref[:, 0:c_r]).astype(BF16)
    for grp in range(d_f // F_GROUP_DIM):
        ab = _dot(zf[:, grp * F_GROUP_DIM:(grp + 1) * F_GROUP_DIM], cs_ref[...])
        for bi in range(nb):
            rows = slice(bi * GRID_W, (bi + 1) * GRID_W)
            col = bi * d_f + grp * F_GROUP_DIM
            a_ref[:, col:col + F_GROUP_DIM] = ab[rows, :F_GROUP_DIM].astype(BF16)
            b_ref[:, col:col + F_GROUP_DIM] = ab[rows, F_GROUP_DIM:].astype(BF16)

    zr = _dot(h, w_ref[:, c_r:c_g])
    u_ref[...] = (0.5 * _conv_time_major(zr, cw_ref[...], cb_ref[...], nb)).astype(BF16)
    gg_ref[...] = _gelu_tanh(_dot(h, w_ref[:, c_g:c_m])).astype(BF16)
    gf_ref[...] = _sigmoid(_dot(h, w_ref[:, c_m:c_m + d])).astype(BF16)
    gr_ref[...] = _sigmoid(_dot(h, w_ref[:, c_m + d:c_m + 2 * d])).astype(BF16)


def _in_proj_call(x, sc, sh, g, w_in, cw, cb, cs, d_f, d_rnn):
    nb, seq, d = x.shape
    n = nb * seq
    d_in = w_in.shape[1]
    nt = n // ROW_TILE
    const = lambda shape: pl.BlockSpec(shape, lambda i: (0,) * len(shape))
    row = lambda c: pl.BlockSpec((ROW_TILE, c), lambda i: (i, 0))
    wide = pl.BlockSpec((GRID_W, nb * d_f), lambda i: (i, 0))
    return pl.pallas_call(
        functools.partial(_in_proj_kernel, nb=nb, d_f=d_f, d_rnn=d_rnn),
        grid=(nt,),
        in_specs=[pl.BlockSpec((nb, GRID_W, d), lambda i: (0, i, 0)), const((nb, d)),
                  const((nb, d)), const((1, d)), const((d, d_in)),
                  const((CONV_W, d_rnn)), const((1, d_rnn)),
                  const((F_GROUP_DIM, 2 * F_GROUP_DIM))],
        out_specs=[wide, wide, row(d_rnn), row(d_rnn), row(d), row(d)],
        out_shape=[jax.ShapeDtypeStruct((seq, nb * d_f), BF16),
                   jax.ShapeDtypeStruct((seq, nb * d_f), BF16),
                   jax.ShapeDtypeStruct((n, d_rnn), BF16),
                   jax.ShapeDtypeStruct((n, d_rnn), BF16),
                   jax.ShapeDtypeStruct((n, d), BF16),
                   jax.ShapeDtypeStruct((n, d), BF16)],
        compiler_params=pltpu.CompilerParams(dimension_semantics=("parallel",),
                                             vmem_limit_bytes=V7X_VMEM_LIMIT),
        name="latent_in_proj",
    )(x, sc, sh, g, w_in, cw, cb, cs)


def _fft_kernel(a_ref, b_ref, d1_ref, m3c_ref, m3s_ref, o_ref, s_ref, *, n1, scale):
    tn = a_ref.shape[1]
    blk = V7X_BF16_ROWS
    for j in range(n1 // blk):
        rows = slice(j * FFT_PERM_ROWS, (j + 1) * FFT_PERM_ROWS)
        s_ref[:, j * blk:(j + 1) * blk, :] = pltpu.einshape(
            "abd->bad", a_ref[rows, :].reshape(blk, GRID_W, tn))
        s_ref[:, n1 + j * blk:n1 + (j + 1) * blk, :] = pltpu.einshape(
            "abd->bad", b_ref[rows, :].reshape(blk, GRID_W, tn))

    def stage1(l2, carry):
        s_ref[l2] = _dot(d1_ref[l2], s_ref[l2]).astype(BF16)
        return carry

    lax.fori_loop(0, GRID_W, stage1, 0)

    for i in range(n1 // blk):
        xr = s_ref[:, i * blk:(i + 1) * blk, :].reshape(FFT_PERM_ROWS, tn)
        xi = s_ref[:, n1 + i * blk:n1 + (i + 1) * blk, :].reshape(FFT_PERM_ROWS, tn)
        y = (_dot(m3c_ref[...], xr) + _dot(m3s_ref[...], xi)) * scale
        o_ref[:, i * blk:(i + 1) * blk, :] = y.astype(BF16).reshape(GRID_W, blk, tn)


def _fft_constants(seq):
    n1 = seq // GRID_W
    blk = V7X_BF16_ROWS
    k1 = np.arange(n1)[None, :, None]
    l1 = np.arange(n1)[None, None, :]
    l2v = np.arange(GRID_W)[:, None, None]
    ang = 2.0 * np.pi * ((k1 * (GRID_W * l1 + l2v)) % seq) / seq
    dc, ds = np.cos(ang), np.sin(ang)
    d1 = np.concatenate([np.concatenate([dc, -ds], axis=2),
                         np.concatenate([-ds, -dc], axis=2)], axis=1)
    k2 = np.arange(GRID_W)[:, None]
    l2m = np.arange(GRID_W)[None, :]
    ang3 = 2.0 * np.pi * ((k2 * l2m) % GRID_W) / GRID_W
    eye = np.eye(blk)
    m3c = np.kron(np.cos(ang3), eye)
    m3s = np.kron(np.sin(ang3), eye)
    as_bf16 = lambda v: jnp.asarray(v, F32).astype(BF16)
    return as_bf16(d1), as_bf16(m3c), as_bf16(m3s)


def _fft_call(a_w, b_w):
    seq, nw = a_w.shape
    n1 = seq // GRID_W
    tn = min(FFT_COLS, nw)
    d1, m3c, m3s = _fft_constants(seq)
    scale = 1.0 / math.sqrt(seq * F_GROUP_DIM)
    const = lambda shape: pl.BlockSpec(shape, lambda j: (0,) * len(shape),
                                       pipeline_mode=pl.Buffered(1))
    strip = pl.BlockSpec((seq, tn), lambda j: (0, j))
    out = pl.pallas_call(
        functools.partial(_fft_kernel, n1=n1, scale=scale),
        grid=(nw // tn,),
        in_specs=[strip, strip, const(d1.shape), const(m3c.shape), const(m3s.shape)],
        out_specs=pl.BlockSpec((GRID_W, n1, tn), lambda j: (0, 0, j)),
        out_shape=jax.ShapeDtypeStruct((GRID_W, n1, nw), BF16),
        scratch_shapes=[pltpu.VMEM((GRID_W, 2 * n1, tn), BF16)],
        compiler_params=pltpu.CompilerParams(dimension_semantics=("parallel",),
                                             vmem_limit_bytes=V7X_VMEM_LIMIT),
        name="position_dft",
    )(a_w, b_w, d1, m3c, m3s)
    return out.reshape(seq, nw)


def _scan_kernel(u_ref, wg_ref, ba_ref, bx_ref, sp_ref, h0_ref, y_ref, hf_ref, a_s, b_s, h_s,
                 *, reverse, steps, heads, nb):
    @pl.when(pl.program_id(0) == 0)
    def _():
        h_s[...] = h0_ref[...]

    u = u_ref[...]
    for hd in range(heads):
        sl = slice(hd * RNN_BLOCK, (hd + 1) * RNN_BLOCK)
        ub = u[:, sl]
        g = _dot(ub, wg_ref[hd])
        tr = jnp.tanh(g[:, :RNN_BLOCK] + ba_ref[:, sl])
        ti = jnp.tanh(g[:, RNN_BLOCK:] + bx_ref[:, sl])
        neg_log_a = sp_ref[:, sl] * (1.0 + tr)
        a = jnp.exp2(neg_log_a * (-LOG2_E))
        q = (a * a + 1.0) * jnp.tanh(neg_log_a)
        mult = jnp.where(q > 0.0, q * lax.rsqrt(q), 0.0)
        a_s[:, sl] = a
        b_s[:, sl] = mult * ((1.0 + ti) * ub.astype(F32))

    def step(j, h):
        t = (steps - 1 - j) if reverse else j
        r0 = pl.multiple_of(t * nb, nb)
        h = a_s[pl.ds(r0, nb), :] * h + b_s[pl.ds(r0, nb), :]
        b_s[pl.ds(r0, nb), :] = h
        return h

    h = lax.fori_loop(0, steps, step, h_s[...], unroll=8)
    h_s[...] = h
    hf_ref[...] = h
    y_ref[...] = b_s[...].astype(y_ref.dtype)


def _scan_call(u, wg, ba, bx, sp, h0, reverse, name):
    n, c = u.shape
    nb = h0.shape[0]
    heads = wg.shape[0]
    tr = min(SCAN_ROWS, n)
    nt = n // tr
    const = lambda shape: pl.BlockSpec(shape, lambda i: (0,) * len(shape))
    tile = pl.BlockSpec((tr, c), (lambda i: (nt - 1 - i, 0)) if reverse else (lambda i: (i, 0)))
    return pl.pallas_call(
        functools.partial(_scan_kernel, reverse=reverse, steps=tr // nb, heads=heads, nb=nb),
        grid=(nt,),
        in_specs=[tile, const(wg.shape), const((1, c)), const((1, c)), const((1, c)),
                  const((nb, c))],
        out_specs=[tile, const((nb, c))],
        out_shape=[jax.ShapeDtypeStruct((n, c), BF16), jax.ShapeDtypeStruct((nb, c), F32)],
        scratch_shapes=[pltpu.VMEM((tr, c), F32), pltpu.VMEM((tr, c), F32),
                        pltpu.VMEM((nb, c), F32)],
        compiler_params=pltpu.CompilerParams(dimension_semantics=("arbitrary",),
                                             vmem_limit_bytes=V7X_VMEM_LIMIT),
        name=name,
    )(u, wg, ba, bx, sp, h0)


def _merge_kernel(x_ref, yf_ref, yb_ref, gg_ref, gf_ref, gr_ref, yw_ref, pi_ref, wf_ref,
                  wr_ref, wo_ref, g1_ref, n2_ref, sc2_ref, sh2_ref, wrt_ref, brt_ref, tri_ref,
                  x1_ref, hm_ref, ti_ref, tw_ref, rk_ref, cnt_ref, carry, *, nb, d_f):
    @pl.when(pl.program_id(0) == 0)
    def _():
        carry[...] = jnp.zeros_like(carry)

    yw = jnp.concatenate([yw_ref[:, bi * d_f:(bi + 1) * d_f] for bi in range(nb)], axis=0)
    yfm = _dot(pi_ref[...], yw).astype(BF16)
    f_out = _dot(yfm, wf_ref[...])
    y = yf_ref[...].astype(F32) + yb_ref[...].astype(F32)
    r_out = _dot((y * gg_ref[...].astype(F32)).astype(BF16), wr_ref[...])
    mix = gf_ref[...].astype(F32) * f_out + gr_ref[...].astype(F32) * r_out
    x1 = _to_time_major(x_ref[...]) + _per_batch(_dot(mix.astype(BF16), wo_ref[...]), g1_ref[...])
    x1_ref[...] = x1

    h2 = _modulate(_rms(x1, n2_ref[...]), sc2_ref[...], sh2_ref[...])
    hm_ref[...] = _pack_bf16_pairs(h2)

    hh, hl = _split_bf16(h2)
    wh, wl = _split_bf16(wrt_ref[...])
    lg = _dot_nt(wh, hh) + _dot_nt(wl, hh) + _dot_nt(wh, hl) + brt_ref[...]
    ne, tm = lg.shape
    eid = lax.broadcasted_iota(jnp.int32, (ne, tm), 0)
    row_k = lax.broadcasted_iota(jnp.int32, (TOP_K, tm), 0)
    vals, hots = [], []
    top_i = jnp.zeros((TOP_K, tm), jnp.int32)
    for k in range(TOP_K):
        m = jnp.max(lg, axis=0, keepdims=True)
        idx = jnp.min(jnp.where(lg == m, eid, ne), axis=0, keepdims=True)
        hit = eid == idx
        vals.append(m)
        hots.append(hit.astype(F32))
        top_i = jnp.where(row_k == k, idx, top_i)
        lg = jnp.where(hit, -jnp.inf, lg)
    exps = [jnp.exp(v - vals[0]) for v in vals]
    inv = 1.0 / (exps[0] + exps[1] + exps[2] + exps[3])
    ti_ref[...] = top_i
    w_pad = jnp.concatenate([e * inv for e in exps]
                            + [jnp.zeros((V7X_LANES - TOP_K, tm), F32)], axis=0)
    tw_ref[...] = jnp.transpose(w_pad)[:, :TOP_K]

    per_tok = hots[0] + hots[1] + hots[2] + hots[3]
    before = _dot(per_tok.astype(BF16), tri_ref[...]) + carry[...]
    rank = jnp.zeros((TOP_K, tm), F32)
    for k in range(TOP_K):
        rank = jnp.where(row_k == k, jnp.sum(hots[k] * before, axis=0, keepdims=True), rank)
    rk_ref[...] = rank.astype(jnp.int32)
    carry[...] = carry[...] + jnp.sum(per_tok, axis=1, keepdims=True)
    cnt_ref[...] = carry[...]


def _merge_call(x, y_f, y_b, gg, gf, gr, yw, pint, w_f, w_r, w_o, g1, n2, sc2, sh2, w_rt, b_rt,
                tri, d_f, chunk, n_chunks):
    nb, seq, d = x.shape
    n = nb * seq // n_chunks
    d_rnn = y_f.shape[1]
    ne = w_rt.shape[0]
    nt = n // ROW_TILE
    off = chunk * nt
    const = lambda shape: pl.BlockSpec(shape, lambda i: (0,) * len(shape))
    row = lambda c: pl.BlockSpec((ROW_TILE, c), lambda i: (i, 0))
    row_in = lambda c: pl.BlockSpec((ROW_TILE, c), lambda i: (i + off, 0))
    k_major = pl.BlockSpec((TOP_K, ROW_TILE), lambda i: (0, i))
    return pl.pallas_call(
        functools.partial(_merge_kernel, nb=nb, d_f=d_f),
        grid=(nt,),
        in_specs=[pl.BlockSpec((nb, GRID_W, d), lambda i: (0, i + off, 0)), row_in(d_rnn),
                  row_in(d_rnn), row_in(d_rnn), row_in(d), row_in(d),
                  pl.BlockSpec((GRID_W, nb * d_f), lambda i: (i + off, 0)),
                  const((ROW_TILE, ROW_TILE)), const((d_f, d)), const((d_rnn, d)), const((d, d)),
                  const((nb, d)), const((1, d)), const((nb, d)), const((nb, d)),
                  const((ne, d)), const((ne, 1)), const((ROW_TILE, ROW_TILE))],
        out_specs=[row(d), row(d // 2), k_major, row(TOP_K), k_major, const((ne, 1))],
        out_shape=[jax.ShapeDtypeStruct((n, d), F32), jax.ShapeDtypeStruct((n, d // 2), jnp.int32),
                   jax.ShapeDtypeStruct((TOP_K, n), jnp.int32),
                   jax.ShapeDtypeStruct((n, TOP_K), F32),
                   jax.ShapeDtypeStruct((TOP_K, n), jnp.int32),
                   jax.ShapeDtypeStruct((ne, 1), F32)],
        scratch_shapes=[pltpu.VMEM((ne, 1), F32)],
        compiler_params=pltpu.CompilerParams(dimension_semantics=("arbitrary",),
                                             vmem_limit_bytes=V7X_VMEM_LIMIT),
        name="merge_and_route",
    )(x, y_f, y_b, gg, gf, gr, yw, pint, w_f, w_r, w_o, g1, n2, sc2, sh2, w_rt, b_rt, tri)


def _sc_mesh():
    return plsc.VectorSubcoreMesh(core_axis_name="core", subcore_axis_name="subcore")


def _sc_scatter_rows(x, idx_rows, n_out):
    n, c = x.shape
    kk = len(idx_rows)
    mesh = _sc_mesh()
    workers = mesh.num_cores * mesh.num_subcores
    per_worker = n // workers
    assert per_worker % SC_WINDOW == 0

    @functools.partial(
        pl.kernel, out_type=jax.ShapeDtypeStruct((n_out, c), x.dtype), mesh=mesh,
        scratch_types=[pltpu.VMEM((SC_WINDOW, c), x.dtype)]
        + [pltpu.VMEM((SC_WINDOW,), jnp.int32)] * kk + [pltpu.SemaphoreType.DMA],
        name="sc_dispatch_scatter")
    def scatter(x_hbm, *refs):
        idx_hbm, o_hbm, rows_v = refs[:kk], refs[kk], refs[kk + 1]
        idx_v, sem = refs[kk + 2:2 * kk + 2], refs[2 * kk + 2]
        wid = lax.axis_index("subcore") * mesh.num_cores + lax.axis_index("core")

        @pl.loop(0, per_worker // SC_WINDOW)
        def _(j):
            base = pl.multiple_of(wid * per_worker + j * SC_WINDOW, SC_WINDOW)
            pltpu.sync_copy(x_hbm.at[pl.ds(base, SC_WINDOW)], rows_v)
            for ih, iv in zip(idx_hbm, idx_v):
                pltpu.sync_copy(ih.at[pl.ds(base, SC_WINDOW)], iv)
            copies = [pltpu.async_copy(rows_v, o_hbm.at[iv], sem) for iv in idx_v]
            for cp in copies:
                cp.wait()

    return scatter(x, *idx_rows)


def _sc_gather_rows(src, idx):
    n_out = idx.shape[0]
    c = src.shape[1]
    mesh = _sc_mesh()
    workers = mesh.num_cores * mesh.num_subcores
    per_worker = n_out // workers
    assert per_worker % SC_WINDOW == 0

    @functools.partial(
        pl.kernel, out_type=jax.ShapeDtypeStruct((n_out, c), src.dtype), mesh=mesh,
        scratch_types=[pltpu.VMEM((SC_WINDOW, c), src.dtype), pltpu.VMEM((SC_WINDOW,), jnp.int32),
                       pltpu.SemaphoreType.DMA],
        name="sc_combine_gather")
    def gather(src_hbm, idx_hbm, o_hbm, rows_v, idx_v, sem):
        wid = lax.axis_index("subcore") * mesh.num_cores + lax.axis_index("core")

        @pl.loop(0, per_worker // SC_WINDOW)
        def _(j):
            base = pl.multiple_of(wid * per_worker + j * SC_WINDOW, SC_WINDOW)
            pltpu.sync_copy(idx_hbm.at[pl.ds(base, SC_WINDOW)], idx_v)
            pltpu.async_copy(src_hbm.at[idx_v], rows_v, sem).wait()
            pltpu.sync_copy(rows_v, o_hbm.at[pl.ds(base, SC_WINDOW)])

    return gather(src, idx)


def _expert_kernel(be_ref, nv_ref, x_ref, wgu_ref, bgu_ref, wd_ref, bd_ref, y_ref, wgu_s, wd_s,
                   *, d_e):
    i = pl.program_id(0)
    valid = nv_ref[i]

    @pl.when((i == 0) | (be_ref[i] != be_ref[jnp.maximum(i - 1, 0)]))
    def _():
        wgu_s[...] = wgu_ref[...].astype(BF16)
        wd_s[...] = wd_ref[...].astype(BF16)

    def ffn(nrows):
        rows = lax.broadcasted_iota(jnp.int32, (nrows, x_ref.shape[1]), 0)
        x = _unpack_bf16_pairs(jnp.where(rows < valid, x_ref[0:nrows, :], 0)).astype(BF16)
        gu = _dot(x, wgu_s[...]) + bgu_ref[...]
        gate = jnp.minimum(gu[:, :d_e], SWIGLU_LIMIT)
        up = jnp.clip(gu[:, d_e:], -SWIGLU_LIMIT, SWIGLU_LIMIT)
        act = (up + 1.0) * gate * _sigmoid(SWIGLU_ALPHA * gate)
        y_ref[0:nrows, :] = _pack_bf16_pairs(_dot(act.astype(BF16), wd_s[...]) + bd_ref[...])
        if nrows < MOE_ROWS:
            y_ref[nrows:, :] = jnp.zeros((MOE_ROWS - nrows, y_ref.shape[1]), y_ref.dtype)

    for nrows in range(EXPERT_ROW_STEP, MOE_ROWS + 1, EXPERT_ROW_STEP):
        pl.when((valid > nrows - EXPERT_ROW_STEP) & (valid <= nrows))(
            functools.partial(ffn, nrows))

    @pl.when(valid <= 0)
    def _():
        y_ref[...] = jnp.zeros_like(y_ref)


def _expert_call(xs, block_expert, block_valid, w_gu, b_gu, w_down, b_down):
    p, dh = xs.shape
    ne, d, d_e2 = w_gu.shape
    d_e = d_e2 // 2
    nblk = p // MOE_ROWS
    grid_spec = pltpu.PrefetchScalarGridSpec(
        num_scalar_prefetch=2,
        grid=(nblk,),
        in_specs=[pl.BlockSpec((MOE_ROWS, dh), lambda i, be, nv: (i, 0)),
                  pl.BlockSpec((None, d, d_e2), lambda i, be, nv: (be[i], 0, 0)),
                  pl.BlockSpec((None, 1, d_e2), lambda i, be, nv: (be[i], 0, 0)),
                  pl.BlockSpec((None, d_e, d), lambda i, be, nv: (be[i], 0, 0)),
                  pl.BlockSpec((None, 1, d), lambda i, be, nv: (be[i], 0, 0))],
        out_specs=pl.BlockSpec((MOE_ROWS, dh), lambda i, be, nv: (i, 0)),
        scratch_shapes=[pltpu.VMEM((d, d_e2), BF16), pltpu.VMEM((d_e, d), BF16)],
    )
    return pl.pallas_call(
        functools.partial(_expert_kernel, d_e=d_e),
        grid_spec=grid_spec,
        out_shape=jax.ShapeDtypeStruct((p, dh), jnp.int32),
        compiler_params=pltpu.CompilerParams(dimension_semantics=("arbitrary",),
                                             vmem_limit_bytes=V7X_VMEM_LIMIT),
        name="expert_ffn",
    )(block_expert, block_valid, xs, w_gu, b_gu.reshape(ne, 1, d_e2), w_down,
      b_down.reshape(ne, 1, d))


def _combine_kernel(x1_ref, yg_ref, tw_ref, g2_ref, nf_ref, *rest):
    o_ref = rest[-1]
    w = tw_ref[...]
    moe = w[:, 0:1] * _unpack_bf16_pairs(yg_ref[0])
    for k in range(1, TOP_K):
        moe = moe + w[:, k:k + 1] * _unpack_bf16_pairs(yg_ref[k])
    out = _rms(x1_ref[...] + _per_batch(moe, g2_ref[...]), nf_ref[...])
    o_ref[...] = _to_batch_major(out, g2_ref.shape[0])


def _combine_call(x1, yg, top_w, g2, nf, chunk, n_chunks, prev_out):
    n, d = x1.shape
    nb = g2.shape[0]
    seq = n * n_chunks // nb
    nt = n // ROW_TILE
    off = chunk * nt
    const = lambda shape: pl.BlockSpec(shape, lambda i: (0,) * len(shape))
    row = lambda c: pl.BlockSpec((ROW_TILE, c), lambda i: (i, 0))
    in_specs = [row(d), pl.BlockSpec((TOP_K, ROW_TILE, d // 2), lambda i: (0, i, 0)),
                row(TOP_K), const((nb, d)), const((1, d))]
    args = [x1, yg, top_w, g2, nf]
    aliases = {}
    if prev_out is not None:
        in_specs.append(pl.BlockSpec(memory_space=pl.ANY))
        args.append(prev_out)
        aliases = {len(args) - 1: 0}
    return pl.pallas_call(
        _combine_kernel,
        grid=(nt,),
        in_specs=in_specs,
        out_specs=pl.BlockSpec((nb, GRID_W, d), lambda i: (0, i + off, 0)),
        out_shape=jax.ShapeDtypeStruct((nb, seq, d), F32),
        input_output_aliases=aliases,
        compiler_params=pltpu.CompilerParams(dimension_semantics=("parallel",),
                                             vmem_limit_bytes=V7X_VMEM_LIMIT),
        name="combine_final_norm",
    )(*args)


def _tile_constants(nb):
    t = np.arange(GRID_W)[:, None]
    b = np.arange(nb)[None, :]
    inter = np.zeros((ROW_TILE, ROW_TILE), np.float32)
    inter[(t * nb + b).ravel(), (b * GRID_W + t).ravel()] = 1.0
    c = np.arange(F_GROUP_DIM)
    ang = 2.0 * np.pi * ((c[:, None] * c[None, :]) % F_GROUP_DIM) / F_GROUP_DIM
    cs = np.concatenate([np.cos(ang), np.sin(ang)], axis=1)
    tri = np.triu(np.ones((ROW_TILE, ROW_TILE), np.float32), 1)
    as_bf16 = lambda v: jnp.asarray(v, F32).astype(BF16)
    return as_bf16(inter), as_bf16(cs), as_bf16(tri)


def kernel(x, c, ctx, c_ctx, w_ada, b_ada, norm1, w_in, conv_w, conv_b, gate_a_w, gate_a_b,
           gate_x_w, gate_x_b, lru_lambda, w_fourier, w_rnn, w_out, norm2, w_router, b_router,
           w_gu, b_gu, w_down, b_down, norm_f):
    nb, seq, d = x.shape
    depth = w_ada.shape[0]
    assert depth == 1 and nb == V7X_SUBLANES and seq % FFT_PERM_ROWS == 0
    n = nb * seq
    d_rnn = conv_w.shape[-1]
    d_f = w_fourier.shape[1]
    ne = w_router.shape[-1]
    lyr = 0
    inter, cs, tri = _tile_constants(nb)

    ctx_t = jnp.transpose(ctx, (1, 0, 2)).reshape(-1, d)

    cond = jnp.zeros((2 * V7X_SUBLANES, d), F32).at[:nb].set(c).at[nb].set(c_ctx)
    mods = _ada_call(cond, w_ada[lyr], b_ada[lyr].reshape(1, -1))
    sh1, sc1, g1, sh2, sc2, g2 = [mods[:nb, k * d:(k + 1) * d] for k in range(6)]
    csh1, csc1 = mods[nb:nb + 1, 0:d], mods[nb:nb + 1, d:2 * d]

    w_in_b = w_in[lyr].astype(BF16)
    cw, cb = conv_w[lyr], conv_b[lyr].reshape(1, d_rnn)
    n1r = norm1[lyr].reshape(1, d)
    wg = jnp.concatenate([gate_a_w[lyr], gate_x_w[lyr]], axis=-1).astype(BF16)
    sp = (0.5 * LRU_C) * jax.nn.softplus(-lru_lambda[lyr].astype(F32))
    scan = lambda u, dr, h0, name: _scan_call(
        u, wg[dr], 0.5 * gate_a_b[lyr, dr].reshape(1, -1), 0.5 * gate_x_b[lyr, dr].reshape(1, -1),
        sp[dr].reshape(1, -1), h0, bool(dr), name)

    u_ctx = _ctx_in_call(ctx_t, csc1, csh1, n1r, w_in_b[:, d_f:d_f + d_rnn], cw, cb, nb)
    h_zero = jnp.zeros((nb, d_rnn), F32)
    _, h_ctx_f = scan(u_ctx, 0, h_zero, "ctx_scan_fwd")
    _, h_ctx_b = scan(u_ctx, 1, h_zero, "ctx_scan_bwd")

    a_w, b_w, u, gg, gf, gr = _in_proj_call(x, sc1, sh1, n1r, w_in_b, cw, cb, cs, d_f,
                                            d_rnn)
    y_wide = _fft_call(a_w, b_w)
    y_f, _ = scan(u, 0, h_ctx_f, "scan_fwd")
    y_b, _ = scan(u, 1, h_ctx_b, "scan_bwd")
    merge_consts = (inter, w_fourier[lyr].astype(BF16), w_rnn[lyr].astype(BF16),
                    w_out[lyr].astype(BF16), g1, norm2[lyr].reshape(1, d), sc2, sh2,
                    w_router[lyr].T, b_router[lyr].reshape(ne, 1), tri, d_f)
    n_c = n // MOE_CHUNKS
    nblk = (n_c * TOP_K + ne * (MOE_ROWS - 1) + MOE_ROWS - 1) // MOE_ROWS
    block_start = jnp.arange(nblk, dtype=jnp.int32) * MOE_ROWS
    expert_ids = jnp.arange(ne, dtype=jnp.int32)[:, None, None]
    out = None
    for chunk in range(MOE_CHUNKS):
        x1, hmod, top_i, top_w, rank, counts = _merge_call(
            x, y_f, y_b, gg, gf, gr, y_wide, *merge_consts, chunk, MOE_CHUNKS)

        cnt = counts.reshape(ne).astype(jnp.int32)
        padded = (cnt + MOE_ROWS - 1) // MOE_ROWS * MOE_ROWS
        ends = jnp.cumsum(padded)
        starts = ends - padded
        start_of = jnp.sum(jnp.where(top_i[None] == expert_ids, starts[:, None, None], 0), axis=0)
        dest_km = start_of + rank
        block_expert = jnp.minimum(jnp.sum(block_start[:, None] >= ends[None, :], axis=1),
                                   ne - 1).astype(jnp.int32)
        block_valid = jnp.clip((starts + cnt)[block_expert] - block_start, 0, MOE_ROWS)
        block_valid = jnp.where(block_start < ends[-1], block_valid, 0).astype(jnp.int32)

        xs = _sc_scatter_rows(hmod, [dest_km[k] for k in range(TOP_K)], nblk * MOE_ROWS)
        ys = _expert_call(xs, block_expert, block_valid, w_gu[lyr], b_gu[lyr], w_down[lyr],
                          b_down[lyr])
        yg = _sc_gather_rows(ys, dest_km.reshape(TOP_K * n_c)).reshape(TOP_K, n_c, d // 2)
        out = _combine_call(x1, yg, top_w, g2, norm_f.reshape(1, d), chunk, MOE_CHUNKS, out)
    return out
```

```python
import functools
import math

import numpy as np
import jax
import jax.numpy as jnp
from jax import lax
from jax.experimental import pallas as pl
from jax.experimental.pallas import tpu as pltpu
from jax.experimental.pallas import tpu_sc as plsc

F32 = jnp.float32
BF16 = jnp.bfloat16

GRID_W = 64
CONV_W = 4
LRU_C = 8.0
TOP_K = 4
F_GROUP_DIM = 128
RNN_BLOCK = 128
SWIGLU_ALPHA = 1.702
SWIGLU_LIMIT = 7.0
EPS = 1e-6
LOG2_E = 1.4426950408889634

V7X_LANES = 128
V7X_SUBLANES = 8
V7X_BF16_ROWS = 16
V7X_VMEM_LIMIT = 56 * 1024 * 1024

ROW_TILE = GRID_W * V7X_SUBLANES
FFT_PERM_ROWS = V7X_BF16_ROWS * GRID_W
FFT_COLS = 256
SCAN_ROWS = 2048
MOE_ROWS = 512
MOE_CHUNK_SHARES = (5, 3)
SC_WINDOW = 128
HIGH_HALF = -65536
LOW_HALF = 65535


def _dot(a, b):
    return jnp.dot(a, b, preferred_element_type=F32)


def _dot_nt(a, b):
    return lax.dot_general(a, b, (((1,), (1,)), ((), ())), preferred_element_type=F32)


def _split_bf16(a):
    hi = a.astype(BF16)
    return hi, (a - hi.astype(F32)).astype(BF16)


def _dot3(a, b):
    ah, al = _split_bf16(a)
    bh, bl = _split_bf16(b)
    return _dot(ah, bh) + _dot(ah, bl) + _dot(al, bh)


def _sigmoid(x):
    return 0.5 * (jnp.tanh(0.5 * x) + 1.0)


def _gelu_tanh(x):
    c = math.sqrt(2.0 / math.pi)
    return 0.5 * x * (1.0 + jnp.tanh(c * (x + 0.044715 * (x * x * x))))


def _pack_bf16_pairs(v):
    c = v.shape[1] // 2
    bits = pltpu.bitcast(v.astype(BF16).astype(F32), jnp.int32)
    return (bits[:, :c] & HIGH_HALF) | ((bits[:, c:] >> 16) & LOW_HALF)


def _unpack_bf16_pairs(p):
    hi = pltpu.bitcast(p & HIGH_HALF, F32)
    lo = pltpu.bitcast(p << 16, F32)
    return jnp.concatenate([hi, lo], axis=1)


def _rms(x, g):
    return x * lax.rsqrt(jnp.mean(x * x, axis=-1, keepdims=True) + EPS) * g


def _modulate(xn, sc, sh):
    if sc.shape[0] == 1:
        return xn * (1.0 + sc) + sh
    r, d = xn.shape
    nb = sc.shape[0]
    x3 = xn.reshape(r // nb, nb, d)
    return (x3 * (1.0 + sc)[None] + sh[None]).reshape(r, d)


def _to_time_major(x3):
    nb, t, d = x3.shape
    return pltpu.einshape("btd->tbd", x3).reshape(t * nb, d)


def _to_batch_major(v, nb):
    r, d = v.shape
    return pltpu.einshape("tbd->btd", v.reshape(r // nb, nb, d))


def _per_batch(v, g):
    r, d = v.shape
    nb = g.shape[0]
    return (v.reshape(r // nb, nb, d) * g[None]).reshape(r, d)


def _conv_time_major(z, cw, cb, nb):
    r, c = z.shape

    def shifted(s):
        k = abs(s) * nb
        zero = jnp.zeros((k, c), z.dtype)
        if s > 0:
            return jnp.concatenate([zero, z[:r - k]], axis=0)
        return jnp.concatenate([z[k:], zero], axis=0)

    return (cb + shifted(2) * cw[0:1] + shifted(1) * cw[1:2] + z * cw[2:3]
            + shifted(-1) * cw[3:4])


def _ada_kernel(c_ref, w_ref, b_ref, o_ref):
    cnd = c_ref[...]
    o_ref[...] = _dot3(cnd * _sigmoid(cnd), w_ref[...]) + b_ref[...]


def _ada_call(cond, w, b):
    r, d = cond.shape
    n = w.shape[1]
    return pl.pallas_call(
        _ada_kernel,
        grid=(n // d,),
        in_specs=[pl.BlockSpec((r, d), lambda j: (0, 0)),
                  pl.BlockSpec((d, d), lambda j: (0, j)),
                  pl.BlockSpec((1, d), lambda j: (0, j))],
        out_specs=pl.BlockSpec((r, d), lambda j: (0, j)),
        out_shape=jax.ShapeDtypeStruct((r, n), F32),
        name="ada_modulation",
    )(cond, w, b)


def _ctx_in_kernel(x_ref, sc_ref, sh_ref, g_ref, w_ref, cw_ref, cb_ref, u_ref, *, nb):
    h = _modulate(_rms(x_ref[...], g_ref[...]), sc_ref[...], sh_ref[...])
    z = _dot(h.astype(BF16), w_ref[...])
    u_ref[...] = (0.5 * _conv_time_major(z, cw_ref[...], cb_ref[...], nb)).astype(u_ref.dtype)


def _ctx_in_call(ctx_t, sc, sh, g, w_r, cw, cb, nb):
    r, d = ctx_t.shape
    c = w_r.shape[1]
    full = lambda shape: pl.BlockSpec(shape, lambda i: (0,) * len(shape))
    return pl.pallas_call(
        functools.partial(_ctx_in_kernel, nb=nb),
        grid=(1,),
        in_specs=[full((r, d)), full((1, d)), full((1, d)), full((1, d)), full((d, c)),
                  full((CONV_W, c)), full((1, c))],
        out_specs=full((r, c)),
        out_shape=jax.ShapeDtypeStruct((r, c), BF16),
        compiler_params=pltpu.CompilerParams(vmem_limit_bytes=V7X_VMEM_LIMIT),
        name="ctx_in_proj",
    )(ctx_t, sc, sh, g, w_r, cw, cb)


def _in_proj_kernel(x_ref, sc_ref, sh_ref, g_ref, w_ref, cw_ref, cb_ref, cs_ref,
                    a_ref, b_ref, u_ref, gg_ref, gf_ref, gr_ref, *, nb, d_f, d_rnn):
    d = x_ref.shape[-1]
    h3 = (_rms(x_ref[...], g_ref[...]) * (1.0 + sc_ref[...])[:, None, :]
          + sh_ref[...][:, None, :])
    h = _to_time_major(h3).astype(BF16)
    c_r, c_g, c_m = d_f, d_f + d_rnn, d_f + 2 * d_rnn

    zf = _dot(h3.reshape(nb * GRID_W, d).astype(BF16), w_ref[:, 0:c_r]).astype(BF16)
    for grp in range(d_f // F_GROUP_DIM):
        ab = _dot(zf[:, grp * F_GROUP_DIM:(grp + 1) * F_GROUP_DIM], cs_ref[...])
        for bi in range(nb):
            rows = slice(bi * GRID_W, (bi + 1) * GRID_W)
            col = bi * d_f + grp * F_GROUP_DIM
            a_ref[:, col:col + F_GROUP_DIM] = ab[rows, :F_GROUP_DIM].astype(BF16)
            b_ref[:, col:col + F_GROUP_DIM] = ab[rows, F_GROUP_DIM:].astype(BF16)

    zr = _dot(h, w_ref[:, c_r:c_g])
    u_ref[...] = (0.5 * _conv_time_major(zr, cw_ref[...], cb_ref[...], nb)).astype(BF16)
    gg_ref[...] = _gelu_tanh(_dot(h, w_ref[:, c_g:c_m])).astype(BF16)
    gf_ref[...] = _sigmoid(_dot(h, w_ref[:, c_m:c_m + d])).astype(BF16)
    gr_ref[...] = _sigmoid(_dot(h, w_ref[:, c_m + d:c_m + 2 * d])).astype(BF16)


def _in_proj_call(x, sc, sh, g, w_in, cw, cb, cs, d_f, d_rnn):
    nb, seq, d = x.shape
    n = nb * seq
    d_in = w_in.shape[1]
    nt = n // ROW_TILE
    const = lambda shape: pl.BlockSpec(shape, lambda i: (0,) * len(shape))
    row = lambda c: pl.BlockSpec((ROW_TILE, c), lambda i: (i, 0))
    wide = pl.BlockSpec((GRID_W, nb * d_f), lambda i: (i, 0))
    return pl.pallas_call(
        functools.partial(_in_proj_kernel, nb=nb, d_f=d_f, d_rnn=d_rnn),
        grid=(nt,),
        in_specs=[pl.BlockSpec((nb, GRID_W, d), lambda i: (0, i, 0)), const((nb, d)),
                  const((nb, d)), const((1, d)), const((d, d_in)),
                  const((CONV_W, d_rnn)), const((1, d_rnn)),
                  const((F_GROUP_DIM, 2 * F_GROUP_DIM))],
        out_specs=[wide, wide, row(d_rnn), row(d_rnn), row(d), row(d)],
        out_shape=[jax.ShapeDtypeStruct((seq, nb * d_f), BF16),
                   jax.ShapeDtypeStruct((seq, nb * d_f), BF16),
                   jax.ShapeDtypeStruct((n, d_rnn), BF16),
                   jax.ShapeDtypeStruct((n, d_rnn), BF16),
                   jax.ShapeDtypeStruct((n, d), BF16),
                   jax.ShapeDtypeStruct((n, d), BF16)],
        compiler_params=pltpu.CompilerParams(dimension_semantics=("parallel",),
                                             vmem_limit_bytes=V7X_VMEM_LIMIT),
        name="latent_in_proj",
    )(x, sc, sh, g, w_in, cw, cb, cs)


def _fft_kernel(a_ref, b_ref, d1_ref, m3c_ref, m3s_ref, o_ref, s_ref, *, n1, scale):
    tn = a_ref.shape[1]
    blk = V7X_BF16_ROWS
    for j in range(n1 // blk):
        rows = slice(j * FFT_PERM_ROWS, (j + 1) * FFT_PERM_ROWS)
        s_ref[:, j * blk:(j + 1) * blk, :] = pltpu.einshape(
            "abd->bad", a_ref[rows, :].reshape(blk, GRID_W, tn))
        s_ref[:, n1 + j * blk:n1 + (j + 1) * blk, :] = pltpu.einshape(
            "abd->bad", b_ref[rows, :].reshape(blk, GRID_W, tn))

    def stage1(l2, carry):
        s_ref[l2] = _dot(d1_ref[l2], s_ref[l2]).astype(BF16)
        return carry

    lax.fori_loop(0, GRID_W, stage1, 0)

    for i in range(n1 // blk):
        xr = s_ref[:, i * blk:(i + 1) * blk, :].reshape(FFT_PERM_ROWS, tn)
        xi = s_ref[:, n1 + i * blk:n1 + (i + 1) * blk, :].reshape(FFT_PERM_ROWS, tn)
        y = (_dot(m3c_ref[...], xr) + _dot(m3s_ref[...], xi)) * scale
        o_ref[:, i * blk:(i + 1) * blk, :] = y.astype(BF16).reshape(GRID_W, blk, tn)


def _fft_constants(seq):
    n1 = seq // GRID_W
    blk = V7X_BF16_ROWS
    k1 = np.arange(n1)[None, :, None]
    l1 = np.arange(n1)[None, None, :]
    l2v = np.arange(GRID_W)[:, None, None]
    ang = 2.0 * np.pi * ((k1 * (GRID_W * l1 + l2v)) % seq) / seq
    dc, ds = np.cos(ang), np.sin(ang)
    d1 = np.concatenate([np.concatenate([dc, -ds], axis=2),
                         np.concatenate([-ds, -dc], axis=2)], axis=1)
    k2 = np.arange(GRID_W)[:, None]
    l2m = np.arange(GRID_W)[None, :]
    ang3 = 2.0 * np.pi * ((k2 * l2m) % GRID_W) / GRID_W
    eye = np.eye(blk)
    m3c = np.kron(np.cos(ang3), eye)
    m3s = np.kron(np.sin(ang3), eye)
    as_bf16 = lambda v: jnp.asarray(v, F32).astype(BF16)
    return as_bf16(d1), as_bf16(m3c), as_bf16(m3s)


def _fft_call(a_w, b_w):
    seq, nw = a_w.shape
    n1 = seq // GRID_W
    tn = min(FFT_COLS, nw)
    d1, m3c, m3s = _fft_constants(seq)
    scale = 1.0 / math.sqrt(seq * F_GROUP_DIM)
    const = lambda shape: pl.BlockSpec(shape, lambda j: (0,) * len(shape),
                                       pipeline_mode=pl.Buffered(1))
    strip = pl.BlockSpec((seq, tn), lambda j: (0, j))
    out = pl.pallas_call(
        functools.partial(_fft_kernel, n1=n1, scale=scale),
        grid=(nw // tn,),
        in_specs=[strip, strip, const(d1.shape), const(m3c.shape), const(m3s.shape)],
        out_specs=pl.BlockSpec((GRID_W, n1, tn), lambda j: (0, 0, j)),
        out_shape=jax.ShapeDtypeStruct((GRID_W, n1, nw), BF16),
        scratch_shapes=[pltpu.VMEM((GRID_W, 2 * n1, tn), BF16)],
        compiler_params=pltpu.CompilerParams(dimension_semantics=("parallel",),
                                             vmem_limit_bytes=V7X_VMEM_LIMIT),
        name="position_dft",
    )(a_w, b_w, d1, m3c, m3s)
    return out.reshape(seq, nw)


def _scan_kernel(u_ref, wg_ref, ba_ref, bx_ref, sp_ref, h0_ref, y_ref, hf_ref, a_s, b_s, h_s,
                 *, reverse, steps, heads, nb):
    @pl.when(pl.program_id(0) == 0)
    def _():
        h_s[...] = h0_ref[...]

    u = u_ref[...]
    for hd in range(heads):
        sl = slice(hd * RNN_BLOCK, (hd + 1) * RNN_BLOCK)
        ub = u[:, sl]
        g = _dot(ub, wg_ref[hd])
        tr = jnp.tanh(g[:, :RNN_BLOCK] + ba_ref[:, sl])
        ti = jnp.tanh(g[:, RNN_BLOCK:] + bx_ref[:, sl])
        neg_log_a = sp_ref[:, sl] * (1.0 + tr)
        a = jnp.exp2(neg_log_a * (-LOG2_E))
        q = (a * a + 1.0) * jnp.tanh(neg_log_a)
        mult = jnp.where(q > 0.0, q * lax.rsqrt(q), 0.0)
        a_s[:, sl] = a
        b_s[:, sl] = mult * ((1.0 + ti) * ub.astype(F32))

    def step(j, h):
        t = (steps - 1 - j) if reverse else j
        r0 = pl.multiple_of(t * nb, nb)
        h = a_s[pl.ds(r0, nb), :] * h + b_s[pl.ds(r0, nb), :]
        b_s[pl.ds(r0, nb), :] = h
        return h

    h = lax.fori_loop(0, steps, step, h_s[...], unroll=8)
    h_s[...] = h
    hf_ref[...] = h
    y_ref[...] = b_s[...].astype(y_ref.dtype)


def _scan_call(u, wg, ba, bx, sp, h0, reverse, name):
    n, c = u.shape
    nb = h0.shape[0]
    heads = wg.shape[0]
    tr = min(SCAN_ROWS, n)
    nt = n // tr
    const = lambda shape: pl.BlockSpec(shape, lambda i: (0,) * len(shape))
    tile = pl.BlockSpec((tr, c), (lambda i: (nt - 1 - i, 0)) if reverse else (lambda i: (i, 0)))
    return pl.pallas_call(
        functools.partial(_scan_kernel, reverse=reverse, steps=tr // nb, heads=heads, nb=nb),
        grid=(nt,),
        in_specs=[tile, const(wg.shape), const((1, c)), const((1, c)), const((1, c)),
                  const((nb, c))],
        out_specs=[tile, const((nb, c))],
        out_shape=[jax.ShapeDtypeStruct((n, c), BF16), jax.ShapeDtypeStruct((nb, c), F32)],
        scratch_shapes=[pltpu.VMEM((tr, c), F32), pltpu.VMEM((tr, c), F32),
                        pltpu.VMEM((nb, c), F32)],
        compiler_params=pltpu.CompilerParams(dimension_semantics=("arbitrary",),
                                             vmem_limit_bytes=V7X_VMEM_LIMIT),
        name=name,
    )(u, wg, ba, bx, sp, h0)


def _merge_kernel(x_ref, yf_ref, yb_ref, gg_ref, gf_ref, gr_ref, yw_ref, pi_ref, wf_ref,
                  wr_ref, wo_ref, g1_ref, n2_ref, sc2_ref, sh2_ref, wrt_ref, brt_ref, tri_ref,
                  x1_ref, hm_ref, ti_ref, tw_ref, rk_ref, cnt_ref, carry, *, nb, d_f):
    @pl.when(pl.program_id(0) == 0)
    def _():
        carry[...] = jnp.zeros_like(carry)

    yw = jnp.concatenate([yw_ref[:, bi * d_f:(bi + 1) * d_f] for bi in range(nb)], axis=0)
    yfm = _dot(pi_ref[...], yw).astype(BF16)
    f_out = _dot(yfm, wf_ref[...])
    y = yf_ref[...].astype(F32) + yb_ref[...].astype(F32)
    r_out = _dot((y * gg_ref[...].astype(F32)).astype(BF16), wr_ref[...])
    mix = gf_ref[...].astype(F32) * f_out + gr_ref[...].astype(F32) * r_out
    x1 = _to_time_major(x_ref[...]) + _per_batch(_dot(mix.astype(BF16), wo_ref[...]), g1_ref[...])
    x1_ref[...] = x1

    h2 = _modulate(_rms(x1, n2_ref[...]), sc2_ref[...], sh2_ref[...])
    hm_ref[...] = _pack_bf16_pairs(h2)

    hh, hl = _split_bf16(h2)
    wh, wl = _split_bf16(wrt_ref[...])
    lg = _dot_nt(wh, hh) + _dot_nt(wl, hh) + _dot_nt(wh, hl) + brt_ref[...]
    ne, tm = lg.shape
    eid = lax.broadcasted_iota(jnp.int32, (ne, tm), 0)
    row_k = lax.broadcasted_iota(jnp.int32, (TOP_K, tm), 0)
    vals, hots = [], []
    top_i = jnp.zeros((TOP_K, tm), jnp.int32)
    for k in range(TOP_K):
        m = jnp.max(lg, axis=0, keepdims=True)
        idx = jnp.min(jnp.where(lg == m, eid, ne), axis=0, keepdims=True)
        hit = eid == idx
        vals.append(m)
        hots.append(hit.astype(F32))
        top_i = jnp.where(row_k == k, idx, top_i)
        lg = jnp.where(hit, -jnp.inf, lg)
    exps = [jnp.exp(v - vals[0]) for v in vals]
    inv = 1.0 / (exps[0] + exps[1] + exps[2] + exps[3])
    ti_ref[...] = top_i
    w_pad = jnp.concatenate([e * inv for e in exps]
                            + [jnp.zeros((V7X_LANES - TOP_K, tm), F32)], axis=0)
    tw_ref[...] = jnp.transpose(w_pad)[:, :TOP_K]

    per_tok = hots[0] + hots[1] + hots[2] + hots[3]
    before = _dot(per_tok.astype(BF16), tri_ref[...]) + carry[...]
    rank = jnp.zeros((TOP_K, tm), F32)
    for k in range(TOP_K):
        rank = jnp.where(row_k == k, jnp.sum(hots[k] * before, axis=0, keepdims=True), rank)
    rk_ref[...] = rank.astype(jnp.int32)
    carry[...] = carry[...] + jnp.sum(per_tok, axis=1, keepdims=True)
    cnt_ref[...] = carry[...]


def _merge_call(x, y_f, y_b, gg, gf, gr, yw, pint, w_f, w_r, w_o, g1, n2, sc2, sh2, w_rt, b_rt,
                tri, d_f, off, nt):
    nb, seq, d = x.shape
    n = nt * ROW_TILE
    d_rnn = y_f.shape[1]
    ne = w_rt.shape[0]
    const = lambda shape: pl.BlockSpec(shape, lambda i: (0,) * len(shape))
    row = lambda c: pl.BlockSpec((ROW_TILE, c), lambda i: (i, 0))
    row_in = lambda c: pl.BlockSpec((ROW_TILE, c), lambda i: (i + off, 0))
    k_major = pl.BlockSpec((TOP_K, ROW_TILE), lambda i: (0, i))
    return pl.pallas_call(
        functools.partial(_merge_kernel, nb=nb, d_f=d_f),
        grid=(nt,),
        in_specs=[pl.BlockSpec((nb, GRID_W, d), lambda i: (0, i + off, 0)), row_in(d_rnn),
                  row_in(d_rnn), row_in(d_rnn), row_in(d), row_in(d),
                  pl.BlockSpec((GRID_W, nb * d_f), lambda i: (i + off, 0)),
                  const((ROW_TILE, ROW_TILE)), const((d_f, d)), const((d_rnn, d)), const((d, d)),
                  const((nb, d)), const((1, d)), const((nb, d)), const((nb, d)),
                  const((ne, d)), const((ne, 1)), const((ROW_TILE, ROW_TILE))],
        out_specs=[row(d), row(d // 2), k_major, row(TOP_K), k_major, const((ne, 1))],
        out_shape=[jax.ShapeDtypeStruct((n, d), F32), jax.ShapeDtypeStruct((n, d // 2), jnp.int32),
                   jax.ShapeDtypeStruct((TOP_K, n), jnp.int32),
                   jax.ShapeDtypeStruct((n, TOP_K), F32),
                   jax.ShapeDtypeStruct((TOP_K, n), jnp.int32),
                   jax.ShapeDtypeStruct((ne, 1), F32)],
        scratch_shapes=[pltpu.VMEM((ne, 1), F32)],
        compiler_params=pltpu.CompilerParams(dimension_semantics=("arbitrary",),
                                             vmem_limit_bytes=V7X_VMEM_LIMIT),
        name="merge_and_route",
    )(x, y_f, y_b, gg, gf, gr, yw, pint, w_f, w_r, w_o, g1, n2, sc2, sh2, w_rt, b_rt, tri)


def _sc_mesh():
    return plsc.VectorSubcoreMesh(core_axis_name="core", subcore_axis_name="subcore")


def _sc_scatter_rows(x, idx_rows, n_out):
    n, c = x.shape
    kk = len(idx_rows)
    mesh = _sc_mesh()
    workers = mesh.num_cores * mesh.num_subcores
    per_worker = n // workers
    assert per_worker % SC_WINDOW == 0

    @functools.partial(
        pl.kernel, out_type=jax.ShapeDtypeStruct((n_out, c), x.dtype), mesh=mesh,
        scratch_types=[pltpu.VMEM((SC_WINDOW, c), x.dtype)]
        + [pltpu.VMEM((SC_WINDOW,), jnp.int32)] * kk + [pltpu.SemaphoreType.DMA],
        name="sc_dispatch_scatter")
    def scatter(x_hbm, *refs):
        idx_hbm, o_hbm, rows_v = refs[:kk], refs[kk], refs[kk + 1]
        idx_v, sem = refs[kk + 2:2 * kk + 2], refs[2 * kk + 2]
        wid = lax.axis_index("subcore") * mesh.num_cores + lax.axis_index("core")

        @pl.loop(0, per_worker // SC_WINDOW)
        def _(j):
            base = pl.multiple_of(wid * per_worker + j * SC_WINDOW, SC_WINDOW)
            pltpu.sync_copy(x_hbm.at[pl.ds(base, SC_WINDOW)], rows_v)
            for ih, iv in zip(idx_hbm, idx_v):
                pltpu.sync_copy(ih.at[pl.ds(base, SC_WINDOW)], iv)
            copies = [pltpu.async_copy(rows_v, o_hbm.at[iv], sem) for iv in idx_v]
            for cp in copies:
                cp.wait()

    return scatter(x, *idx_rows)


def _sc_gather_rows(src, idx):
    n_out = idx.shape[0]
    c = src.shape[1]
    mesh = _sc_mesh()
    workers = mesh.num_cores * mesh.num_subcores
    per_worker = n_out // workers
    assert per_worker % SC_WINDOW == 0

    @functools.partial(
        pl.kernel, out_type=jax.ShapeDtypeStruct((n_out, c), src.dtype), mesh=mesh,
        scratch_types=[pltpu.VMEM((SC_WINDOW, c), src.dtype), pltpu.VMEM((SC_WINDOW,), jnp.int32),
                       pltpu.SemaphoreType.DMA],
        name="sc_combine_gather")
    def gather(src_hbm, idx_hbm, o_hbm, rows_v, idx_v, sem):
        wid = lax.axis_index("subcore") * mesh.num_cores + lax.axis_index("core")

        @pl.loop(0, per_worker // SC_WINDOW)
        def _(j):
            base = pl.multiple_of(wid * per_worker + j * SC_WINDOW, SC_WINDOW)
            pltpu.sync_copy(idx_hbm.at[pl.ds(base, SC_WINDOW)], idx_v)
            pltpu.async_copy(src_hbm.at[idx_v], rows_v, sem).wait()
            pltpu.sync_copy(rows_v, o_hbm.at[pl.ds(base, SC_WINDOW)])

    return gather(src, idx)


def _expert_kernel(be_ref, nv_ref, x_ref, wgu_ref, bgu_ref, wd_ref, bd_ref, y_ref, wgu_s, wd_s,
                   *, d_e):
    i = pl.program_id(0)
    valid = nv_ref[i]

    @pl.when((i == 0) | (be_ref[i] != be_ref[jnp.maximum(i - 1, 0)]))
    def _():
        wgu_s[...] = wgu_ref[...].astype(BF16)
        wd_s[...] = wd_ref[...].astype(BF16)

    @pl.when(valid > 0)
    def _():
        rows = lax.broadcasted_iota(jnp.int32, x_ref.shape, 0)
        x = _unpack_bf16_pairs(jnp.where(rows < valid, x_ref[...], 0)).astype(BF16)
        gu = _dot(x, wgu_s[...]) + bgu_ref[...]
        gate = jnp.minimum(gu[:, :d_e], SWIGLU_LIMIT)
        up = jnp.clip(gu[:, d_e:], -SWIGLU_LIMIT, SWIGLU_LIMIT)
        act = (up + 1.0) * gate * _sigmoid(SWIGLU_ALPHA * gate)
        y_ref[...] = _pack_bf16_pairs(_dot(act.astype(BF16), wd_s[...]) + bd_ref[...])

    @pl.when(valid <= 0)
    def _():
        y_ref[...] = jnp.zeros_like(y_ref)


def _expert_call(xs, block_expert, block_valid, w_gu, b_gu, w_down, b_down):
    p, dh = xs.shape
    ne, d, d_e2 = w_gu.shape
    d_e = d_e2 // 2
    nblk = p // MOE_ROWS
    grid_spec = pltpu.PrefetchScalarGridSpec(
        num_scalar_prefetch=2,
        grid=(nblk,),
        in_specs=[pl.BlockSpec((MOE_ROWS, dh), lambda i, be, nv: (i, 0)),
                  pl.BlockSpec((None, d, d_e2), lambda i, be, nv: (be[i], 0, 0)),
                  pl.BlockSpec((None, 1, d_e2), lambda i, be, nv: (be[i], 0, 0)),
                  pl.BlockSpec((None, d_e, d), lambda i, be, nv: (be[i], 0, 0)),
                  pl.BlockSpec((None, 1, d), lambda i, be, nv: (be[i], 0, 0))],
        out_specs=pl.BlockSpec((MOE_ROWS, dh), lambda i, be, nv: (i, 0)),
        scratch_shapes=[pltpu.VMEM((d, d_e2), BF16), pltpu.VMEM((d_e, d), BF16)],
    )
    return pl.pallas_call(
        functools.partial(_expert_kernel, d_e=d_e),
        grid_spec=grid_spec,
        out_shape=jax.ShapeDtypeStruct((p, dh), jnp.int32),
        compiler_params=pltpu.CompilerParams(dimension_semantics=("arbitrary",),
                                             vmem_limit_bytes=V7X_VMEM_LIMIT),
        name="expert_ffn",
    )(block_expert, block_valid, xs, w_gu, b_gu.reshape(ne, 1, d_e2), w_down,
      b_down.reshape(ne, 1, d))


def _combine_kernel(x1_ref, yg_ref, tw_ref, g2_ref, nf_ref, *rest):
    o_ref = rest[-1]
    w = tw_ref[...]
    moe = w[:, 0:1] * _unpack_bf16_pairs(yg_ref[0])
    for k in range(1, TOP_K):
        moe = moe + w[:, k:k + 1] * _unpack_bf16_pairs(yg_ref[k])
    out = _rms(x1_ref[...] + _per_batch(moe, g2_ref[...]), nf_ref[...])
    o_ref[...] = _to_batch_major(out, g2_ref.shape[0])


def _combine_call(x1, yg, top_w, g2, nf, off, seq, prev_out):
    n, d = x1.shape
    nb = g2.shape[0]
    nt = n // ROW_TILE
    const = lambda shape: pl.BlockSpec(shape, lambda i: (0,) * len(shape))
    row = lambda c: pl.BlockSpec((ROW_TILE, c), lambda i: (i, 0))
    in_specs = [row(d), pl.BlockSpec((TOP_K, ROW_TILE, d // 2), lambda i: (0, i, 0)),
                row(TOP_K), const((nb, d)), const((1, d))]
    args = [x1, yg, top_w, g2, nf]
    aliases = {}
    if prev_out is not None:
        in_specs.append(pl.BlockSpec(memory_space=pl.ANY))
        args.append(prev_out)
        aliases = {len(args) - 1: 0}
    return pl.pallas_call(
        _combine_kernel,
        grid=(nt,),
        in_specs=in_specs,
        out_specs=pl.BlockSpec((nb, GRID_W, d), lambda i: (0, i + off, 0)),
        out_shape=jax.ShapeDtypeStruct((nb, seq, d), F32),
        input_output_aliases=aliases,
        compiler_params=pltpu.CompilerParams(dimension_semantics=("parallel",),
                                             vmem_limit_bytes=V7X_VMEM_LIMIT),
        name="combine_final_norm",
    )(*args)


def _tile_constants(nb):
    t = np.arange(GRID_W)[:, None]
    b = np.arange(nb)[None, :]
    inter = np.zeros((ROW_TILE, ROW_TILE), np.float32)
    inter[(t * nb + b).ravel(), (b * GRID_W + t).ravel()] = 1.0
    c = np.arange(F_GROUP_DIM)
    ang = 2.0 * np.pi * ((c[:, None] * c[None, :]) % F_GROUP_DIM) / F_GROUP_DIM
    cs = np.concatenate([np.cos(ang), np.sin(ang)], axis=1)
    tri = np.triu(np.ones((ROW_TILE, ROW_TILE), np.float32), 1)
    as_bf16 = lambda v: jnp.asarray(v, F32).astype(BF16)
    return as_bf16(inter), as_bf16(cs), as_bf16(tri)


def kernel(x, c, ctx, c_ctx, w_ada, b_ada, norm1, w_in, conv_w, conv_b, gate_a_w, gate_a_b,
           gate_x_w, gate_x_b, lru_lambda, w_fourier, w_rnn, w_out, norm2, w_router, b_router,
           w_gu, b_gu, w_down, b_down, norm_f):
    nb, seq, d = x.shape
    depth = w_ada.shape[0]
    assert depth == 1 and nb == V7X_SUBLANES and seq % FFT_PERM_ROWS == 0
    n = nb * seq
    d_rnn = conv_w.shape[-1]
    d_f = w_fourier.shape[1]
    ne = w_router.shape[-1]
    lyr = 0
    inter, cs, tri = _tile_constants(nb)

    ctx_t = jnp.transpose(ctx, (1, 0, 2)).reshape(-1, d)

    cond = jnp.zeros((2 * V7X_SUBLANES, d), F32).at[:nb].set(c).at[nb].set(c_ctx)
    mods = _ada_call(cond, w_ada[lyr], b_ada[lyr].reshape(1, -1))
    sh1, sc1, g1, sh2, sc2, g2 = [mods[:nb, k * d:(k + 1) * d] for k in range(6)]
    csh1, csc1 = mods[nb:nb + 1, 0:d], mods[nb:nb + 1, d:2 * d]

    w_in_b = w_in[lyr].astype(BF16)
    cw, cb = conv_w[lyr], conv_b[lyr].reshape(1, d_rnn)
    n1r = norm1[lyr].reshape(1, d)
    wg = jnp.concatenate([gate_a_w[lyr], gate_x_w[lyr]], axis=-1).astype(BF16)
    sp = (0.5 * LRU_C) * jax.nn.softplus(-lru_lambda[lyr].astype(F32))
    scan = lambda u, dr, h0, name: _scan_call(
        u, wg[dr], 0.5 * gate_a_b[lyr, dr].reshape(1, -1), 0.5 * gate_x_b[lyr, dr].reshape(1, -1),
        sp[dr].reshape(1, -1), h0, bool(dr), name)

    u_ctx = _ctx_in_call(ctx_t, csc1, csh1, n1r, w_in_b[:, d_f:d_f + d_rnn], cw, cb, nb)
    h_zero = jnp.zeros((nb, d_rnn), F32)
    _, h_ctx_f = scan(u_ctx, 0, h_zero, "ctx_scan_fwd")
    _, h_ctx_b = scan(u_ctx, 1, h_zero, "ctx_scan_bwd")

    a_w, b_w, u, gg, gf, gr = _in_proj_call(x, sc1, sh1, n1r, w_in_b, cw, cb, cs, d_f,
                                            d_rnn)
    y_wide = _fft_call(a_w, b_w)
    y_f, _ = scan(u, 0, h_ctx_f, "scan_fwd")
    y_b, _ = scan(u, 1, h_ctx_b, "scan_bwd")
    merge_consts = (inter, w_fourier[lyr].astype(BF16), w_rnn[lyr].astype(BF16),
                    w_out[lyr].astype(BF16), g1, norm2[lyr].reshape(1, d), sc2, sh2,
                    w_router[lyr].T, b_router[lyr].reshape(ne, 1), tri, d_f)
    expert_ids = jnp.arange(ne, dtype=jnp.int32)[:, None, None]
    tiles = n // ROW_TILE
    out, off = None, 0
    for share in MOE_CHUNK_SHARES:
        nt = tiles * share // sum(MOE_CHUNK_SHARES)
        n_c = nt * ROW_TILE
        nblk = (n_c * TOP_K + ne * (MOE_ROWS - 1) + MOE_ROWS - 1) // MOE_ROWS
        block_start = jnp.arange(nblk, dtype=jnp.int32) * MOE_ROWS
        x1, hmod, top_i, top_w, rank, counts = _merge_call(
            x, y_f, y_b, gg, gf, gr, y_wide, *merge_consts, off, nt)

        cnt = counts.reshape(ne).astype(jnp.int32)
        padded = (cnt + MOE_ROWS - 1) // MOE_ROWS * MOE_ROWS
        ends = jnp.cumsum(padded)
        starts = ends - padded
        start_of = jnp.sum(jnp.where(top_i[None] == expert_ids, starts[:, None, None], 0), axis=0)
        dest_km = start_of + rank
        block_expert = jnp.minimum(jnp.sum(block_start[:, None] >= ends[None, :], axis=1),
                                   ne - 1).astype(jnp.int32)
        block_valid = jnp.clip((starts + cnt)[block_expert] - block_start, 0, MOE_ROWS)
        block_valid = jnp.where(block_start < ends[-1], block_valid, 0).astype(jnp.int32)

        xs = _sc_scatter_rows(hmod, [dest_km[k] for k in range(TOP_K)], nblk * MOE_ROWS)
        ys = _expert_call(xs, block_expert, block_valid, w_gu[lyr], b_gu[lyr], w_down[lyr],
                          b_down[lyr])
        yg = _sc_gather_rows(ys, dest_km.reshape(TOP_K * n_c)).reshape(TOP_K, n_c, d // 2)
        out = _combine_call(x1, yg, top_w, g2, norm_f.reshape(1, d), off, seq, out)
        off += nt
    return out
```

```python
import functools
import math

import numpy as np
import jax
import jax.numpy as jnp
from jax import lax
from jax.experimental import pallas as pl
from jax.experimental.pallas import tpu as pltpu
from jax.experimental.pallas import tpu_sc as plsc

F32 = jnp.float32
BF16 = jnp.bfloat16

GRID_W = 64
CONV_W = 4
LRU_C = 8.0
TOP_K = 4
F_GROUP_DIM = 128
RNN_BLOCK = 128
SWIGLU_ALPHA = 1.702
SWIGLU_LIMIT = 7.0
EPS = 1e-6
LOG2_E = 1.4426950408889634

V7X_LANES = 128
V7X_SUBLANES = 8
V7X_BF16_ROWS = 16
V7X_VMEM_LIMIT = 56 * 1024 * 1024

ROW_TILE = GRID_W * V7X_SUBLANES
FFT_PERM_ROWS = V7X_BF16_ROWS * GRID_W
FFT_COLS = 256
SCAN_ROWS = 2048
IN_PROJ_GRID_ROWS = 2
MOE_ROWS = 512
MOE_CHUNKS = 2
SC_WINDOW = 128
HIGH_HALF = -65536
LOW_HALF = 65535


def _dot(a, b):
    return jnp.dot(a, b, preferred_element_type=F32)


def _dot_nt(a, b):
    return lax.dot_general(a, b, (((1,), (1,)), ((), ())), preferred_element_type=F32)


def _split_bf16(a):
    hi = a.astype(BF16)
    return hi, (a - hi.astype(F32)).astype(BF16)


def _dot3(a, b):
    ah, al = _split_bf16(a)
    bh, bl = _split_bf16(b)
    return _dot(ah, bh) + _dot(ah, bl) + _dot(al, bh)


def _sigmoid(x):
    return 0.5 * (jnp.tanh(0.5 * x) + 1.0)


def _gelu_tanh(x):
    c = math.sqrt(2.0 / math.pi)
    return 0.5 * x * (1.0 + jnp.tanh(c * (x + 0.044715 * (x * x * x))))


def _pack_bf16_pairs(v):
    c = v.shape[1] // 2
    bits = pltpu.bitcast(v.astype(BF16).astype(F32), jnp.int32)
    return (bits[:, :c] & HIGH_HALF) | ((bits[:, c:] >> 16) & LOW_HALF)


def _unpack_bf16_pairs(p):
    hi = pltpu.bitcast(p & HIGH_HALF, F32)
    lo = pltpu.bitcast(p << 16, F32)
    return jnp.concatenate([hi, lo], axis=1)


def _rms(x, g):
    return x * lax.rsqrt(jnp.mean(x * x, axis=-1, keepdims=True) + EPS) * g


def _modulate(xn, sc, sh):
    if sc.shape[0] == 1:
        return xn * (1.0 + sc) + sh
    r, d = xn.shape
    nb = sc.shape[0]
    x3 = xn.reshape(r // nb, nb, d)
    return (x3 * (1.0 + sc)[None] + sh[None]).reshape(r, d)


def _to_time_major(x3):
    nb, t, d = x3.shape
    return pltpu.einshape("btd->tbd", x3).reshape(t * nb, d)


def _to_batch_major(v, nb):
    r, d = v.shape
    return pltpu.einshape("tbd->btd", v.reshape(r // nb, nb, d))


def _per_batch(v, g):
    r, d = v.shape
    nb = g.shape[0]
    return (v.reshape(r // nb, nb, d) * g[None]).reshape(r, d)


def _conv_time_major(z, cw, cb, nb):
    r, c = z.shape

    def shifted(s):
        k = abs(s) * nb
        zero = jnp.zeros((k, c), z.dtype)
        if s > 0:
            return jnp.concatenate([zero, z[:r - k]], axis=0)
        return jnp.concatenate([z[k:], zero], axis=0)

    return (cb + shifted(2) * cw[0:1] + shifted(1) * cw[1:2] + z * cw[2:3]
            + shifted(-1) * cw[3:4])


def _ada_kernel(c_ref, w_ref, b_ref, o_ref):
    cnd = c_ref[...]
    o_ref[...] = _dot3(cnd * _sigmoid(cnd), w_ref[...]) + b_ref[...]


def _ada_call(cond, w, b):
    r, d = cond.shape
    n = w.shape[1]
    return pl.pallas_call(
        _ada_kernel,
        grid=(n // d,),
        in_specs=[pl.BlockSpec((r, d), lambda j: (0, 0)),
                  pl.BlockSpec((d, d), lambda j: (0, j)),
                  pl.BlockSpec((1, d), lambda j: (0, j))],
        out_specs=pl.BlockSpec((r, d), lambda j: (0, j)),
        out_shape=jax.ShapeDtypeStruct((r, n), F32),
        name="ada_modulation",
    )(cond, w, b)


def _ctx_in_kernel(x_ref, sc_ref, sh_ref, g_ref, w_ref, cw_ref, cb_ref, u_ref, *, nb):
    h = _modulate(_rms(x_ref[...], g_ref[...]), sc_ref[...], sh_ref[...])
    z = _dot(h.astype(BF16), w_ref[...])
    u_ref[...] = (0.5 * _conv_time_major(z, cw_ref[...], cb_ref[...], nb)).astype(u_ref.dtype)


def _ctx_in_call(ctx_t, sc, sh, g, w_r, cw, cb, nb):
    r, d = ctx_t.shape
    c = w_r.shape[1]
    full = lambda shape: pl.BlockSpec(shape, lambda i: (0,) * len(shape))
    return pl.pallas_call(
        functools.partial(_ctx_in_kernel, nb=nb),
        grid=(1,),
        in_specs=[full((r, d)), full((1, d)), full((1, d)), full((1, d)), full((d, c)),
                  full((CONV_W, c)), full((1, c))],
        out_specs=full((r, c)),
        out_shape=jax.ShapeDtypeStruct((r, c), BF16),
        compiler_params=pltpu.CompilerParams(vmem_limit_bytes=V7X_VMEM_LIMIT),
        name="ctx_in_proj",
    )(ctx_t, sc, sh, g, w_r, cw, cb)


def _in_proj_kernel(x_ref, sc_ref, sh_ref, g_ref, w_ref, cw_ref, cb_ref, cs_ref,
                    a_ref, b_ref, u_ref, gg_ref, gf_ref, gr_ref, *, nb, d_f, d_rnn):
    steps, d = x_ref.shape[1], x_ref.shape[2]
    h3 = (_rms(x_ref[...], g_ref[...]) * (1.0 + sc_ref[...])[:, None, :]
          + sh_ref[...][:, None, :])
    h = _to_time_major(h3).astype(BF16)
    c_r, c_g, c_m = d_f, d_f + d_rnn, d_f + 2 * d_rnn

    zf = _dot(h3.reshape(nb * steps, d).astype(BF16), w_ref[:, 0:c_r]).astype(BF16)
    for grp in range(d_f // F_GROUP_DIM):
        ab = _dot(zf[:, grp * F_GROUP_DIM:(grp + 1) * F_GROUP_DIM], cs_ref[...])
        for bi in range(nb):
            rows = slice(bi * steps, (bi + 1) * steps)
            col = bi * d_f + grp * F_GROUP_DIM
            a_ref[:, col:col + F_GROUP_DIM] = ab[rows, :F_GROUP_DIM].astype(BF16)
            b_ref[:, col:col + F_GROUP_DIM] = ab[rows, F_GROUP_DIM:].astype(BF16)

    zr = _dot(h, w_ref[:, c_r:c_g])
    for grow in range(steps // GRID_W):
        rows = slice(grow * ROW_TILE, (grow + 1) * ROW_TILE)
        u_ref[rows, :] = (0.5 * _conv_time_major(zr[rows], cw_ref[...], cb_ref[...],
                                                 nb)).astype(BF16)
    gg_ref[...] = _gelu_tanh(_dot(h, w_ref[:, c_g:c_m])).astype(BF16)
    gf_ref[...] = _sigmoid(_dot(h, w_ref[:, c_m:c_m + d])).astype(BF16)
    gr_ref[...] = _sigmoid(_dot(h, w_ref[:, c_m + d:c_m + 2 * d])).astype(BF16)


def _in_proj_call(x, sc, sh, g, w_in, cw, cb, cs, d_f, d_rnn):
    nb, seq, d = x.shape
    n = nb * seq
    d_in = w_in.shape[1]
    steps = IN_PROJ_GRID_ROWS * GRID_W
    nt = seq // steps
    const = lambda shape: pl.BlockSpec(shape, lambda i: (0,) * len(shape),
                                       pipeline_mode=pl.Buffered(1))
    row = lambda c: pl.BlockSpec((nb * steps, c), lambda i: (i, 0))
    wide = pl.BlockSpec((steps, nb * d_f), lambda i: (i, 0))
    return pl.pallas_call(
        functools.partial(_in_proj_kernel, nb=nb, d_f=d_f, d_rnn=d_rnn),
        grid=(nt,),
        in_specs=[pl.BlockSpec((nb, steps, d), lambda i: (0, i, 0)), const((nb, d)),
                  const((nb, d)), const((1, d)), const((d, d_in)),
                  const((CONV_W, d_rnn)), const((1, d_rnn)),
                  const((F_GROUP_DIM, 2 * F_GROUP_DIM))],
        out_specs=[wide, wide, row(d_rnn), row(d_rnn), row(d), row(d)],
        out_shape=[jax.ShapeDtypeStruct((seq, nb * d_f), BF16),
                   jax.ShapeDtypeStruct((seq, nb * d_f), BF16),
                   jax.ShapeDtypeStruct((n, d_rnn), BF16),
                   jax.ShapeDtypeStruct((n, d_rnn), BF16),
                   jax.ShapeDtypeStruct((n, d), BF16),
                   jax.ShapeDtypeStruct((n, d), BF16)],
        compiler_params=pltpu.CompilerParams(dimension_semantics=("parallel",),
                                             vmem_limit_bytes=V7X_VMEM_LIMIT),
        name="latent_in_proj",
    )(x, sc, sh, g, w_in, cw, cb, cs)


def _fft_kernel(a_ref, b_ref, d1_ref, m3c_ref, m3s_ref, o_ref, s_ref, *, n1, scale):
    tn = a_ref.shape[1]
    blk = V7X_BF16_ROWS
    for j in range(n1 // blk):
        rows = slice(j * FFT_PERM_ROWS, (j + 1) * FFT_PERM_ROWS)
        s_ref[:, j * blk:(j + 1) * blk, :] = pltpu.einshape(
            "abd->bad", a_ref[rows, :].reshape(blk, GRID_W, tn))
        s_ref[:, n1 + j * blk:n1 + (j + 1) * blk, :] = pltpu.einshape(
            "abd->bad", b_ref[rows, :].reshape(blk, GRID_W, tn))

    def stage1(l2, carry):
        s_ref[l2] = _dot(d1_ref[l2], s_ref[l2]).astype(BF16)
        return carry

    lax.fori_loop(0, GRID_W, stage1, 0)

    for i in range(n1 // blk):
        xr = s_ref[:, i * blk:(i + 1) * blk, :].reshape(FFT_PERM_ROWS, tn)
        xi = s_ref[:, n1 + i * blk:n1 + (i + 1) * blk, :].reshape(FFT_PERM_ROWS, tn)
        y = (_dot(m3c_ref[...], xr) + _dot(m3s_ref[...], xi)) * scale
        o_ref[:, i * blk:(i + 1) * blk, :] = y.astype(BF16).reshape(GRID_W, blk, tn)


def _fft_constants(seq):
    n1 = seq // GRID_W
    blk = V7X_BF16_ROWS
    k1 = np.arange(n1)[None, :, None]
    l1 = np.arange(n1)[None, None, :]
    l2v = np.arange(GRID_W)[:, None, None]
    ang = 2.0 * np.pi * ((k1 * (GRID_W * l1 + l2v)) % seq) / seq
    dc, ds = np.cos(ang), np.sin(ang)
    d1 = np.concatenate([np.concatenate([dc, -ds], axis=2),
                         np.concatenate([-ds, -dc], axis=2)], axis=1)
    k2 = np.arange(GRID_W)[:, None]
    l2m = np.arange(GRID_W)[None, :]
    ang3 = 2.0 * np.pi * ((k2 * l2m) % GRID_W) / GRID_W
    eye = np.eye(blk)
    m3c = np.kron(np.cos(ang3), eye)
    m3s = np.kron(np.sin(ang3), eye)
    as_bf16 = lambda v: jnp.asarray(v, F32).astype(BF16)
    return as_bf16(d1), as_bf16(m3c), as_bf16(m3s)


def _fft_call(a_w, b_w):
    seq, nw = a_w.shape
    n1 = seq // GRID_W
    tn = min(FFT_COLS, nw)
    d1, m3c, m3s = _fft_constants(seq)
    scale = 1.0 / math.sqrt(seq * F_GROUP_DIM)
    const = lambda shape: pl.BlockSpec(shape, lambda j: (0,) * len(shape),
                                       pipeline_mode=pl.Buffered(1))
    strip = pl.BlockSpec((seq, tn), lambda j: (0, j))
    out = pl.pallas_call(
        functools.partial(_fft_kernel, n1=n1, scale=scale),
        grid=(nw // tn,),
        in_specs=[strip, strip, const(d1.shape), const(m3c.shape), const(m3s.shape)],
        out_specs=pl.BlockSpec((GRID_W, n1, tn), lambda j: (0, 0, j)),
        out_shape=jax.ShapeDtypeStruct((GRID_W, n1, nw), BF16),
        scratch_shapes=[pltpu.VMEM((GRID_W, 2 * n1, tn), BF16)],
        compiler_params=pltpu.CompilerParams(dimension_semantics=("parallel",),
                                             vmem_limit_bytes=V7X_VMEM_LIMIT),
        name="position_dft",
    )(a_w, b_w, d1, m3c, m3s)
    return out.reshape(seq, nw)


def _scan_kernel(u_ref, wg_ref, ba_ref, bx_ref, sp_ref, h0_ref, y_ref, hf_ref, a_s, b_s, h_s,
                 *, reverse, steps, heads, nb):
    @pl.when(pl.program_id(0) == 0)
    def _():
        h_s[...] = h0_ref[...]

    u = u_ref[...]
    for hd in range(heads):
        sl = slice(hd * RNN_BLOCK, (hd + 1) * RNN_BLOCK)
        ub = u[:, sl]
        g = _dot(ub, wg_ref[hd])
        tr = jnp.tanh(g[:, :RNN_BLOCK] + ba_ref[:, sl])
        ti = jnp.tanh(g[:, RNN_BLOCK:] + bx_ref[:, sl])
        neg_log_a = sp_ref[:, sl] * (1.0 + tr)
        a = jnp.exp2(neg_log_a * (-LOG2_E))
        q = (a * a + 1.0) * jnp.tanh(neg_log_a)
        mult = jnp.where(q > 0.0, q * lax.rsqrt(q), 0.0)
        a_s[:, sl] = a
        b_s[:, sl] = mult * ((1.0 + ti) * ub.astype(F32))

    def step(j, h):
        t = (steps - 1 - j) if reverse else j
        r0 = pl.multiple_of(t * nb, nb)
        h = a_s[pl.ds(r0, nb), :] * h + b_s[pl.ds(r0, nb), :]
        b_s[pl.ds(r0, nb), :] = h
        return h

    h = lax.fori_loop(0, steps, step, h_s[...], unroll=8)
    h_s[...] = h
    hf_ref[...] = h
    y_ref[...] = b_s[...].astype(y_ref.dtype)


def _scan_call(u, wg, ba, bx, sp, h0, reverse, name):
    n, c = u.shape
    nb = h0.shape[0]
    heads = wg.shape[0]
    tr = min(SCAN_ROWS, n)
    nt = n // tr
    const = lambda shape: pl.BlockSpec(shape, lambda i: (0,) * len(shape))
    tile = pl.BlockSpec((tr, c), (lambda i: (nt - 1 - i, 0)) if reverse else (lambda i: (i, 0)))
    return pl.pallas_call(
        functools.partial(_scan_kernel, reverse=reverse, steps=tr // nb, heads=heads, nb=nb),
        grid=(nt,),
        in_specs=[tile, const(wg.shape), const((1, c)), const((1, c)), const((1, c)),
                  const((nb, c))],
        out_specs=[tile, const((nb, c))],
        out_shape=[jax.ShapeDtypeStruct((n, c), BF16), jax.ShapeDtypeStruct((nb, c), F32)],
        scratch_shapes=[pltpu.VMEM((tr, c), F32), pltpu.VMEM((tr, c), F32),
                        pltpu.VMEM((nb, c), F32)],
        compiler_params=pltpu.CompilerParams(dimension_semantics=("arbitrary",),
                                             vmem_limit_bytes=V7X_VMEM_LIMIT),
        name=name,
    )(u, wg, ba, bx, sp, h0)


def _merge_kernel(x_ref, yf_ref, yb_ref, gg_ref, gf_ref, gr_ref, yw_ref, pi_ref, wf_ref,
                  wr_ref, wo_ref, g1_ref, n2_ref, sc2_ref, sh2_ref, wrt_ref, brt_ref, tri_ref,
                  x1_ref, hm_ref, ti_ref, tw_ref, rk_ref, cnt_ref, carry, *, nb, d_f):
    @pl.when(pl.program_id(0) == 0)
    def _():
        carry[...] = jnp.zeros_like(carry)

    yw = jnp.concatenate([yw_ref[:, bi * d_f:(bi + 1) * d_f] for bi in range(nb)], axis=0)
    yfm = _dot(pi_ref[...], yw).astype(BF16)
    f_out = _dot(yfm, wf_ref[...])
    y = yf_ref[...].astype(F32) + yb_ref[...].astype(F32)
    r_out = _dot((y * gg_ref[...].astype(F32)).astype(BF16), wr_ref[...])
    mix = gf_ref[...].astype(F32) * f_out + gr_ref[...].astype(F32) * r_out
    x1 = _to_time_major(x_ref[...]) + _per_batch(_dot(mix.astype(BF16), wo_ref[...]), g1_ref[...])
    x1_ref[...] = x1

    h2 = _modulate(_rms(x1, n2_ref[...]), sc2_ref[...], sh2_ref[...])
    hm_ref[...] = _pack_bf16_pairs(h2)

    hh, hl = _split_bf16(h2)
    wh, wl = _split_bf16(wrt_ref[...])
    lg = _dot_nt(wh, hh) + _dot_nt(wl, hh) + _dot_nt(wh, hl) + brt_ref[...]
    ne, tm = lg.shape
    eid = lax.broadcasted_iota(jnp.int32, (ne, tm), 0)
    row_k = lax.broadcasted_iota(jnp.int32, (TOP_K, tm), 0)
    vals, hots = [], []
    top_i = jnp.zeros((TOP_K, tm), jnp.int32)
    for k in range(TOP_K):
        m = jnp.max(lg, axis=0, keepdims=True)
        idx = jnp.min(jnp.where(lg == m, eid, ne), axis=0, keepdims=True)
        hit = eid == idx
        vals.append(m)
        hots.append(hit.astype(F32))
        top_i = jnp.where(row_k == k, idx, top_i)
        lg = jnp.where(hit, -jnp.inf, lg)
    exps = [jnp.exp(v - vals[0]) for v in vals]
    inv = 1.0 / (exps[0] + exps[1] + exps[2] + exps[3])
    ti_ref[...] = top_i
    w_pad = jnp.concatenate([e * inv for e in exps]
                            + [jnp.zeros((V7X_LANES - TOP_K, tm), F32)], axis=0)
    tw_ref[...] = jnp.transpose(w_pad)[:, :TOP_K]

    per_tok = hots[0] + hots[1] + hots[2] + hots[3]
    before = _dot(per_tok.astype(BF16), tri_ref[...]) + carry[...]
    rank = jnp.zeros((TOP_K, tm), F32)
    for k in range(TOP_K):
        rank = jnp.where(row_k == k, jnp.sum(hots[k] * before, axis=0, keepdims=True), rank)
    rk_ref[...] = rank.astype(jnp.int32)
    carry[...] = carry[...] + jnp.sum(per_tok, axis=1, keepdims=True)
    cnt_ref[...] = carry[...]


def _merge_call(x, y_f, y_b, gg, gf, gr, yw, pint, w_f, w_r, w_o, g1, n2, sc2, sh2, w_rt, b_rt,
                tri, d_f, chunk, n_chunks):
    nb, seq, d = x.shape
    n = nb * seq // n_chunks
    d_rnn = y_f.shape[1]
    ne = w_rt.shape[0]
    nt = n // ROW_TILE
    off = chunk * nt
    const = lambda shape: pl.BlockSpec(shape, lambda i: (0,) * len(shape))
    row = lambda c: pl.BlockSpec((ROW_TILE, c), lambda i: (i, 0))
    row_in = lambda c: pl.BlockSpec((ROW_TILE, c), lambda i: (i + off, 0))
    k_major = pl.BlockSpec((TOP_K, ROW_TILE), lambda i: (0, i))
    return pl.pallas_call(
        functools.partial(_merge_kernel, nb=nb, d_f=d_f),
        grid=(nt,),
        in_specs=[pl.BlockSpec((nb, GRID_W, d), lambda i: (0, i + off, 0)), row_in(d_rnn),
                  row_in(d_rnn), row_in(d_rnn), row_in(d), row_in(d),
                  pl.BlockSpec((GRID_W, nb * d_f), lambda i: (i + off, 0)),
                  const((ROW_TILE, ROW_TILE)), const((d_f, d)), const((d_rnn, d)), const((d, d)),
                  const((nb, d)), const((1, d)), const((nb, d)), const((nb, d)),
                  const((ne, d)), const((ne, 1)), const((ROW_TILE, ROW_TILE))],
        out_specs=[row(d), row(d // 2), k_major, row(TOP_K), k_major, const((ne, 1))],
        out_shape=[jax.ShapeDtypeStruct((n, d), F32), jax.ShapeDtypeStruct((n, d // 2), jnp.int32),
                   jax.ShapeDtypeStruct((TOP_K, n), jnp.int32),
                   jax.ShapeDtypeStruct((n, TOP_K), F32),
                   jax.ShapeDtypeStruct((TOP_K, n), jnp.int32),
                   jax.ShapeDtypeStruct((ne, 1), F32)],
        scratch_shapes=[pltpu.VMEM((ne, 1), F32)],
        compiler_params=pltpu.CompilerParams(dimension_semantics=("arbitrary",),
                                             vmem_limit_bytes=V7X_VMEM_LIMIT),
        name="merge_and_route",
    )(x, y_f, y_b, gg, gf, gr, yw, pint, w_f, w_r, w_o, g1, n2, sc2, sh2, w_rt, b_rt, tri)


def _sc_mesh():
    return plsc.VectorSubcoreMesh(core_axis_name="core", subcore_axis_name="subcore")


def _sc_scatter_rows(x, idx_rows, n_out):
    n, c = x.shape
    kk = len(idx_rows)
    mesh = _sc_mesh()
    workers = mesh.num_cores * mesh.num_subcores
    per_worker = n // workers
    assert per_worker % SC_WINDOW == 0

    @functools.partial(
        pl.kernel, out_type=jax.ShapeDtypeStruct((n_out, c), x.dtype), mesh=mesh,
        scratch_types=[pltpu.VMEM((SC_WINDOW, c), x.dtype)]
        + [pltpu.VMEM((SC_WINDOW,), jnp.int32)] * kk + [pltpu.SemaphoreType.DMA],
        name="sc_dispatch_scatter")
    def scatter(x_hbm, *refs):
        idx_hbm, o_hbm, rows_v = refs[:kk], refs[kk], refs[kk + 1]
        idx_v, sem = refs[kk + 2:2 * kk + 2], refs[2 * kk + 2]
        wid = lax.axis_index("subcore") * mesh.num_cores + lax.axis_index("core")

        @pl.loop(0, per_worker // SC_WINDOW)
        def _(j):
            base = pl.multiple_of(wid * per_worker + j * SC_WINDOW, SC_WINDOW)
            pltpu.sync_copy(x_hbm.at[pl.ds(base, SC_WINDOW)], rows_v)
            for ih, iv in zip(idx_hbm, idx_v):
                pltpu.sync_copy(ih.at[pl.ds(base, SC_WINDOW)], iv)
            copies = [pltpu.async_copy(rows_v, o_hbm.at[iv], sem) for iv in idx_v]
            for cp in copies:
                cp.wait()

    return scatter(x, *idx_rows)


def _sc_gather_rows(src, idx):
    n_out = idx.shape[0]
    c = src.shape[1]
    mesh = _sc_mesh()
    workers = mesh.num_cores * mesh.num_subcores
    per_worker = n_out // workers
    assert per_worker % SC_WINDOW == 0

    @functools.partial(
        pl.kernel, out_type=jax.ShapeDtypeStruct((n_out, c), src.dtype), mesh=mesh,
        scratch_types=[pltpu.VMEM((SC_WINDOW, c), src.dtype), pltpu.VMEM((SC_WINDOW,), jnp.int32),
                       pltpu.SemaphoreType.DMA],
        name="sc_combine_gather")
    def gather(src_hbm, idx_hbm, o_hbm, rows_v, idx_v, sem):
        wid = lax.axis_index("subcore") * mesh.num_cores + lax.axis_index("core")

        @pl.loop(0, per_worker // SC_WINDOW)
        def _(j):
            base = pl.multiple_of(wid * per_worker + j * SC_WINDOW, SC_WINDOW)
            pltpu.sync_copy(idx_hbm.at[pl.ds(base, SC_WINDOW)], idx_v)
            pltpu.async_copy(src_hbm.at[idx_v], rows_v, sem).wait()
            pltpu.sync_copy(rows_v, o_hbm.at[pl.ds(base, SC_WINDOW)])

    return gather(src, idx)


def _expert_kernel(be_ref, nv_ref, x_ref, wgu_ref, bgu_ref, wd_ref, bd_ref, y_ref, wgu_s, wd_s,
                   *, d_e):
    i = pl.program_id(0)
    valid = nv_ref[i]

    @pl.when((i == 0) | (be_ref[i] != be_ref[jnp.maximum(i - 1, 0)]))
    def _():
        wgu_s[...] = wgu_ref[...].astype(BF16)
        wd_s[...] = wd_ref[...].astype(BF16)

    @pl.when(valid > 0)
    def _():
        rows = lax.broadcasted_iota(jnp.int32, x_ref.shape, 0)
        x = _unpack_bf16_pairs(jnp.where(rows < valid, x_ref[...], 0)).astype(BF16)
        gu = _dot(x, wgu_s[...]) + bgu_ref[...]
        gate = jnp.minimum(gu[:, :d_e], SWIGLU_LIMIT)
        up = jnp.clip(gu[:, d_e:], -SWIGLU_LIMIT, SWIGLU_LIMIT)
        act = (up + 1.0) * gate * _sigmoid(SWIGLU_ALPHA * gate)
        y_ref[...] = _pack_bf16_pairs(_dot(act.astype(BF16), wd_s[...]) + bd_ref[...])

    @pl.when(valid <= 0)
    def _():
        y_ref[...] = jnp.zeros_like(y_ref)


def _expert_call(xs, block_expert, block_valid, w_gu, b_gu, w_down, b_down):
    p, dh = xs.shape
    ne, d, d_e2 = w_gu.shape
    d_e = d_e2 // 2
    nblk = p // MOE_ROWS
    grid_spec = pltpu.PrefetchScalarGridSpec(
        num_scalar_prefetch=2,
        grid=(nblk,),
        in_specs=[pl.BlockSpec((MOE_ROWS, dh), lambda i, be, nv: (i, 0)),
                  pl.BlockSpec((None, d, d_e2), lambda i, be, nv: (be[i], 0, 0)),
                  pl.BlockSpec((None, 1, d_e2), lambda i, be, nv: (be[i], 0, 0)),
                  pl.BlockSpec((None, d_e, d), lambda i, be, nv: (be[i], 0, 0)),
                  pl.BlockSpec((None, 1, d), lambda i, be, nv: (be[i], 0, 0))],
        out_specs=pl.BlockSpec((MOE_ROWS, dh), lambda i, be, nv: (i, 0)),
        scratch_shapes=[pltpu.VMEM((d, d_e2), BF16), pltpu.VMEM((d_e, d), BF16)],
    )
    return pl.pallas_call(
        functools.partial(_expert_kernel, d_e=d_e),
        grid_spec=grid_spec,
        out_shape=jax.ShapeDtypeStruct((p, dh), jnp.int32),
        compiler_params=pltpu.CompilerParams(dimension_semantics=("arbitrary",),
                                             vmem_limit_bytes=V7X_VMEM_LIMIT),
        name="expert_ffn",
    )(block_expert, block_valid, xs, w_gu, b_gu.reshape(ne, 1, d_e2), w_down,
      b_down.reshape(ne, 1, d))


def _combine_kernel(x1_ref, yg_ref, tw_ref, g2_ref, nf_ref, *rest):
    o_ref = rest[-1]
    w = tw_ref[...]
    moe = w[:, 0:1] * _unpack_bf16_pairs(yg_ref[0])
    for k in range(1, TOP_K):
        moe = moe + w[:, k:k + 1] * _unpack_bf16_pairs(yg_ref[k])
    out = _rms(x1_ref[...] + _per_batch(moe, g2_ref[...]), nf_ref[...])
    o_ref[...] = _to_batch_major(out, g2_ref.shape[0])


def _combine_call(x1, yg, top_w, g2, nf, chunk, n_chunks, prev_out):
    n, d = x1.shape
    nb = g2.shape[0]
    seq = n * n_chunks // nb
    nt = n // ROW_TILE
    off = chunk * nt
    const = lambda shape: pl.BlockSpec(shape, lambda i: (0,) * len(shape))
    row = lambda c: pl.BlockSpec((ROW_TILE, c), lambda i: (i, 0))
    in_specs = [row(d), pl.BlockSpec((TOP_K, ROW_TILE, d // 2), lambda i: (0, i, 0)),
                row(TOP_K), const((nb, d)), const((1, d))]
    args = [x1, yg, top_w, g2, nf]
    aliases = {}
    if prev_out is not None:
        in_specs.append(pl.BlockSpec(memory_space=pl.ANY))
        args.append(prev_out)
        aliases = {len(args) - 1: 0}
    return pl.pallas_call(
        _combine_kernel,
        grid=(nt,),
        in_specs=in_specs,
        out_specs=pl.BlockSpec((nb, GRID_W, d), lambda i: (0, i + off, 0)),
        out_shape=jax.ShapeDtypeStruct((nb, seq, d), F32),
        input_output_aliases=aliases,
        compiler_params=pltpu.CompilerParams(dimension_semantics=("parallel",),
                                             vmem_limit_bytes=V7X_VMEM_LIMIT),
        name="combine_final_norm",
    )(*args)


def _tile_constants(nb):
    t = np.arange(GRID_W)[:, None]
    b = np.arange(nb)[None, :]
    inter = np.zeros((ROW_TILE, ROW_TILE), np.float32)
    inter[(t * nb + b).ravel(), (b * GRID_W + t).ravel()] = 1.0
    c = np.arange(F_GROUP_DIM)
    ang = 2.0 * np.pi * ((c[:, None] * c[None, :]) % F_GROUP_DIM) / F_GROUP_DIM
    cs = np.concatenate([np.cos(ang), np.sin(ang)], axis=1)
    tri = np.triu(np.ones((ROW_TILE, ROW_TILE), np.float32), 1)
    as_bf16 = lambda v: jnp.asarray(v, F32).astype(BF16)
    return as_bf16(inter), as_bf16(cs), as_bf16(tri)


def kernel(x, c, ctx, c_ctx, w_ada, b_ada, norm1, w_in, conv_w, conv_b, gate_a_w, gate_a_b,
           gate_x_w, gate_x_b, lru_lambda, w_fourier, w_rnn, w_out, norm2, w_router, b_router,
           w_gu, b_gu, w_down, b_down, norm_f):
    nb, seq, d = x.shape
    depth = w_ada.shape[0]
    assert depth == 1 and nb == V7X_SUBLANES and seq % FFT_PERM_ROWS == 0
    n = nb * seq
    d_rnn = conv_w.shape[-1]
    d_f = w_fourier.shape[1]
    ne = w_router.shape[-1]
    lyr = 0
    inter, cs, tri = _tile_constants(nb)

    ctx_t = jnp.transpose(ctx, (1, 0, 2)).reshape(-1, d)

    cond = jnp.zeros((2 * V7X_SUBLANES, d), F32).at[:nb].set(c).at[nb].set(c_ctx)
    mods = _ada_call(cond, w_ada[lyr], b_ada[lyr].reshape(1, -1))
    sh1, sc1, g1, sh2, sc2, g2 = [mods[:nb, k * d:(k + 1) * d] for k in range(6)]
    csh1, csc1 = mods[nb:nb + 1, 0:d], mods[nb:nb + 1, d:2 * d]

    w_in_b = w_in[lyr].astype(BF16)
    cw, cb = conv_w[lyr], conv_b[lyr].reshape(1, d_rnn)
    n1r = norm1[lyr].reshape(1, d)
    wg = jnp.concatenate([gate_a_w[lyr], gate_x_w[lyr]], axis=-1).astype(BF16)
    sp = (0.5 * LRU_C) * jax.nn.softplus(-lru_lambda[lyr].astype(F32))
    scan = lambda u, dr, h0, name: _scan_call(
        u, wg[dr], 0.5 * gate_a_b[lyr, dr].reshape(1, -1), 0.5 * gate_x_b[lyr, dr].reshape(1, -1),
        sp[dr].reshape(1, -1), h0, bool(dr), name)

    u_ctx = _ctx_in_call(ctx_t, csc1, csh1, n1r, w_in_b[:, d_f:d_f + d_rnn], cw, cb, nb)
    h_zero = jnp.zeros((nb, d_rnn), F32)
    _, h_ctx_f = scan(u_ctx, 0, h_zero, "ctx_scan_fwd")
    _, h_ctx_b = scan(u_ctx, 1, h_zero, "ctx_scan_bwd")

    a_w, b_w, u, gg, gf, gr = _in_proj_call(x, sc1, sh1, n1r, w_in_b, cw, cb, cs, d_f,
                                            d_rnn)
    y_wide = _fft_call(a_w, b_w)
    y_f, _ = scan(u, 0, h_ctx_f, "scan_fwd")
    y_b, _ = scan(u, 1, h_ctx_b, "scan_bwd")
    merge_consts = (inter, w_fourier[lyr].astype(BF16), w_rnn[lyr].astype(BF16),
                    w_out[lyr].astype(BF16), g1, norm2[lyr].reshape(1, d), sc2, sh2,
                    w_router[lyr].T, b_router[lyr].reshape(ne, 1), tri, d_f)
    n_c = n // MOE_CHUNKS
    nblk = (n_c * TOP_K + ne * (MOE_ROWS - 1) + MOE_ROWS - 1) // MOE_ROWS
    block_start = jnp.arange(nblk, dtype=jnp.int32) * MOE_ROWS
    expert_ids = jnp.arange(ne, dtype=jnp.int32)[:, None, None]
    out = None
    for chunk in range(MOE_CHUNKS):
        x1, hmod, top_i, top_w, rank, counts = _merge_call(
            x, y_f, y_b, gg, gf, gr, y_wide, *merge_consts, chunk, MOE_CHUNKS)

        cnt = counts.reshape(ne).astype(jnp.int32)
        padded = (cnt + MOE_ROWS - 1) // MOE_ROWS * MOE_ROWS
        ends = jnp.cumsum(padded)
        starts = ends - padded
        start_of = jnp.sum(jnp.where(top_i[None] == expert_ids, starts[:, None, None], 0), axis=0)
        dest_km = start_of + rank
        block_expert = jnp.minimum(jnp.sum(block_start[:, None] >= ends[None, :], axis=1),
                                   ne - 1).astype(jnp.int32)
        block_valid = jnp.clip((starts + cnt)[block_expert] - block_start, 0, MOE_ROWS)
        block_valid = jnp.where(block_start < ends[-1], block_valid, 0).astype(jnp.int32)

        xs = _sc_scatter_rows(hmod, [dest_km[k] for k in range(TOP_K)], nblk * MOE_ROWS)
        ys = _expert_call(xs, block_expert, block_valid, w_gu[lyr], b_gu[lyr], w_down[lyr],
                          b_down[lyr])
        yg = _sc_gather_rows(ys, dest_km.reshape(TOP_K * n_c)).reshape(TOP_K, n_c, d // 2)
        out = _combine_call(x1, yg, top_w, g2, norm_f.reshape(1, d), chunk, MOE_CHUNKS, out)
    return out
```

```python
import functools
import math

import numpy as np
import jax
import jax.numpy as jnp
from jax import lax
from jax.experimental import pallas as pl
from jax.experimental.pallas import tpu as pltpu
from jax.experimental.pallas import tpu_sc as plsc

F32 = jnp.float32
BF16 = jnp.bfloat16

GRID_W = 64
CONV_W = 4
LRU_C = 8.0
TOP_K = 4
F_GROUP_DIM = 128
RNN_BLOCK = 128
SWIGLU_ALPHA = 1.702
SWIGLU_LIMIT = 7.0
EPS = 1e-6
LOG2_E = 1.4426950408889634

V7X_LANES = 128
V7X_SUBLANES = 8
V7X_BF16_ROWS = 16
V7X_VMEM_LIMIT = 56 * 1024 * 1024

ROW_TILE = GRID_W * V7X_SUBLANES
FFT_PERM_ROWS = V7X_BF16_ROWS * GRID_W
FFT_COLS = 256
SCAN_ROWS = 2048
IN_PROJ_GRID_ROWS = 2
COMBINE_GRID_ROWS = 2
MOE_ROWS = 512
MOE_CHUNKS = 2
SC_WINDOW = 128
HIGH_HALF = -65536
LOW_HALF = 65535


def _dot(a, b):
    return jnp.dot(a, b, preferred_element_type=F32)


def _dot_nt(a, b):
    return lax.dot_general(a, b, (((1,), (1,)), ((), ())), preferred_element_type=F32)


def _split_bf16(a):
    hi = a.astype(BF16)
    return hi, (a - hi.astype(F32)).astype(BF16)


def _dot3(a, b):
    ah, al = _split_bf16(a)
    bh, bl = _split_bf16(b)
    return _dot(ah, bh) + _dot(ah, bl) + _dot(al, bh)


def _sigmoid(x):
    return 0.5 * (jnp.tanh(0.5 * x) + 1.0)


def _gelu_tanh(x):
    c = math.sqrt(2.0 / math.pi)
    return 0.5 * x * (1.0 + jnp.tanh(c * (x + 0.044715 * (x * x * x))))


def _pack_bf16_pairs(v):
    c = v.shape[1] // 2
    bits = pltpu.bitcast(v.astype(BF16).astype(F32), jnp.int32)
    return (bits[:, :c] & HIGH_HALF) | ((bits[:, c:] >> 16) & LOW_HALF)


def _unpack_bf16_pairs(p):
    hi = pltpu.bitcast(p & HIGH_HALF, F32)
    lo = pltpu.bitcast(p << 16, F32)
    return jnp.concatenate([hi, lo], axis=1)


def _rms(x, g):
    return x * lax.rsqrt(jnp.mean(x * x, axis=-1, keepdims=True) + EPS) * g


def _modulate(xn, sc, sh):
    if sc.shape[0] == 1:
        return xn * (1.0 + sc) + sh
    r, d = xn.shape
    nb = sc.shape[0]
    x3 = xn.reshape(r // nb, nb, d)
    return (x3 * (1.0 + sc)[None] + sh[None]).reshape(r, d)


def _to_time_major(x3):
    nb, t, d = x3.shape
    return pltpu.einshape("btd->tbd", x3).reshape(t * nb, d)


def _to_batch_major(v, nb):
    r, d = v.shape
    return pltpu.einshape("tbd->btd", v.reshape(r // nb, nb, d))


def _per_batch(v, g):
    r, d = v.shape
    nb = g.shape[0]
    return (v.reshape(r // nb, nb, d) * g[None]).reshape(r, d)


def _conv_time_major(z, cw, cb, nb):
    r, c = z.shape

    def shifted(s):
        k = abs(s) * nb
        zero = jnp.zeros((k, c), z.dtype)
        if s > 0:
            return jnp.concatenate([zero, z[:r - k]], axis=0)
        return jnp.concatenate([z[k:], zero], axis=0)

    return (cb + shifted(2) * cw[0:1] + shifted(1) * cw[1:2] + z * cw[2:3]
            + shifted(-1) * cw[3:4])


def _ada_kernel(c_ref, w_ref, b_ref, o_ref):
    cnd = c_ref[...]
    o_ref[...] = _dot3(cnd * _sigmoid(cnd), w_ref[...]) + b_ref[...]


def _ada_call(cond, w, b):
    r, d = cond.shape
    n = w.shape[1]
    return pl.pallas_call(
        _ada_kernel,
        grid=(n // d,),
        in_specs=[pl.BlockSpec((r, d), lambda j: (0, 0)),
                  pl.BlockSpec((d, d), lambda j: (0, j)),
                  pl.BlockSpec((1, d), lambda j: (0, j))],
        out_specs=pl.BlockSpec((r, d), lambda j: (0, j)),
        out_shape=jax.ShapeDtypeStruct((r, n), F32),
        name="ada_modulation",
    )(cond, w, b)


def _ctx_in_kernel(x_ref, sc_ref, sh_ref, g_ref, w_ref, cw_ref, cb_ref, u_ref, *, nb):
    h = _modulate(_rms(x_ref[...], g_ref[...]), sc_ref[...], sh_ref[...])
    z = _dot(h.astype(BF16), w_ref[...])
    u_ref[...] = (0.5 * _conv_time_major(z, cw_ref[...], cb_ref[...], nb)).astype(u_ref.dtype)


def _ctx_in_call(ctx_t, sc, sh, g, w_r, cw, cb, nb):
    r, d = ctx_t.shape
    c = w_r.shape[1]
    full = lambda shape: pl.BlockSpec(shape, lambda i: (0,) * len(shape))
    return pl.pallas_call(
        functools.partial(_ctx_in_kernel, nb=nb),
        grid=(1,),
        in_specs=[full((r, d)), full((1, d)), full((1, d)), full((1, d)), full((d, c)),
                  full((CONV_W, c)), full((1, c))],
        out_specs=full((r, c)),
        out_shape=jax.ShapeDtypeStruct((r, c), BF16),
        compiler_params=pltpu.CompilerParams(vmem_limit_bytes=V7X_VMEM_LIMIT),
        name="ctx_in_proj",
    )(ctx_t, sc, sh, g, w_r, cw, cb)


def _in_proj_kernel(x_ref, sc_ref, sh_ref, g_ref, w_ref, cw_ref, cb_ref, cs_ref,
                    a_ref, b_ref, u_ref, gg_ref, gf_ref, gr_ref, *, nb, d_f, d_rnn):
    steps, d = x_ref.shape[1], x_ref.shape[2]
    h3 = (_rms(x_ref[...], g_ref[...]) * (1.0 + sc_ref[...])[:, None, :]
          + sh_ref[...][:, None, :])
    h = _to_time_major(h3).astype(BF16)
    c_r, c_g, c_m = d_f, d_f + d_rnn, d_f + 2 * d_rnn

    zf = _dot(h3.reshape(nb * steps, d).astype(BF16), w_ref[:, 0:c_r]).astype(BF16)
    for grp in range(d_f // F_GROUP_DIM):
        ab = _dot(zf[:, grp * F_GROUP_DIM:(grp + 1) * F_GROUP_DIM], cs_ref[...])
        for bi in range(nb):
            rows = slice(bi * steps, (bi + 1) * steps)
            col = bi * d_f + grp * F_GROUP_DIM
            a_ref[:, col:col + F_GROUP_DIM] = ab[rows, :F_GROUP_DIM].astype(BF16)
            b_ref[:, col:col + F_GROUP_DIM] = ab[rows, F_GROUP_DIM:].astype(BF16)

    zr = _dot(h, w_ref[:, c_r:c_g])
    for grow in range(steps // GRID_W):
        rows = slice(grow * ROW_TILE, (grow + 1) * ROW_TILE)
        u_ref[rows, :] = (0.5 * _conv_time_major(zr[rows], cw_ref[...], cb_ref[...],
                                                 nb)).astype(BF16)
    gg_ref[...] = _gelu_tanh(_dot(h, w_ref[:, c_g:c_m])).astype(BF16)
    gf_ref[...] = _sigmoid(_dot(h, w_ref[:, c_m:c_m + d])).astype(BF16)
    gr_ref[...] = _sigmoid(_dot(h, w_ref[:, c_m + d:c_m + 2 * d])).astype(BF16)


def _in_proj_call(x, sc, sh, g, w_in, cw, cb, cs, d_f, d_rnn):
    nb, seq, d = x.shape
    n = nb * seq
    d_in = w_in.shape[1]
    steps = IN_PROJ_GRID_ROWS * GRID_W
    nt = seq // steps
    const = lambda shape: pl.BlockSpec(shape, lambda i: (0,) * len(shape),
                                       pipeline_mode=pl.Buffered(1))
    row = lambda c: pl.BlockSpec((nb * steps, c), lambda i: (i, 0))
    wide = pl.BlockSpec((steps, nb * d_f), lambda i: (i, 0))
    return pl.pallas_call(
        functools.partial(_in_proj_kernel, nb=nb, d_f=d_f, d_rnn=d_rnn),
        grid=(nt,),
        in_specs=[pl.BlockSpec((nb, steps, d), lambda i: (0, i, 0)), const((nb, d)),
                  const((nb, d)), const((1, d)), const((d, d_in)),
                  const((CONV_W, d_rnn)), const((1, d_rnn)),
                  const((F_GROUP_DIM, 2 * F_GROUP_DIM))],
        out_specs=[wide, wide, row(d_rnn), row(d_rnn), row(d), row(d)],
        out_shape=[jax.ShapeDtypeStruct((seq, nb * d_f), BF16),
                   jax.ShapeDtypeStruct((seq, nb * d_f), BF16),
                   jax.ShapeDtypeStruct((n, d_rnn), BF16),
                   jax.ShapeDtypeStruct((n, d_rnn), BF16),
                   jax.ShapeDtypeStruct((n, d), BF16),
                   jax.ShapeDtypeStruct((n, d), BF16)],
        compiler_params=pltpu.CompilerParams(dimension_semantics=("parallel",),
                                             vmem_limit_bytes=V7X_VMEM_LIMIT),
        name="latent_in_proj",
    )(x, sc, sh, g, w_in, cw, cb, cs)


def _fft_kernel(a_ref, b_ref, d1_ref, m3c_ref, m3s_ref, o_ref, s_ref, *, n1, scale):
    tn = a_ref.shape[1]
    blk = V7X_BF16_ROWS
    for j in range(n1 // blk):
        rows = slice(j * FFT_PERM_ROWS, (j + 1) * FFT_PERM_ROWS)
        s_ref[:, j * blk:(j + 1) * blk, :] = pltpu.einshape(
            "abd->bad", a_ref[rows, :].reshape(blk, GRID_W, tn))
        s_ref[:, n1 + j * blk:n1 + (j + 1) * blk, :] = pltpu.einshape(
            "abd->bad", b_ref[rows, :].reshape(blk, GRID_W, tn))

    def stage1(l2, carry):
        s_ref[l2] = _dot(d1_ref[l2], s_ref[l2]).astype(BF16)
        return carry

    lax.fori_loop(0, GRID_W, stage1, 0)

    for i in range(n1 // blk):
        xr = s_ref[:, i * blk:(i + 1) * blk, :].reshape(FFT_PERM_ROWS, tn)
        xi = s_ref[:, n1 + i * blk:n1 + (i + 1) * blk, :].reshape(FFT_PERM_ROWS, tn)
        y = (_dot(m3c_ref[...], xr) + _dot(m3s_ref[...], xi)) * scale
        o_ref[:, i * blk:(i + 1) * blk, :] = y.astype(BF16).reshape(GRID_W, blk, tn)


def _fft_constants(seq):
    n1 = seq // GRID_W
    blk = V7X_BF16_ROWS
    k1 = np.arange(n1)[None, :, None]
    l1 = np.arange(n1)[None, None, :]
    l2v = np.arange(GRID_W)[:, None, None]
    ang = 2.0 * np.pi * ((k1 * (GRID_W * l1 + l2v)) % seq) / seq
    dc, ds = np.cos(ang), np.sin(ang)
    d1 = np.concatenate([np.concatenate([dc, -ds], axis=2),
                         np.concatenate([-ds, -dc], axis=2)], axis=1)
    k2 = np.arange(GRID_W)[:, None]
    l2m = np.arange(GRID_W)[None, :]
    ang3 = 2.0 * np.pi * ((k2 * l2m) % GRID_W) / GRID_W
    eye = np.eye(blk)
    m3c = np.kron(np.cos(ang3), eye)
    m3s = np.kron(np.sin(ang3), eye)
    as_bf16 = lambda v: jnp.asarray(v, F32).astype(BF16)
    return as_bf16(d1), as_bf16(m3c), as_bf16(m3s)


def _fft_call(a_w, b_w):
    seq, nw = a_w.shape
    n1 = seq // GRID_W
    tn = min(FFT_COLS, nw)
    d1, m3c, m3s = _fft_constants(seq)
    scale = 1.0 / math.sqrt(seq * F_GROUP_DIM)
    const = lambda shape: pl.BlockSpec(shape, lambda j: (0,) * len(shape),
                                       pipeline_mode=pl.Buffered(1))
    strip = pl.BlockSpec((seq, tn), lambda j: (0, j))
    out = pl.pallas_call(
        functools.partial(_fft_kernel, n1=n1, scale=scale),
        grid=(nw // tn,),
        in_specs=[strip, strip, const(d1.shape), const(m3c.shape), const(m3s.shape)],
        out_specs=pl.BlockSpec((GRID_W, n1, tn), lambda j: (0, 0, j)),
        out_shape=jax.ShapeDtypeStruct((GRID_W, n1, nw), BF16),
        scratch_shapes=[pltpu.VMEM((GRID_W, 2 * n1, tn), BF16)],
        compiler_params=pltpu.CompilerParams(dimension_semantics=("parallel",),
                                             vmem_limit_bytes=V7X_VMEM_LIMIT),
        name="position_dft",
    )(a_w, b_w, d1, m3c, m3s)
    return out.reshape(seq, nw)


def _scan_kernel(u_ref, wg_ref, ba_ref, bx_ref, sp_ref, h0_ref, y_ref, hf_ref, a_s, b_s, h_s,
                 *, reverse, steps, heads, nb):
    @pl.when(pl.program_id(0) == 0)
    def _():
        h_s[...] = h0_ref[...]

    u = u_ref[...]
    for hd in range(heads):
        sl = slice(hd * RNN_BLOCK, (hd + 1) * RNN_BLOCK)
        ub = u[:, sl]
        g = _dot(ub, wg_ref[hd])
        tr = jnp.tanh(g[:, :RNN_BLOCK] + ba_ref[:, sl])
        ti = jnp.tanh(g[:, RNN_BLOCK:] + bx_ref[:, sl])
        neg_log_a = sp_ref[:, sl] * (1.0 + tr)
        a = jnp.exp2(neg_log_a * (-LOG2_E))
        q = (a * a + 1.0) * jnp.tanh(neg_log_a)
        mult = jnp.where(q > 0.0, q * lax.rsqrt(q), 0.0)
        a_s[:, sl] = a
        b_s[:, sl] = mult * ((1.0 + ti) * ub.astype(F32))

    def step(j, h):
        t = (steps - 1 - j) if reverse else j
        r0 = pl.multiple_of(t * nb, nb)
        h = a_s[pl.ds(r0, nb), :] * h + b_s[pl.ds(r0, nb), :]
        b_s[pl.ds(r0, nb), :] = h
        return h

    h = lax.fori_loop(0, steps, step, h_s[...], unroll=8)
    h_s[...] = h
    hf_ref[...] = h
    y_ref[...] = b_s[...].astype(y_ref.dtype)


def _scan_call(u, wg, ba, bx, sp, h0, reverse, name):
    n, c = u.shape
    nb = h0.shape[0]
    heads = wg.shape[0]
    tr = min(SCAN_ROWS, n)
    nt = n // tr
    const = lambda shape: pl.BlockSpec(shape, lambda i: (0,) * len(shape))
    tile = pl.BlockSpec((tr, c), (lambda i: (nt - 1 - i, 0)) if reverse else (lambda i: (i, 0)))
    return pl.pallas_call(
        functools.partial(_scan_kernel, reverse=reverse, steps=tr // nb, heads=heads, nb=nb),
        grid=(nt,),
        in_specs=[tile, const(wg.shape), const((1, c)), const((1, c)), const((1, c)),
                  const((nb, c))],
        out_specs=[tile, const((nb, c))],
        out_shape=[jax.ShapeDtypeStruct((n, c), BF16), jax.ShapeDtypeStruct((nb, c), F32)],
        scratch_shapes=[pltpu.VMEM((tr, c), F32), pltpu.VMEM((tr, c), F32),
                        pltpu.VMEM((nb, c), F32)],
        compiler_params=pltpu.CompilerParams(dimension_semantics=("arbitrary",),
                                             vmem_limit_bytes=V7X_VMEM_LIMIT),
        name=name,
    )(u, wg, ba, bx, sp, h0)


def _merge_kernel(x_ref, yf_ref, yb_ref, gg_ref, gf_ref, gr_ref, yw_ref, pi_ref, wf_ref,
                  wr_ref, wo_ref, g1_ref, n2_ref, sc2_ref, sh2_ref, wrt_ref, brt_ref, tri_ref,
                  x1_ref, hm_ref, ti_ref, tw_ref, rk_ref, cnt_ref, carry, *, nb, d_f):
    @pl.when(pl.program_id(0) == 0)
    def _():
        carry[...] = jnp.zeros_like(carry)

    yw = jnp.concatenate([yw_ref[:, bi * d_f:(bi + 1) * d_f] for bi in range(nb)], axis=0)
    yfm = _dot(pi_ref[...], yw).astype(BF16)
    f_out = _dot(yfm, wf_ref[...])
    y = yf_ref[...].astype(F32) + yb_ref[...].astype(F32)
    r_out = _dot((y * gg_ref[...].astype(F32)).astype(BF16), wr_ref[...])
    mix = gf_ref[...].astype(F32) * f_out + gr_ref[...].astype(F32) * r_out
    x1 = _to_time_major(x_ref[...]) + _per_batch(_dot(mix.astype(BF16), wo_ref[...]), g1_ref[...])
    x1_ref[...] = x1

    h2 = _modulate(_rms(x1, n2_ref[...]), sc2_ref[...], sh2_ref[...])
    hm_ref[...] = _pack_bf16_pairs(h2)

    hh, hl = _split_bf16(h2)
    wh, wl = _split_bf16(wrt_ref[...])
    lg = _dot_nt(wh, hh) + _dot_nt(wl, hh) + _dot_nt(wh, hl) + brt_ref[...]
    ne, tm = lg.shape
    eid = lax.broadcasted_iota(jnp.int32, (ne, tm), 0)
    row_k = lax.broadcasted_iota(jnp.int32, (TOP_K, tm), 0)
    vals, hots = [], []
    top_i = jnp.zeros((TOP_K, tm), jnp.int32)
    for k in range(TOP_K):
        m = jnp.max(lg, axis=0, keepdims=True)
        idx = jnp.min(jnp.where(lg == m, eid, ne), axis=0, keepdims=True)
        hit = eid == idx
        vals.append(m)
        hots.append(hit.astype(F32))
        top_i = jnp.where(row_k == k, idx, top_i)
        lg = jnp.where(hit, -jnp.inf, lg)
    exps = [jnp.exp(v - vals[0]) for v in vals]
    inv = 1.0 / (exps[0] + exps[1] + exps[2] + exps[3])
    ti_ref[...] = top_i
    w_pad = jnp.concatenate([e * inv for e in exps]
                            + [jnp.zeros((V7X_LANES - TOP_K, tm), F32)], axis=0)
    tw_ref[...] = jnp.transpose(w_pad)[:, :TOP_K]

    per_tok = hots[0] + hots[1] + hots[2] + hots[3]
    before = _dot(per_tok.astype(BF16), tri_ref[...]) + carry[...]
    rank = jnp.zeros((TOP_K, tm), F32)
    for k in range(TOP_K):
        rank = jnp.where(row_k == k, jnp.sum(hots[k] * before, axis=0, keepdims=True), rank)
    rk_ref[...] = rank.astype(jnp.int32)
    carry[...] = carry[...] + jnp.sum(per_tok, axis=1, keepdims=True)
    cnt_ref[...] = carry[...]


def _merge_call(x, y_f, y_b, gg, gf, gr, yw, pint, w_f, w_r, w_o, g1, n2, sc2, sh2, w_rt, b_rt,
                tri, d_f, chunk, n_chunks):
    nb, seq, d = x.shape
    n = nb * seq // n_chunks
    d_rnn = y_f.shape[1]
    ne = w_rt.shape[0]
    nt = n // ROW_TILE
    off = chunk * nt
    const = lambda shape: pl.BlockSpec(shape, lambda i: (0,) * len(shape))
    row = lambda c: pl.BlockSpec((ROW_TILE, c), lambda i: (i, 0))
    row_in = lambda c: pl.BlockSpec((ROW_TILE, c), lambda i: (i + off, 0))
    k_major = pl.BlockSpec((TOP_K, ROW_TILE), lambda i: (0, i))
    return pl.pallas_call(
        functools.partial(_merge_kernel, nb=nb, d_f=d_f),
        grid=(nt,),
        in_specs=[pl.BlockSpec((nb, GRID_W, d), lambda i: (0, i + off, 0)), row_in(d_rnn),
                  row_in(d_rnn), row_in(d_rnn), row_in(d), row_in(d),
                  pl.BlockSpec((GRID_W, nb * d_f), lambda i: (i + off, 0)),
                  const((ROW_TILE, ROW_TILE)), const((d_f, d)), const((d_rnn, d)), const((d, d)),
                  const((nb, d)), const((1, d)), const((nb, d)), const((nb, d)),
                  const((ne, d)), const((ne, 1)), const((ROW_TILE, ROW_TILE))],
        out_specs=[row(d), row(d // 2), k_major, row(TOP_K), k_major, const((ne, 1))],
        out_shape=[jax.ShapeDtypeStruct((n, d), F32), jax.ShapeDtypeStruct((n, d // 2), jnp.int32),
                   jax.ShapeDtypeStruct((TOP_K, n), jnp.int32),
                   jax.ShapeDtypeStruct((n, TOP_K), F32),
                   jax.ShapeDtypeStruct((TOP_K, n), jnp.int32),
                   jax.ShapeDtypeStruct((ne, 1), F32)],
        scratch_shapes=[pltpu.VMEM((ne, 1), F32)],
        compiler_params=pltpu.CompilerParams(dimension_semantics=("arbitrary",),
                                             vmem_limit_bytes=V7X_VMEM_LIMIT),
        name="merge_and_route",
    )(x, y_f, y_b, gg, gf, gr, yw, pint, w_f, w_r, w_o, g1, n2, sc2, sh2, w_rt, b_rt, tri)


def _sc_mesh():
    return plsc.VectorSubcoreMesh(core_axis_name="core", subcore_axis_name="subcore")


def _sc_scatter_rows(x, idx_rows, n_out):
    n, c = x.shape
    kk = len(idx_rows)
    mesh = _sc_mesh()
    workers = mesh.num_cores * mesh.num_subcores
    per_worker = n // workers
    assert per_worker % SC_WINDOW == 0

    @functools.partial(
        pl.kernel, out_type=jax.ShapeDtypeStruct((n_out, c), x.dtype), mesh=mesh,
        scratch_types=[pltpu.VMEM((SC_WINDOW, c), x.dtype)]
        + [pltpu.VMEM((SC_WINDOW,), jnp.int32)] * kk + [pltpu.SemaphoreType.DMA],
        name="sc_dispatch_scatter")
    def scatter(x_hbm, *refs):
        idx_hbm, o_hbm, rows_v = refs[:kk], refs[kk], refs[kk + 1]
        idx_v, sem = refs[kk + 2:2 * kk + 2], refs[2 * kk + 2]
        wid = lax.axis_index("subcore") * mesh.num_cores + lax.axis_index("core")

        @pl.loop(0, per_worker // SC_WINDOW)
        def _(j):
            base = pl.multiple_of(wid * per_worker + j * SC_WINDOW, SC_WINDOW)
            pltpu.sync_copy(x_hbm.at[pl.ds(base, SC_WINDOW)], rows_v)
            for ih, iv in zip(idx_hbm, idx_v):
                pltpu.sync_copy(ih.at[pl.ds(base, SC_WINDOW)], iv)
            copies = [pltpu.async_copy(rows_v, o_hbm.at[iv], sem) for iv in idx_v]
            for cp in copies:
                cp.wait()

    return scatter(x, *idx_rows)


def _sc_gather_rows(src, idx):
    n_out = idx.shape[0]
    c = src.shape[1]
    mesh = _sc_mesh()
    workers = mesh.num_cores * mesh.num_subcores
    per_worker = n_out // workers
    assert per_worker % SC_WINDOW == 0

    @functools.partial(
        pl.kernel, out_type=jax.ShapeDtypeStruct((n_out, c), src.dtype), mesh=mesh,
        scratch_types=[pltpu.VMEM((SC_WINDOW, c), src.dtype), pltpu.VMEM((SC_WINDOW,), jnp.int32),
                       pltpu.SemaphoreType.DMA],
        name="sc_combine_gather")
    def gather(src_hbm, idx_hbm, o_hbm, rows_v, idx_v, sem):
        wid = lax.axis_index("subcore") * mesh.num_cores + lax.axis_index("core")

        @pl.loop(0, per_worker // SC_WINDOW)
        def _(j):
            base = pl.multiple_of(wid * per_worker + j * SC_WINDOW, SC_WINDOW)
            pltpu.sync_copy(idx_hbm.at[pl.ds(base, SC_WINDOW)], idx_v)
            pltpu.async_copy(src_hbm.at[idx_v], rows_v, sem).wait()
            pltpu.sync_copy(rows_v, o_hbm.at[pl.ds(base, SC_WINDOW)])

    return gather(src, idx)


def _expert_kernel(be_ref, nv_ref, x_ref, wgu_ref, bgu_ref, wd_ref, bd_ref, y_ref, wgu_s, wd_s,
                   *, d_e):
    i = pl.program_id(0)
    valid = nv_ref[i]

    @pl.when((i == 0) | (be_ref[i] != be_ref[jnp.maximum(i - 1, 0)]))
    def _():
        wgu_s[...] = wgu_ref[...].astype(BF16)
        wd_s[...] = wd_ref[...].astype(BF16)

    @pl.when(valid > 0)
    def _():
        rows = lax.broadcasted_iota(jnp.int32, x_ref.shape, 0)
        x = _unpack_bf16_pairs(jnp.where(rows < valid, x_ref[...], 0)).astype(BF16)
        gu = _dot(x, wgu_s[...]) + bgu_ref[...]
        gate = jnp.minimum(gu[:, :d_e], SWIGLU_LIMIT)
        up = jnp.clip(gu[:, d_e:], -SWIGLU_LIMIT, SWIGLU_LIMIT)
        act = (up + 1.0) * gate * _sigmoid(SWIGLU_ALPHA * gate)
        y_ref[...] = _pack_bf16_pairs(_dot(act.astype(BF16), wd_s[...]) + bd_ref[...])

    @pl.when(valid <= 0)
    def _():
        y_ref[...] = jnp.zeros_like(y_ref)


def _expert_call(xs, block_expert, block_valid, w_gu, b_gu, w_down, b_down):
    p, dh = xs.shape
    ne, d, d_e2 = w_gu.shape
    d_e = d_e2 // 2
    nblk = p // MOE_ROWS
    grid_spec = pltpu.PrefetchScalarGridSpec(
        num_scalar_prefetch=2,
        grid=(nblk,),
        in_specs=[pl.BlockSpec((MOE_ROWS, dh), lambda i, be, nv: (i, 0)),
                  pl.BlockSpec((None, d, d_e2), lambda i, be, nv: (be[i], 0, 0)),
                  pl.BlockSpec((None, 1, d_e2), lambda i, be, nv: (be[i], 0, 0)),
                  pl.BlockSpec((None, d_e, d), lambda i, be, nv: (be[i], 0, 0)),
                  pl.BlockSpec((None, 1, d), lambda i, be, nv: (be[i], 0, 0))],
        out_specs=pl.BlockSpec((MOE_ROWS, dh), lambda i, be, nv: (i, 0)),
        scratch_shapes=[pltpu.VMEM((d, d_e2), BF16), pltpu.VMEM((d_e, d), BF16)],
    )
    return pl.pallas_call(
        functools.partial(_expert_kernel, d_e=d_e),
        grid_spec=grid_spec,
        out_shape=jax.ShapeDtypeStruct((p, dh), jnp.int32),
        compiler_params=pltpu.CompilerParams(dimension_semantics=("arbitrary",),
                                             vmem_limit_bytes=V7X_VMEM_LIMIT),
        name="expert_ffn",
    )(block_expert, block_valid, xs, w_gu, b_gu.reshape(ne, 1, d_e2), w_down,
      b_down.reshape(ne, 1, d))


def _combine_kernel(x1_ref, yg_ref, tw_ref, g2_ref, nf_ref, *rest):
    o_ref = rest[-1]
    w = tw_ref[...]
    moe = w[:, 0:1] * _unpack_bf16_pairs(yg_ref[0])
    for k in range(1, TOP_K):
        moe = moe + w[:, k:k + 1] * _unpack_bf16_pairs(yg_ref[k])
    out = _rms(x1_ref[...] + _per_batch(moe, g2_ref[...]), nf_ref[...])
    o_ref[...] = _to_batch_major(out, g2_ref.shape[0])


def _combine_call(x1, yg, top_w, g2, nf, chunk, n_chunks, prev_out):
    n, d = x1.shape
    nb = g2.shape[0]
    seq = n * n_chunks // nb
    steps = COMBINE_GRID_ROWS * GRID_W
    rows = nb * steps
    nt = n // rows
    off = chunk * nt
    const = lambda shape: pl.BlockSpec(shape, lambda i: (0,) * len(shape))
    row = lambda c: pl.BlockSpec((rows, c), lambda i: (i, 0))
    in_specs = [row(d), pl.BlockSpec((TOP_K, rows, d // 2), lambda i: (0, i, 0)),
                row(TOP_K), const((nb, d)), const((1, d))]
    args = [x1, yg, top_w, g2, nf]
    aliases = {}
    if prev_out is not None:
        in_specs.append(pl.BlockSpec(memory_space=pl.ANY))
        args.append(prev_out)
        aliases = {len(args) - 1: 0}
    return pl.pallas_call(
        _combine_kernel,
        grid=(nt,),
        in_specs=in_specs,
        out_specs=pl.BlockSpec((nb, steps, d), lambda i: (0, i + off, 0)),
        out_shape=jax.ShapeDtypeStruct((nb, seq, d), F32),
        input_output_aliases=aliases,
        compiler_params=pltpu.CompilerParams(dimension_semantics=("parallel",),
                                             vmem_limit_bytes=V7X_VMEM_LIMIT),
        name="combine_final_norm",
    )(*args)


def _tile_constants(nb):
    t = np.arange(GRID_W)[:, None]
    b = np.arange(nb)[None, :]
    inter = np.zeros((ROW_TILE, ROW_TILE), np.float32)
    inter[(t * nb + b).ravel(), (b * GRID_W + t).ravel()] = 1.0
    c = np.arange(F_GROUP_DIM)
    ang = 2.0 * np.pi * ((c[:, None] * c[None, :]) % F_GROUP_DIM) / F_GROUP_DIM
    cs = np.concatenate([np.cos(ang), np.sin(ang)], axis=1)
    tri = np.triu(np.ones((ROW_TILE, ROW_TILE), np.float32), 1)
    as_bf16 = lambda v: jnp.asarray(v, F32).astype(BF16)
    return as_bf16(inter), as_bf16(cs), as_bf16(tri)


def kernel(x, c, ctx, c_ctx, w_ada, b_ada, norm1, w_in, conv_w, conv_b, gate_a_w, gate_a_b,
           gate_x_w, gate_x_b, lru_lambda, w_fourier, w_rnn, w_out, norm2, w_router, b_router,
           w_gu, b_gu, w_down, b_down, norm_f):
    nb, seq, d = x.shape
    depth = w_ada.shape[0]
    assert depth == 1 and nb == V7X_SUBLANES and seq % FFT_PERM_ROWS == 0
    n = nb * seq
    d_rnn = conv_w.shape[-1]
    d_f = w_fourier.shape[1]
    ne = w_router.shape[-1]
    lyr = 0
    inter, cs, tri = _tile_constants(nb)

    ctx_t = jnp.transpose(ctx, (1, 0, 2)).reshape(-1, d)

    cond = jnp.zeros((2 * V7X_SUBLANES, d), F32).at[:nb].set(c).at[nb].set(c_ctx)
    mods = _ada_call(cond, w_ada[lyr], b_ada[lyr].reshape(1, -1))
    sh1, sc1, g1, sh2, sc2, g2 = [mods[:nb, k * d:(k + 1) * d] for k in range(6)]
    csh1, csc1 = mods[nb:nb + 1, 0:d], mods[nb:nb + 1, d:2 * d]

    w_in_b = w_in[lyr].astype(BF16)
    cw, cb = conv_w[lyr], conv_b[lyr].reshape(1, d_rnn)
    n1r = norm1[lyr].reshape(1, d)
    wg = jnp.concatenate([gate_a_w[lyr], gate_x_w[lyr]], axis=-1).astype(BF16)
    sp = (0.5 * LRU_C) * jax.nn.softplus(-lru_lambda[lyr].astype(F32))
    scan = lambda u, dr, h0, name: _scan_call(
        u, wg[dr], 0.5 * gate_a_b[lyr, dr].reshape(1, -1), 0.5 * gate_x_b[lyr, dr].reshape(1, -1),
        sp[dr].reshape(1, -1), h0, bool(dr), name)

    u_ctx = _ctx_in_call(ctx_t, csc1, csh1, n1r, w_in_b[:, d_f:d_f + d_rnn], cw, cb, nb)
    h_zero = jnp.zeros((nb, d_rnn), F32)
    _, h_ctx_f = scan(u_ctx, 0, h_zero, "ctx_scan_fwd")
    _, h_ctx_b = scan(u_ctx, 1, h_zero, "ctx_scan_bwd")

    a_w, b_w, u, gg, gf, gr = _in_proj_call(x, sc1, sh1, n1r, w_in_b, cw, cb, cs, d_f,
                                            d_rnn)
    y_wide = _fft_call(a_w, b_w)
    y_f, _ = scan(u, 0, h_ctx_f, "scan_fwd")
    y_b, _ = scan(u, 1, h_ctx_b, "scan_bwd")
    merge_consts = (inter, w_fourier[lyr].astype(BF16), w_rnn[lyr].astype(BF16),
                    w_out[lyr].astype(BF16), g1, norm2[lyr].reshape(1, d), sc2, sh2,
                    w_router[lyr].T, b_router[lyr].reshape(ne, 1), tri, d_f)
    n_c = n // MOE_CHUNKS
    nblk = (n_c * TOP_K + ne * (MOE_ROWS - 1) + MOE_ROWS - 1) // MOE_ROWS
    block_start = jnp.arange(nblk, dtype=jnp.int32) * MOE_ROWS
    expert_ids = jnp.arange(ne, dtype=jnp.int32)[:, None, None]
    out = None
    for chunk in range(MOE_CHUNKS):
        x1, hmod, top_i, top_w, rank, counts = _merge_call(
            x, y_f, y_b, gg, gf, gr, y_wide, *merge_consts, chunk, MOE_CHUNKS)

        cnt = counts.reshape(ne).astype(jnp.int32)
        padded = (cnt + MOE_ROWS - 1) // MOE_ROWS * MOE_ROWS
        ends = jnp.cumsum(padded)
        starts = ends - padded
        start_of = jnp.sum(jnp.where(top_i[None] == expert_ids, starts[:, None, None], 0), axis=0)
        dest_km = start_of + rank
        block_expert = jnp.minimum(jnp.sum(block_start[:, None] >= ends[None, :], axis=1),
                                   ne - 1).astype(jnp.int32)
        block_valid = jnp.clip((starts + cnt)[block_expert] - block_start, 0, MOE_ROWS)
        block_valid = jnp.where(block_start < ends[-1], block_valid, 0).astype(jnp.int32)

        xs = _sc_scatter_rows(hmod, [dest_km[k] for k in range(TOP_K)], nblk * MOE_ROWS)
        ys = _expert_call(xs, block_expert, block_valid, w_gu[lyr], b_gu[lyr], w_down[lyr],
                          b_down[lyr])
        yg = _sc_gather_rows(ys, dest_km.reshape(TOP_K * n_c)).reshape(TOP_K, n_c, d // 2)
        out = _combine_call(x1, yg, top_w, g2, norm_f.reshape(1, d), chunk, MOE_CHUNKS, out)
    return out
```
